```python
import jax
import jax.numpy as jnp
from jax import lax
import numpy as np

D_MODEL = 1024
BATCH = 32
SEQ = 256
DEPTH = 1
DEC_BATCH = 4
DEC_SEQ = 4096
PAST_LEN = 512

GRID_W = 64
MLA_HEADS = 8
Q_LORA = 384
KV_LORA = 256
QK_NOPE = 64
QK_ROPE = 32
V_HEAD = 64
MLA_W = MLA_HEADS * V_HEAD
MLA_SCALE = (QK_NOPE + QK_ROPE) ** -0.5
HG_HEADS = 8
HG_DK = 64
HG_DV = 64
HG_KW = HG_HEADS * HG_DK
HG_W = HG_HEADS * HG_DV
HG_CHUNK = 32
N_DIR = 2
D_FF = 4 * D_MODEL
Q_BLOCK = 128
ROPE_BASE = 10000.0
EPS = 1e-6
IN_SIZES = (Q_LORA, KV_LORA, QK_ROPE, HG_KW, HG_KW, HG_KW, HG_W, HG_W, D_MODEL, D_MODEL)
IN_OFFSETS = tuple(int(o) for o in np.cumsum(IN_SIZES)[:-1])
N_IN = sum(IN_SIZES)

kernel_name = "hybrid_mla_hgrn2_prefix_dit_step"


def rms_norm(x, g):
    xf = x.astype(jnp.float32)
    y = xf * lax.rsqrt(jnp.mean(xf * xf, axis=-1, keepdims=True) + EPS)
    return (y * g.astype(jnp.float32)).astype(x.dtype)


def axial_rope(n):
    rows = n // GRID_W
    row = jnp.repeat(jnp.arange(rows, dtype=jnp.float32), GRID_W)
    col = jnp.tile(jnp.arange(GRID_W, dtype=jnp.float32), rows)
    half = QK_ROPE // 2
    inv = ROPE_BASE ** (-jnp.arange(0, half, 2, dtype=jnp.float32) / half)
    ang = jnp.concatenate([row[:, None] * inv, col[:, None] * inv], axis=-1)
    return jnp.cos(ang), jnp.sin(ang)


def apply_rope(x, cos, sin):
    x1 = x[..., 0::2].astype(jnp.float32)
    x2 = x[..., 1::2].astype(jnp.float32)
    out = jnp.stack([x1 * cos - x2 * sin, x1 * sin + x2 * cos], axis=-1)
    return out.reshape(x.shape).astype(x.dtype)


def mla_attend(q_nope, q_rope, k_nope, k_rope, v):
    b, nq, h, _ = q_nope.shape
    nb = nq // Q_BLOCK

    def blocks(a):
        return a.reshape(b, nb, Q_BLOCK, h, a.shape[-1]).swapaxes(0, 1)

    def attend_block(qs):
        qn, qr = qs
        s = jnp.einsum('bqhd,bkhd->bhqk', qn, k_nope) + jnp.einsum('bqhr,bkr->bhqk', qr, k_rope)
        p = jax.nn.softmax(s.astype(jnp.float32) * MLA_SCALE, axis=-1).astype(v.dtype)
        return jnp.einsum('bhqk,bkhd->bqhd', p, v)

    o = lax.map(attend_block, (blocks(q_nope), blocks(q_rope)))
    return o.swapaxes(0, 1).reshape(b, nq, h * V_HEAD)


def hgrn2_chunk_scan(q, k, v, log_f, s0):
    b, n, h, _ = q.shape
    dv = v.shape[-1]
    nc = n // HG_CHUNK

    def chunks(a):
        return a.reshape(b, nc, HG_CHUNK, h, a.shape[-1]).transpose(1, 0, 3, 2, 4)

    tri = jnp.tril(jnp.ones((HG_CHUNK, HG_CHUNK), dtype=bool))[:, :, None]

    def step(s, inp):
        qc, kc, vc, gc = inp
        cum = jnp.cumsum(gc, axis=2)
        diff = cum[:, :, :, None, :] - cum[:, :, None, :, :]
        decay = jnp.exp(jnp.where(tri, diff, -jnp.inf))
        scores = jnp.einsum('bhtk,bhsk,bhtsk->bhts', qc, kc, decay)
        o = jnp.einsum('bhts,bhsv->bhtv', scores, vc) + jnp.einsum('bhtk,bhkv->bhtv', qc * jnp.exp(cum), s)
        last = cum[:, :, -1, :]
        s_new = jnp.exp(last)[..., None] * s + jnp.einsum(
            'bhsk,bhsv->bhkv', kc * jnp.exp(last[:, :, None, :] - cum), vc)
        return s_new, o

    s_fin, o = lax.scan(step, s0, (chunks(q), chunks(k), chunks(v), chunks(log_f)))
    return o.transpose(1, 0, 3, 2, 4).reshape(b, n, h, dv), s_fin


def token_mixers(h, lw, rope, ctx):
    (w_in, g_q, w_uq, g_kv, w_ukv, g_hg, lb, w_br_mla, w_br_hg, w_out) = lw
    b, n, _ = h.shape
    f32 = jnp.float32
    (q_lat, kv_lat, k_rope, hq, hf_fwd, hf_bwd, hi, hg, gate_mla, gate_hg) = jnp.split(
        h @ w_in, IN_OFFSETS, axis=-1)

    q = (rms_norm(q_lat, g_q) @ w_uq).reshape(b, n, MLA_HEADS, QK_NOPE + QK_ROPE)
    q_nope, q_rope = q[..., :QK_NOPE], q[..., QK_NOPE:]
    ckv = rms_norm(kv_lat, g_kv)
    kv = (ckv @ w_ukv).reshape(b, n, MLA_HEADS, QK_NOPE + V_HEAD)
    k_nope, v = kv[..., :QK_NOPE], kv[..., QK_NOPE:]
    if ctx is None:
        o_mla = mla_attend(q_nope, q_rope, k_nope, k_rope, v)
        s0_f = jnp.zeros((b, HG_HEADS, HG_DK, HG_DV), f32)
        s0_b = jnp.zeros((b, HG_HEADS, HG_DK, HG_DV), f32)
    else:
        ckv_c, krope_c, state_c = ctx
        cos, sin = rope
        q_rope = apply_rope(q_rope, cos[:, None, :], sin[:, None, :])
        k_rope_lat = apply_rope(k_rope, cos, sin)
        kv_c = (ckv_c @ w_ukv).reshape(b, ckv_c.shape[1], MLA_HEADS, QK_NOPE + V_HEAD)
        o_mla = mla_attend(
            q_nope, q_rope,
            jnp.concatenate([k_nope, kv_c[..., :QK_NOPE]], axis=1),
            jnp.concatenate([k_rope_lat, krope_c], axis=1),
            jnp.concatenate([v, kv_c[..., QK_NOPE:]], axis=1))
        s0_f = state_c[:, 0].astype(f32)
        s0_b = state_c[:, 1].astype(f32)

    qh = (jax.nn.silu(hq.astype(f32)) * HG_DK ** -0.5).reshape(b, n, HG_HEADS, HG_DK)
    vh = hi.astype(f32).reshape(b, n, HG_HEADS, HG_DV)

    def forget(z, lbd):
        f = lbd + (1.0 - lbd) * jax.nn.sigmoid(z.astype(f32))
        return ((1.0 - f).reshape(b, n, HG_HEADS, HG_DK),
                jnp.log(f).reshape(b, n, HG_HEADS, HG_DK))

    k_f, lf_f = forget(hf_fwd, lb[0])
    k_b, lf_b = forget(hf_bwd, lb[1])
    o_f, s_f = hgrn2_chunk_scan(qh, k_f, vh, lf_f, s0_f)
    rev = lambda a: jnp.flip(a, axis=1)
    o_b, s_b = hgrn2_chunk_scan(rev(qh), rev(k_b), rev(vh), rev(lf_b), s0_b)
    o_hg = rms_norm(o_f + rev(o_b), g_hg.reshape(HG_HEADS, HG_DV)).reshape(b, n, HG_W)
    o_hg = o_hg.astype(h.dtype) * jax.nn.silu(hg)

    merged = jax.nn.sigmoid(gate_mla) * (o_mla @ w_br_mla) + jax.nn.sigmoid(gate_hg) * (o_hg @ w_br_hg)
    out = merged @ w_out
    if ctx is None:
        return out, (ckv, k_rope, jnp.stack([s_f, s_b], axis=1))
    return out, None


def trunk_layer(x, mods, g_mix, g_ff, lw, w_ff1, w_ff2, rope, ctx):
    sh1, sc1, ga1, sh2, sc2, ga2 = [m[:, None, :] for m in mods]
    h = rms_norm(x, g_mix) * (1.0 + sc1) + sh1
    mix, state = token_mixers(h, lw, rope, ctx)
    x = x + ga1 * mix
    h = rms_norm(x, g_ff) * (1.0 + sc2) + sh2
    x = x + ga2 * (jnp.square(jax.nn.relu(h @ w_ff1)) @ w_ff2)
    return x, state


def setup_inputs(seed: int = 0) -> dict:
    key = jax.random.key(seed)
    ks = jax.random.split(key, 26)
    f32 = jnp.float32

    def nrm(k, shape, scale):
        return jax.random.normal(k, shape, f32) * scale

    def gain(k, shape):
        return 1.0 + nrm(k, shape, 0.01)

    return {
        "x_prompt": nrm(ks[0], (BATCH, SEQ, D_MODEL), 1.0),
        "x_sample": nrm(ks[1], (DEC_BATCH, DEC_SEQ, D_MODEL), 1.0),
        "cache_ckv": nrm(ks[2], (DEC_BATCH, DEPTH, PAST_LEN, KV_LORA), 1.0),
        "cache_krope": nrm(ks[3], (DEC_BATCH, DEPTH, PAST_LEN, QK_ROPE), 1.0),
        "state_hgrn": nrm(ks[4], (DEC_BATCH, DEPTH, N_DIR, HG_HEADS, HG_DK, HG_DV), 0.3),
        "c": nrm(ks[5], (DEC_BATCH, D_MODEL), 1.0),
        "c_ctx": nrm(ks[6], (D_MODEL,), 1.0),
        "w_ada": nrm(ks[7], (DEPTH, D_MODEL, 6 * D_MODEL), 0.5 * D_MODEL ** -0.5),
        "b_ada": nrm(ks[8], (DEPTH, 6 * D_MODEL), 0.02),
        "g_norm_mix": gain(ks[9], (DEPTH, D_MODEL)),
        "g_norm_ff": gain(ks[10], (DEPTH, D_MODEL)),
        "w_in": nrm(ks[11], (DEPTH, D_MODEL, N_IN), D_MODEL ** -0.5),
        "g_q_norm": gain(ks[12], (DEPTH, Q_LORA)),
        "w_uq": nrm(ks[13], (DEPTH, Q_LORA, MLA_HEADS * (QK_NOPE + QK_ROPE)), Q_LORA ** -0.5),
        "g_kv_norm": gain(ks[14], (DEPTH, KV_LORA)),
        "w_ukv": nrm(ks[15], (DEPTH, KV_LORA, MLA_HEADS * (QK_NOPE + V_HEAD)), KV_LORA ** -0.5),
        "g_hg_norm": gain(ks[16], (DEPTH, HG_W)),
        "hg_lb_logits": nrm(ks[17], (DEPTH + 1, N_DIR, HG_KW), 0.5),
        "w_br_mla": nrm(ks[18], (DEPTH, MLA_W, D_MODEL), MLA_W ** -0.5),
        "w_br_hg": nrm(ks[19], (DEPTH, HG_W, D_MODEL), HG_W ** -0.5),
        "w_out": nrm(ks[20], (DEPTH, D_MODEL, D_MODEL), D_MODEL ** -0.5),
        "w_ff1": nrm(ks[21], (DEPTH, D_MODEL, D_FF), D_MODEL ** -0.5),
        "w_ff2": nrm(ks[22], (DEPTH, D_FF, D_MODEL), D_FF ** -0.5),
        "g_final": gain(ks[23], (D_MODEL,)),
    }


def reference(x_prompt, x_sample, cache_ckv, cache_krope, state_hgrn, c, c_ctx, w_ada, b_ada,
              g_norm_mix, g_norm_ff, w_in, g_q_norm, w_uq, g_kv_norm, w_ukv, g_hg_norm, hg_lb_logits,
              w_br_mla, w_br_hg, w_out, w_ff1, w_ff2, g_final):
    lb_all = jnp.cumsum(jax.nn.softmax(hg_lb_logits.astype(jnp.float32), axis=0), axis=0)
    rope = axial_rope(x_sample.shape[1])
    x_c, x_l = x_prompt, x_sample
    ckv_list, krope_list, st_list = [], [], []
    for l in range(DEPTH):
        lw = (w_in[l], g_q_norm[l], w_uq[l], g_kv_norm[l], w_ukv[l], g_hg_norm[l], lb_all[l],
              w_br_mla[l], w_br_hg[l], w_out[l])
        mods_c = jnp.split(jax.nn.silu(c_ctx[None, :]) @ w_ada[l] + b_ada[l], 6, axis=-1)
        mods_l = jnp.split(jax.nn.silu(c) @ w_ada[l] + b_ada[l], 6, axis=-1)
        x_c, (ckv, krope, st) = trunk_layer(x_c, mods_c, g_norm_mix[l], g_norm_ff[l], lw,
                                            w_ff1[l], w_ff2[l], None, None)
        ckv_list.append(ckv)
        krope_list.append(krope)
        st_list.append(st)
        x_l, _ = trunk_layer(x_l, mods_l, g_norm_mix[l], g_norm_ff[l], lw, w_ff1[l], w_ff2[l], rope,
                             (cache_ckv[:, l], cache_krope[:, l], state_hgrn[:, l]))
    y_prompt = rms_norm(x_c, g_final)
    y_sample = rms_norm(x_l, g_final)
    return (y_prompt, y_sample, jnp.stack(ckv_list, axis=1), jnp.stack(krope_list, axis=1),
            jnp.stack(st_list, axis=1))
```

```python
import functools

import numpy as np
import jax
import jax.numpy as jnp
from jax import lax
from jax.experimental import pallas as pl
from jax.experimental.pallas import tpu as pltpu

D_MODEL = 1024
GRID_W = 64
MLA_HEADS = 8
Q_LORA = 384
KV_LORA = 256
QK_NOPE = 64
QK_ROPE = 32
V_HEAD = 64
MLA_W = MLA_HEADS * V_HEAD
MLA_SCALE = (QK_NOPE + QK_ROPE) ** -0.5
HG_HEADS = 8
HG_DK = 64
HG_DV = 64
HG_KW = HG_HEADS * HG_DK
HG_W = HG_HEADS * HG_DV
D_FF = 4 * D_MODEL
ROPE_BASE = 10000.0
EPS = 1e-6

LANES = 128
HEAD_PAD = LANES
QK_PAD_W = MLA_HEADS * HEAD_PAD
N_PAIRS = HG_HEADS // 2
VMEM_LIMIT = 56 * 1024 * 1024

_SEG_SIZES = (Q_LORA, KV_LORA, LANES, HG_KW, HG_KW, HG_KW, HG_W, HG_W, D_MODEL, D_MODEL)
_SEG_OFFS = tuple(int(o) for o in np.cumsum((0,) + _SEG_SIZES))
N_IN_PAD = _SEG_OFFS[-1]

HG_CHUNK = 128
HG_LEVELS = 7

F32 = jnp.float32
BF16 = jnp.bfloat16


def _cparams(sem):
    return pltpu.CompilerParams(dimension_semantics=sem, vmem_limit_bytes=VMEM_LIMIT)


def _rms(x, g):
    return x * lax.rsqrt(jnp.mean(x * x, axis=-1, keepdims=True) + EPS) * g


def _dot(a, b):
    return jnp.dot(a, b, preferred_element_type=F32)


def _dot_nt(a, b):
    return lax.dot_general(a, b, (((1,), (1,)), ((), ())), preferred_element_type=F32)


def _dot_tn(a, b):
    return lax.dot_general(a, b, (((0,), (0,)), ((), ())), preferred_element_type=F32)


def _split_hi_lo(x):
    hi = x.astype(BF16)
    lo = (x - hi.astype(F32)).astype(BF16)
    return hi, lo


def _ada_kernel(c_ref, w_ref, b_ref, o_ref):
    c = c_ref[...]
    a = c * jax.nn.sigmoid(c)
    a_hi, a_lo = _split_hi_lo(a)
    w_hi, w_lo = _split_hi_lo(w_ref[...])
    o_ref[...] = _dot(a_hi, w_hi) + _dot(a_hi, w_lo) + _dot(a_lo, w_hi) + b_ref[...]


def _ada(cc, w_ada, b_ada):
    rows, tn = cc.shape[0], 1536
    n = w_ada.shape[1]
    return pl.pallas_call(
        _ada_kernel,
        grid=(n // tn,),
        in_specs=[pl.BlockSpec((rows, D_MODEL), lambda j: (0, 0)),
                  pl.BlockSpec((D_MODEL, tn), lambda j: (0, j)),
                  pl.BlockSpec((1, tn), lambda j: (0, j))],
        out_specs=pl.BlockSpec((rows, tn), lambda j: (0, j)),
        out_shape=jax.ShapeDtypeStruct((rows, n), F32),
        compiler_params=_cparams(("arbitrary",)),
        name="ada",
    )(cc, w_ada, b_ada)


def _rope_tile(blk, cos, sin, even):
    rot = jnp.where(even, -pltpu.roll(blk, LANES - 1, 1), pltpu.roll(blk, 1, 1))
    return blk * cos + rot * sin


def _inproj_kernel(*refs, rope, cache_out):
    (x_ref, sh_ref, sc_ref, gmix_ref, win_ref, gq_ref, wuq_ref, gkv_ref, wukv_ref, lbl_ref) = refs[:10]
    refs = refs[10:]
    if rope:
        cos_ref, sin_ref = refs[:2]
        refs = refs[2:]
    (q_ref, k_ref, v_ref, qh_ref, kf_ref, lff_ref, kb_ref, lfb_ref, vh_ref, sg_ref, gm_ref, gh_ref) = refs[:12]
    refs = refs[12:]

    x = x_ref[0]
    h = _rms(x, gmix_ref[...]) * (1.0 + sc_ref[0]) + sh_ref[0]
    hb = h.astype(BF16)

    def proj(i):
        return _dot(hb, win_ref[:, _SEG_OFFS[i]:_SEG_OFFS[i + 1]])

    if rope:
        cos = cos_ref[...]
        sin = sin_ref[...]
        even = (lax.broadcasted_iota(jnp.int32, cos.shape, 1) & 1) == 0

    qn = _rms(proj(0), gq_ref[...]).astype(BF16)
    q = _dot(qn, wuq_ref[...]) * MLA_SCALE
    for hh in range(MLA_HEADS):
        blk = q[:, hh * HEAD_PAD:(hh + 1) * HEAD_PAD]
        if rope:
            blk = _rope_tile(blk, cos, sin, even)
        q_ref[0, :, hh * HEAD_PAD:(hh + 1) * HEAD_PAD] = blk.astype(BF16)

    ckv = _rms(proj(1), gkv_ref[...])
    kr = proj(2)
    if cache_out:
        ckv_ref, krope_ref = refs
        ckv_ref[0] = ckv
        krope_ref[0] = kr[:, :QK_ROPE]
    if rope:
        kr = _rope_tile(kr, cos, sin, even)
    kv = _dot(ckv.astype(BF16), wukv_ref[...])
    for hh in range(MLA_HEADS):
        k_ref[0, :, hh * HEAD_PAD:(hh + 1) * HEAD_PAD] = (kv[:, hh * HEAD_PAD:(hh + 1) * HEAD_PAD] + kr).astype(BF16)
    v_ref[0] = kv[:, QK_PAD_W:].astype(BF16)

    hq = proj(3)
    qh_ref[0] = hq * jax.nn.sigmoid(hq) * (HG_DK ** -0.5)
    l0, l1 = lbl_ref[0], lbl_ref[1]
    lmax = jnp.maximum(l0, l1)
    e0, e1 = jnp.exp(l0 - lmax), jnp.exp(l1 - lmax)
    lb = e0 / (e0 + e1)
    for d, (k_out, lf_out) in enumerate(((kf_ref, lff_ref), (kb_ref, lfb_ref))):
        lbd = lb[d:d + 1]
        f = lbd + (1.0 - lbd) * jax.nn.sigmoid(proj(4 + d))
        k_out[0] = 1.0 - f
        lf_out[0] = jnp.log(f)
    vh_ref[0] = proj(6)
    hg = proj(7)
    sg_ref[0] = hg * jax.nn.sigmoid(hg)
    gm_ref[0] = jax.nn.sigmoid(proj(8))
    gh_ref[0] = jax.nn.sigmoid(proj(9))


def _const_spec(shape):
    return pl.BlockSpec(shape, lambda *_: (0,) * len(shape))


def _inproj(x, mods, mod_row, wts, rope_tabs, cache_out, tm):
    bsz, n, _ = x.shape
    rope = rope_tabs is not None
    w_in_p, g_mix, g_q, w_uq_p, g_kv, w_ukv_p, lb_logits = wts

    def tok(width):
        return pl.BlockSpec((1, tm, width), lambda b, i: (b, i, 0))

    def mod(col):
        return pl.BlockSpec((1, 1, D_MODEL), lambda b, i: (mod_row(b), 0, col))

    in_specs = [tok(D_MODEL), mod(0), mod(1), _const_spec((1, D_MODEL)), _const_spec(w_in_p.shape),
                _const_spec((1, Q_LORA)), _const_spec(w_uq_p.shape), _const_spec((1, KV_LORA)),
                _const_spec(w_ukv_p.shape), _const_spec(lb_logits.shape)]
    args = [x, mods, mods, g_mix, w_in_p, g_q, w_uq_p, g_kv, w_ukv_p, lb_logits]
    if rope:
        in_specs += [pl.BlockSpec((tm, LANES), lambda b, i: (i, 0))] * 2
        args += list(rope_tabs)

    widths = [(QK_PAD_W, BF16), (QK_PAD_W, BF16), (MLA_W, BF16)] + [(HG_KW, F32)] * 7 + [(D_MODEL, F32)] * 2
    if cache_out:
        widths += [(KV_LORA, F32), (QK_ROPE, F32)]
    out_specs = [tok(w) for w, _ in widths]
    out_shape = [jax.ShapeDtypeStruct((bsz, n, w), dt) for w, dt in widths]
    return pl.pallas_call(
        functools.partial(_inproj_kernel, rope=rope, cache_out=cache_out),
        grid=(bsz, n // tm),
        in_specs=in_specs, out_specs=out_specs, out_shape=out_shape,
        compiler_params=_cparams(("parallel", "parallel")),
        name="inproj_rope" if rope else "inproj",
    )(*args)


def _cache_kv_kernel(ckv_ref, kr_ref, wukv_ref, k_ref, v_ref):
    kv = _dot(ckv_ref[0].astype(BF16), wukv_ref[...])
    kr = kr_ref[0]
    for hh in range(MLA_HEADS):
        k_ref[0, :, hh * HEAD_PAD:(hh + 1) * HEAD_PAD] = (kv[:, hh * HEAD_PAD:(hh + 1) * HEAD_PAD] + kr).astype(BF16)
    v_ref[0] = kv[:, QK_PAD_W:].astype(BF16)


def _cache_kv(ckv_c, krope_c_pad, w_ukv_p):
    bsz, n, _ = ckv_c.shape
    return pl.pallas_call(
        _cache_kv_kernel,
        grid=(bsz,),
        in_specs=[pl.BlockSpec((1, n, KV_LORA), lambda b: (b, 0, 0)),
                  pl.BlockSpec((1, n, LANES), lambda b: (b, 0, 0)),
                  _const_spec(w_ukv_p.shape)],
        out_specs=[pl.BlockSpec((1, n, QK_PAD_W), lambda b: (b, 0, 0)),
                   pl.BlockSpec((1, n, MLA_W), lambda b: (b, 0, 0))],
        out_shape=[jax.ShapeDtypeStruct((bsz, n, QK_PAD_W), BF16),
                   jax.ShapeDtypeStruct((bsz, n, MLA_W), BF16)],
        compiler_params=_cparams(("parallel",)),
        name="cache_kv",
    )(ckv_c, krope_c_pad, w_ukv_p)


def _attn_kernel(q_ref, k_ref, v_ref, o_ref, m_scr, l_scr, acc_scr):
    j = pl.program_id(2)

    @pl.when(j == 0)
    def _():
        m_scr[...] = jnp.full(m_scr.shape, -jnp.inf, F32)
        l_scr[...] = jnp.zeros(l_scr.shape, F32)
        acc_scr[...] = jnp.zeros(acc_scr.shape, F32)

    for hh in range(MLA_HEADS):
        q = q_ref[0, :, hh * HEAD_PAD:(hh + 1) * HEAD_PAD]
        k = k_ref[0, :, hh * HEAD_PAD:(hh + 1) * HEAD_PAD]
        s = _dot_nt(q, k)
        m_prev = m_scr[hh]
        m_cur = jnp.maximum(m_prev, jnp.max(s, axis=-1, keepdims=True))
        alpha = jnp.exp(m_prev - m_cur)
        p = jnp.exp(s - m_cur[:, :1])
        l_scr[hh] = alpha * l_scr[hh] + jnp.sum(p, axis=-1, keepdims=True)
        m_scr[hh] = m_cur
        pair = hh // 2
        pv = _dot(p.astype(BF16), v_ref[0, :, pair * LANES:(pair + 1) * LANES])
        acc_scr[hh] = alpha * acc_scr[hh] + pv

    @pl.when(j == pl.num_programs(2) - 1)
    def _():
        low = lax.broadcasted_iota(jnp.int32, acc_scr.shape[1:], 1) < V_HEAD
        for pair in range(MLA_HEADS // 2):
            a = acc_scr[2 * pair] / l_scr[2 * pair]
            b = acc_scr[2 * pair + 1] / l_scr[2 * pair + 1]
            o_ref[0, :, pair * LANES:(pair + 1) * LANES] = jnp.where(low, a, b).astype(o_ref.dtype)


def _attention(q, k, v, bq, bk):
    bsz, nq, _ = q.shape
    nk = k.shape[1]
    return pl.pallas_call(
        _attn_kernel,
        grid=(bsz, nq // bq, nk // bk),
        in_specs=[pl.BlockSpec((1, bq, QK_PAD_W), lambda b, i, j: (b, i, 0)),
                  pl.BlockSpec((1, bk, QK_PAD_W), lambda b, i, j: (b, j, 0)),
                  pl.BlockSpec((1, bk, MLA_W), lambda b, i, j: (b, j, 0))],
        out_specs=pl.BlockSpec((1, bq, MLA_W), lambda b, i, j: (b, i, 0)),
        out_shape=jax.ShapeDtypeStruct((bsz, nq, MLA_W), BF16),
        scratch_shapes=[pltpu.VMEM((MLA_HEADS, bq, LANES), F32),
                        pltpu.VMEM((MLA_HEADS, bq, LANES), F32),
                        pltpu.VMEM((MLA_HEADS, bq, LANES), F32)],
        compiler_params=_cparams(("parallel", "parallel", "arbitrary")),
        name="attn",
    )(q, k, v)


def _hgrn_constants(reverse):
    c = HG_CHUNK
    t = np.arange(c)
    mats = [(t[None, :] <= t[:, None])]
    qside = [np.ones(c, bool)]
    kside = [np.ones(c, bool)]
    for lvl in range(1, HG_LEVELS + 1):
        g, half = 1 << lvl, 1 << (lvl - 1)
        p = t % g
        mid = t - p + half - 1
        isq = p >= half
        u = t[None, :]
        m = np.where(isq[:, None], (u > mid[:, None]) & (u <= t[:, None]), (u > t[:, None]) & (u <= mid[:, None]))
        mats.append(m)
        qside.append(isq)
        kside.append(~isq)
    mstack = np.stack(mats).astype(np.float32)
    qside = np.stack(qside).astype(np.float32)
    kside = np.stack(kside).astype(np.float32)
    if reverse:
        mstack = mstack[:, ::-1, ::-1]
        qside = qside[:, ::-1]
        kside = kside[:, ::-1]
    lane_low = (np.arange(LANES) < HG_DK).astype(np.float32)
    qm_even = qside[:, :, None] * lane_low[None, None, :]
    qm_odd = qside[:, :, None] * (1.0 - lane_low)[None, None, :]
    km = np.broadcast_to(kside[:, :, None], qm_even.shape)
    masks = np.stack([qm_even, qm_odd, km], axis=1)
    return (jnp.asarray(mstack.reshape(-1, c), BF16), jnp.asarray(np.ascontiguousarray(masks), F32))


def _hgrn_kernel(q_ref, k_ref, lf_ref, v_ref, s0_ref, mstack_ref, masks_ref, o_ref, sfin_ref, e_scr, st_scr, *, reverse):
    c = HG_CHUNK
    step = pl.program_id(1)

    @pl.when(step == 0)
    def _():
        st_scr[...] = s0_ref[0]

    lf_hi, lf_lo = _split_hi_lo(lf_ref[0])
    mstack = mstack_ref[...]
    e_scr[...] = _dot(mstack, lf_hi) + _dot(mstack, lf_lo)

    last = 0 if reverse else c - 1
    row = lax.broadcasted_iota(jnp.int32, (c, c), 0)
    col = lax.broadcasted_iota(jnp.int32, (c, c), 1)
    xor = row ^ col
    lane_low = lax.broadcasted_iota(jnp.int32, (c, LANES), 1) < HG_DK
    diag_block = (lax.broadcasted_iota(jnp.int32, (LANES, LANES), 0) < HG_DV) == (
        lax.broadcasted_iota(jnp.int32, (LANES, LANES), 1) < HG_DK)

    for pair in range(N_PAIRS):
        sl = slice(pair * LANES, (pair + 1) * LANES)
        q = q_ref[0, :, sl]
        k = k_ref[0, :, sl]
        v = v_ref[0, :, sl]
        a_even = a_odd = None
        for lvl in range(HG_LEVELS, -1, -1):
            if lvl == 0:
                qz, kz = q, k
            else:
                z = jnp.exp(e_scr[lvl * c:(lvl + 1) * c, sl])
                qz, kz = q * z, k * z
            kt = (kz * masks_ref[lvl, 2]).astype(BF16)
            p_even = _dot_nt((qz * masks_ref[lvl, 0]).astype(BF16), kt)
            p_odd = _dot_nt((qz * masks_ref[lvl, 1]).astype(BF16), kt)
            if a_even is None:
                a_even, a_odd = p_even, p_odd
            else:
                same = xor < (1 << lvl)
                a_even = jnp.where(same, p_even, a_even)
                a_odd = jnp.where(same, p_odd, a_odd)
        cum = e_scr[0:c, sl]
        cum_last = cum[last:last + 1]
        a_cat = jnp.concatenate([a_even, a_odd], axis=1).astype(BF16)
        v_cat = jnp.concatenate([jnp.where(lane_low, v, 0.0), jnp.where(lane_low, 0.0, v)], axis=0).astype(BF16)
        st = st_scr[pair]
        o = _dot(a_cat, v_cat) + _dot_nt((q * jnp.exp(cum)).astype(BF16), st.astype(BF16))
        o_ref[0, :, sl] = o
        kd = (k * jnp.exp(cum_last - cum)).astype(BF16)
        upd = _dot_tn(v.astype(BF16), kd)
        st_scr[pair] = jnp.exp(cum_last) * st + jnp.where(diag_block, upd, 0.0)

    @pl.when(step == pl.num_programs(1) - 1)
    def _():
        sfin_ref[0] = st_scr[...]


def _hgrn(qh, kk, lf, vh, s0t, reverse):
    bsz, n, _ = qh.shape
    c = HG_CHUNK
    nc = n // c
    mstack, masks = _hgrn_constants(reverse)
    if reverse:
        tok = pl.BlockSpec((1, c, HG_KW), lambda b, i: (b, nc - 1 - i, 0))
    else:
        tok = pl.BlockSpec((1, c, HG_KW), lambda b, i: (b, i, 0))
    st_spec = pl.BlockSpec((1, N_PAIRS, LANES, LANES), lambda b, i: (b, 0, 0, 0))
    return pl.pallas_call(
        functools.partial(_hgrn_kernel, reverse=reverse),
        grid=(bsz, nc),
        in_specs=[tok, tok, tok, tok, st_spec, _const_spec(mstack.shape), _const_spec(masks.shape)],
        out_specs=[tok, st_spec],
        out_shape=[jax.ShapeDtypeStruct((bsz, n, HG_W), F32),
                   jax.ShapeDtypeStruct((bsz, N_PAIRS, LANES, LANES), F32)],
        scratch_shapes=[pltpu.VMEM(((HG_LEVELS + 1) * c, HG_KW), F32),
                        pltpu.VMEM((N_PAIRS, LANES, LANES), F32)],
        compiler_params=_cparams(("parallel", "arbitrary")),
        name="hgrn_bwd" if reverse else "hgrn_fwd",
    )(qh, kk, lf, vh, s0t, mstack, masks)


def _state_to_pairs(s):
    b = s.shape[0]
    st = jnp.swapaxes(s, -1, -2).reshape(b, N_PAIRS, 2, HG_DV, HG_DK)
    eye = jnp.eye(2, dtype=s.dtype)
    out = jnp.einsum('bpavk,ac->bpavck', st, eye)
    return out.reshape(b, N_PAIRS, 2 * HG_DV, 2 * HG_DK)


def _pairs_to_state(sp):
    b = sp.shape[0]
    x = sp.reshape(b, N_PAIRS, 2, HG_DV, 2, HG_DK)
    diag = jnp.stack([x[:, :, 0, :, 0, :], x[:, :, 1, :, 1, :]], axis=2)
    return jnp.swapaxes(diag.reshape(b, HG_HEADS, HG_DV, HG_DK), -1, -2)


def _merge_kernel(x_ref, ga_ref, om_ref, of_ref, ob_ref, sg_ref, gm_ref, gh_ref, ghg_ref, hmean_ref,
                  wbm_ref, wbh_ref, wout_ref, x1_ref):
    o = of_ref[0] + ob_ref[0]
    sq_hi, sq_lo = _split_hi_lo(o * o)
    ms = _dot(sq_hi, hmean_ref[...]) + _dot(sq_lo, hmean_ref[...])
    o_hg = (o * lax.rsqrt(ms + EPS) * ghg_ref[...]) * sg_ref[0]
    merged = gm_ref[0] * _dot(om_ref[0], wbm_ref[...]) + gh_ref[0] * _dot(o_hg.astype(BF16), wbh_ref[...])
    out = _dot(merged.astype(BF16), wout_ref[...])
    x1_ref[0] = x_ref[0] + ga_ref[0] * out


def _merge(x, mods, mod_row, o_mla, o_f, o_b, sg, gm, gh, wts, tm):
    bsz, n, _ = x.shape
    g_hg, hmean, w_br_mla, w_br_hg, w_out = wts

    def tok(width):
        return pl.BlockSpec((1, tm, width), lambda b, i: (b, i, 0))

    return pl.pallas_call(
        _merge_kernel,
        grid=(bsz, n // tm),
        in_specs=[tok(D_MODEL), pl.BlockSpec((1, 1, D_MODEL), lambda b, i: (mod_row(b), 0, 2)),
                  tok(MLA_W), tok(HG_W), tok(HG_W), tok(HG_W), tok(D_MODEL), tok(D_MODEL),
                  _const_spec((1, HG_W)), _const_spec(hmean.shape), _const_spec(w_br_mla.shape),
                  _const_spec(w_br_hg.shape), _const_spec(w_out.shape)],
        out_specs=tok(D_MODEL),
        out_shape=jax.ShapeDtypeStruct((bsz, n, D_MODEL), F32),
        compiler_params=_cparams(("parallel", "parallel")),
        name="merge",
    )(x, mods, o_mla, o_f, o_b, sg, gm, gh, g_hg, hmean, w_br_mla, w_br_hg, w_out)


def _ffn_kernel(x_ref, sh_ref, sc_ref, ga_ref, gff_ref, w1_ref, w2_ref, gfin_ref, y_ref, acc_scr, h_scr):
    j = pl.program_id(2)

    @pl.when(j == 0)
    def _():
        h = _rms(x_ref[0], gff_ref[...]) * (1.0 + sc_ref[0]) + sh_ref[0]
        h_scr[...] = h.astype(BF16)
        acc_scr[...] = jnp.zeros(acc_scr.shape, F32)

    a = jnp.maximum(_dot(h_scr[...], w1_ref[...]), 0.0)
    acc_scr[...] += _dot((a * a).astype(BF16), w2_ref[...])

    @pl.when(j == pl.num_programs(2) - 1)
    def _():
        x2 = x_ref[0] + ga_ref[0] * acc_scr[...]
        y_ref[0] = _rms(x2, gfin_ref[...])


def _ffn(x1, mods, mod_row, wts, tm, tf):
    bsz, n, _ = x1.shape
    g_ff, w_ff1, w_ff2, g_final = wts

    def mod(col):
        return pl.BlockSpec((1, 1, D_MODEL), lambda b, i, j: (mod_row(b), 0, col))

    tok = pl.BlockSpec((1, tm, D_MODEL), lambda b, i, j: (b, i, 0))
    return pl.pallas_call(
        _ffn_kernel,
        grid=(bsz, n // tm, D_FF // tf),
        in_specs=[tok, mod(3), mod(4), mod(5), _const_spec((1, D_MODEL)),
                  pl.BlockSpec((D_MODEL, tf), lambda b, i, j: (0, j)),
                  pl.BlockSpec((tf, D_MODEL), lambda b, i, j: (j, 0)),
                  _const_spec((1, D_MODEL))],
        out_specs=tok,
        out_shape=jax.ShapeDtypeStruct((bsz, n, D_MODEL), F32),
        scratch_shapes=[pltpu.VMEM((tm, D_MODEL), F32), pltpu.VMEM((tm, D_MODEL), BF16)],
        compiler_params=_cparams(("parallel", "parallel", "arbitrary")),
        name="ffn",
    )(x1, mods, mods, mods, g_ff, w_ff1, w_ff2, g_final)


def _pad_head_cols(w, widths_in, layout):
    rows = w.shape[0]
    w = w.reshape(rows, MLA_HEADS, sum(width for _, width in widths_in))
    pieces, off = {}, 0
    for name, width in widths_in:
        pieces[name] = w[:, :, off:off + width]
        off += width
    cols = [pieces[name] if name in pieces else jnp.zeros((rows, MLA_HEADS, width), w.dtype) for name, width in layout]
    return jnp.concatenate(cols, axis=-1).reshape(rows, -1)


def _prep_weights(w_in, w_uq, w_ukv):
    segs = list(jnp.split(w_in, np.cumsum((Q_LORA, KV_LORA, QK_ROPE, HG_KW, HG_KW, HG_KW, HG_W, HG_W, D_MODEL)), axis=1))
    segs[2] = jnp.pad(segs[2], ((0, 0), (0, LANES - QK_ROPE)))
    w_in_p = jnp.concatenate(segs, axis=1).astype(BF16)
    w_uq_p = _pad_head_cols(w_uq, (("nope", QK_NOPE), ("rope", QK_ROPE)),
                            (("rope", QK_ROPE), ("zero", HEAD_PAD - QK_ROPE - QK_NOPE), ("nope", QK_NOPE))).astype(BF16)
    w_ukv_r = w_ukv.reshape(KV_LORA, MLA_HEADS, QK_NOPE + V_HEAD)
    k_cols = jnp.concatenate([jnp.zeros((KV_LORA, MLA_HEADS, HEAD_PAD - QK_NOPE), w_ukv.dtype), w_ukv_r[:, :, :QK_NOPE]], axis=-1)
    w_ukv_p = jnp.concatenate([k_cols.reshape(KV_LORA, QK_PAD_W), w_ukv_r[:, :, QK_NOPE:].reshape(KV_LORA, MLA_W)], axis=1)
    return w_in_p, w_uq_p, w_ukv_p.astype(BF16)


def _rope_tables(n):
    rows = n // GRID_W
    row = jnp.repeat(jnp.arange(rows, dtype=F32), GRID_W)
    col = jnp.tile(jnp.arange(GRID_W, dtype=F32), rows)
    half = QK_ROPE // 2
    inv = ROPE_BASE ** (-jnp.arange(0, half, 2, dtype=F32) / half)
    ang = jnp.concatenate([row[:, None] * inv, col[:, None] * inv], axis=-1)
    cos = jnp.repeat(jnp.cos(ang), 2, axis=-1)
    sin = jnp.repeat(jnp.sin(ang), 2, axis=-1)
    cos = jnp.concatenate([cos, jnp.ones((n, LANES - QK_ROPE), F32)], axis=-1)
    sin = jnp.concatenate([sin, jnp.zeros((n, LANES - QK_ROPE), F32)], axis=-1)
    return cos, sin


def _trunk(x, mods, mod_row, wts, rope_tabs, ctx):
    (inproj_w, merge_w, ffn_w, w_ukv_p) = wts
    bsz, n, _ = x.shape
    tm = 256
    outs = _inproj(x, mods, mod_row, inproj_w, rope_tabs, ctx is None, tm)
    q, k, v, qh, kf, lff, kb, lfb, vh, sg, gm, gh = outs[:12]
    if ctx is None:
        ckv, krope = outs[12:]
        s0f = s0b = jnp.zeros((bsz, N_PAIRS, LANES, LANES), F32)
        bk = n
    else:
        ckv_c, krope_c, state_c = ctx
        kr_pad = jnp.pad(krope_c, ((0, 0), (0, 0), (0, LANES - QK_ROPE)))
        k_c, v_c = _cache_kv(ckv_c, kr_pad, w_ukv_p)
        k = jnp.concatenate([k, k_c], axis=1)
        v = jnp.concatenate([v, v_c], axis=1)
        s0f = _state_to_pairs(state_c[:, 0])
        s0b = _state_to_pairs(state_c[:, 1])
        bk = 512
    o_mla = _attention(q, k, v, min(n, 512), bk)
    o_f, sf = _hgrn(qh, kf, lff, vh, s0f, False)
    o_b, sb = _hgrn(qh, kb, lfb, vh, s0b, True)
    x1 = _merge(x, mods, mod_row, o_mla, o_f, o_b, sg, gm, gh, merge_w, tm)
    if ctx is None:
        x1 = x1.reshape(1, bsz * n, D_MODEL)
    y = _ffn(x1, mods, mod_row, ffn_w, min(1024, x1.shape[1]), 1024).reshape(bsz, n, D_MODEL)
    if ctx is None:
        state = jnp.stack([_pairs_to_state(sf), _pairs_to_state(sb)], axis=1)
        return y, (ckv, krope, state)
    return y, None


def kernel(x_prompt, x_sample, cache_ckv, cache_krope, state_hgrn, c, c_ctx, w_ada, b_ada, g_norm_mix, g_norm_ff, w_in, g_q_norm, w_uq, g_kv_norm, w_ukv, g_hg_norm, hg_lb_logits, w_br_mla, w_br_hg, w_out, w_ff1, w_ff2, g_final):
    assert w_in.shape[0] == 1, "single-layer trunk"
    dec_b = c.shape[0]
    cc = jnp.concatenate([c, c_ctx[None, :], jnp.zeros((8 - dec_b - 1, D_MODEL), F32)], axis=0)
    mods = _ada(cc, w_ada[0], b_ada[0][None, :]).reshape(8, 1, 6 * D_MODEL)

    w_in_p, w_uq_p, w_ukv_p = _prep_weights(w_in[0], w_uq[0], w_ukv[0])
    inproj_w = (w_in_p, g_norm_mix[0][None], g_q_norm[0][None], w_uq_p, g_kv_norm[0][None], w_ukv_p, hg_lb_logits)
    head_id = np.arange(HG_W) // HG_DV
    hmean = jnp.asarray((head_id[:, None] == head_id[None, :]).astype(np.float32) / HG_DV, BF16)
    merge_w = (g_hg_norm[0][None], hmean, w_br_mla[0].astype(BF16), w_br_hg[0].astype(BF16), w_out[0].astype(BF16))
    ffn_w = (g_norm_ff[0][None], w_ff1[0].astype(BF16), w_ff2[0].astype(BF16), g_final[None])
    wts = (inproj_w, merge_w, ffn_w, w_ukv_p)

    y_prompt, (ckv, krope, state) = _trunk(x_prompt, mods, lambda b: dec_b, wts, None, None)
    rope_tabs = _rope_tables(x_sample.shape[1])
    y_sample, _ = _trunk(x_sample, mods, lambda b: b, wts, rope_tabs,
                         (cache_ckv[:, 0], cache_krope[:, 0], state_hgrn[:, 0]))
    return (y_prompt, y_sample, ckv[:, None], krope[:, None], state[:, None])
```

```python
import functools

import numpy as np
import jax
import jax.numpy as jnp
from jax import lax
from jax.experimental import pallas as pl
from jax.experimental.pallas import tpu as pltpu

D_MODEL = 1024
GRID_W = 64
MLA_HEADS = 8
Q_LORA = 384
KV_LORA = 256
QK_NOPE = 64
QK_ROPE = 32
V_HEAD = 64
MLA_W = MLA_HEADS * V_HEAD
MLA_SCALE = (QK_NOPE + QK_ROPE) ** -0.5
LOG2_E = 1.4426950408889634
HG_HEADS = 8
HG_DK = 64
HG_DV = 64
HG_KW = HG_HEADS * HG_DK
HG_W = HG_HEADS * HG_DV
D_FF = 4 * D_MODEL
ROPE_BASE = 10000.0
EPS = 1e-6

LANES = 128
HEAD_PAD = LANES
QK_PAD_W = MLA_HEADS * HEAD_PAD
N_PAIRS = HG_HEADS // 2
VMEM_LIMIT = 56 * 1024 * 1024

_SEG_SIZES = (Q_LORA, KV_LORA, LANES, HG_KW, HG_KW, HG_KW, HG_W, HG_W, D_MODEL, D_MODEL)
_SEG_OFFS = tuple(int(o) for o in np.cumsum((0,) + _SEG_SIZES))
N_IN_PAD = _SEG_OFFS[-1]

HG_CHUNK = 128
HG_LEVELS = 7

F32 = jnp.float32
BF16 = jnp.bfloat16


def _cparams(sem):
    return pltpu.CompilerParams(dimension_semantics=sem, vmem_limit_bytes=VMEM_LIMIT)


def _rms(x, g):
    return x * lax.rsqrt(jnp.mean(x * x, axis=-1, keepdims=True) + EPS) * g


def _dot(a, b):
    return jnp.dot(a, b, preferred_element_type=F32)


def _dot_nt(a, b):
    return lax.dot_general(a, b, (((1,), (1,)), ((), ())), preferred_element_type=F32)


def _dot_tn(a, b):
    return lax.dot_general(a, b, (((0,), (0,)), ((), ())), preferred_element_type=F32)


def _split_hi_lo(x):
    hi = x.astype(BF16)
    lo = (x - hi.astype(F32)).astype(BF16)
    return hi, lo


def _ada_kernel(c_ref, w_ref, b_ref, o_ref):
    c = c_ref[...]
    a = c * jax.nn.sigmoid(c)
    a_hi, a_lo = _split_hi_lo(a)
    w_hi, w_lo = _split_hi_lo(w_ref[...])
    o_ref[...] = _dot(a_hi, w_hi) + _dot(a_hi, w_lo) + _dot(a_lo, w_hi) + b_ref[...]


def _ada(cc, w_ada, b_ada):
    rows, tn = cc.shape[0], 1536
    n = w_ada.shape[1]
    return pl.pallas_call(
        _ada_kernel,
        grid=(n // tn,),
        in_specs=[pl.BlockSpec((rows, D_MODEL), lambda j: (0, 0)),
                  pl.BlockSpec((D_MODEL, tn), lambda j: (0, j)),
                  pl.BlockSpec((1, tn), lambda j: (0, j))],
        out_specs=pl.BlockSpec((rows, tn), lambda j: (0, j)),
        out_shape=jax.ShapeDtypeStruct((rows, n), F32),
        compiler_params=_cparams(("arbitrary",)),
        name="ada",
    )(cc, w_ada, b_ada)


def _rope_tile(blk, cos, sin, even):
    rot = jnp.where(even, -pltpu.roll(blk, LANES - 1, 1), pltpu.roll(blk, 1, 1))
    return blk * cos + rot * sin


def _store_kv(ckv, kr, wukv_ref, k_ref, v_ref):
    kv = _dot(ckv.astype(BF16), wukv_ref[...])
    one_lane = ((lax.broadcasted_iota(jnp.int32, (1, QK_PAD_W), 1) & (HEAD_PAD - 1)) == V_HEAD).astype(F32)
    for hh in range(MLA_HEADS):
        sl = slice(hh * HEAD_PAD, (hh + 1) * HEAD_PAD)
        k_ref[0, :, sl] = (kv[:, sl] + kr).astype(BF16)
    v_ref[0] = (kv[:, QK_PAD_W:] + one_lane).astype(BF16)


def _inproj_kernel(*refs, rope, cache_out):
    (x_ref, sh_ref, sc_ref, gmix_ref, win_ref, gq_ref, wuq_ref, gkv_ref, wukv_ref, lbl_ref) = refs[:10]
    refs = refs[10:]
    if rope:
        cos_ref, sin_ref = refs[:2]
        refs = refs[2:]
    (q_ref, k_ref, v_ref, qh_ref, kf_ref, lff_ref, kb_ref, lfb_ref, vh_ref, sg_ref, gm_ref, gh_ref) = refs[:12]
    refs = refs[12:]

    x = x_ref[0]
    h = _rms(x, gmix_ref[...]) * (1.0 + sc_ref[0]) + sh_ref[0]
    hb = h.astype(BF16)

    def proj(i):
        return _dot(hb, win_ref[:, _SEG_OFFS[i]:_SEG_OFFS[i + 1]])

    if rope:
        cos = cos_ref[...]
        sin = sin_ref[...]
        even = (lax.broadcasted_iota(jnp.int32, cos.shape, 1) & 1) == 0

    qn = _rms(proj(0), gq_ref[...]).astype(BF16)
    q = _dot(qn, wuq_ref[...]) * (MLA_SCALE * LOG2_E)
    for hh in range(MLA_HEADS):
        blk = q[:, hh * HEAD_PAD:(hh + 1) * HEAD_PAD]
        if rope:
            blk = _rope_tile(blk, cos, sin, even)
        q_ref[0, :, hh * HEAD_PAD:(hh + 1) * HEAD_PAD] = blk.astype(BF16)

    ckv = _rms(proj(1), gkv_ref[...])
    kr = proj(2)
    if cache_out:
        ckv_ref, krope_ref = refs
        ckv_ref[0] = ckv
        krope_ref[0] = kr[:, :QK_ROPE]
    if rope:
        kr = _rope_tile(kr, cos, sin, even)
    _store_kv(ckv, kr, wukv_ref, k_ref, v_ref)

    hq = proj(3)
    qh_ref[0] = hq * jax.nn.sigmoid(hq) * (HG_DK ** -0.5)
    l0, l1 = lbl_ref[0], lbl_ref[1]
    lmax = jnp.maximum(l0, l1)
    e0, e1 = jnp.exp(l0 - lmax), jnp.exp(l1 - lmax)
    lb = e0 / (e0 + e1)
    for d, (k_out, lf_out) in enumerate(((kf_ref, lff_ref), (kb_ref, lfb_ref))):
        lbd = lb[d:d + 1]
        f = lbd + (1.0 - lbd) * jax.nn.sigmoid(proj(4 + d))
        k_out[0] = 1.0 - f
        lf_out[0] = jnp.log(f)
    vh_ref[0] = proj(6)
    hg = proj(7)
    sg_ref[0] = hg * jax.nn.sigmoid(hg)
    gm_ref[0] = jax.nn.sigmoid(proj(8))
    gh_ref[0] = jax.nn.sigmoid(proj(9))


def _const_spec(shape):
    return pl.BlockSpec(shape, lambda *_: (0,) * len(shape))


def _inproj(x, mods, mod_row, wts, rope_tabs, cache_out, tm):
    bsz, n, _ = x.shape
    rope = rope_tabs is not None
    w_in_p, g_mix, g_q, w_uq_p, g_kv, w_ukv_p, lb_logits = wts

    def tok(width):
        return pl.BlockSpec((1, tm, width), lambda b, i: (b, i, 0))

    def mod(col):
        return pl.BlockSpec((1, 1, D_MODEL), lambda b, i: (mod_row(b), 0, col))

    in_specs = [tok(D_MODEL), mod(0), mod(1), _const_spec((1, D_MODEL)), _const_spec(w_in_p.shape),
                _const_spec((1, Q_LORA)), _const_spec(w_uq_p.shape), _const_spec((1, KV_LORA)),
                _const_spec(w_ukv_p.shape), _const_spec(lb_logits.shape)]
    args = [x, mods, mods, g_mix, w_in_p, g_q, w_uq_p, g_kv, w_ukv_p, lb_logits]
    if rope:
        in_specs += [pl.BlockSpec((tm, LANES), lambda b, i: (i, 0))] * 2
        args += list(rope_tabs)

    widths = [(QK_PAD_W, BF16)] * 3 + [(HG_KW, F32)] * 7 + [(D_MODEL, F32)] * 2
    if cache_out:
        widths += [(KV_LORA, F32), (QK_ROPE, F32)]
    out_specs = [tok(w) for w, _ in widths]
    out_shape = [jax.ShapeDtypeStruct((bsz, n, w), dt) for w, dt in widths]
    return pl.pallas_call(
        functools.partial(_inproj_kernel, rope=rope, cache_out=cache_out),
        grid=(bsz, n // tm),
        in_specs=in_specs, out_specs=out_specs, out_shape=out_shape,
        compiler_params=_cparams(("parallel", "parallel")),
        name="inproj_rope" if rope else "inproj",
    )(*args)


def _cache_kv_kernel(ckv_ref, kr_ref, wukv_ref, k_ref, v_ref):
    _store_kv(ckv_ref[0], kr_ref[0], wukv_ref, k_ref, v_ref)


def _cache_kv(ckv_c, krope_c_pad, w_ukv_p):
    bsz, n, _ = ckv_c.shape
    return pl.pallas_call(
        _cache_kv_kernel,
        grid=(bsz,),
        in_specs=[pl.BlockSpec((1, n, KV_LORA), lambda b: (b, 0, 0)),
                  pl.BlockSpec((1, n, LANES), lambda b: (b, 0, 0)),
                  _const_spec(w_ukv_p.shape)],
        out_specs=[pl.BlockSpec((1, n, QK_PAD_W), lambda b: (b, 0, 0)),
                   pl.BlockSpec((1, n, QK_PAD_W), lambda b: (b, 0, 0))],
        out_shape=[jax.ShapeDtypeStruct((bsz, n, QK_PAD_W), BF16)] * 2,
        compiler_params=_cparams(("parallel",)),
        name="cache_kv",
    )(ckv_c, krope_c_pad, w_ukv_p)


def _attn_kernel(q_ref, k_ref, v_ref, o_ref, m_scr, acc_scr):
    j = pl.program_id(2)

    @pl.when(j == 0)
    def _():
        m_scr[...] = jnp.full(m_scr.shape, -jnp.inf, F32)
        acc_scr[...] = jnp.zeros(acc_scr.shape, F32)

    for hh in range(MLA_HEADS):
        sl = slice(hh * HEAD_PAD, (hh + 1) * HEAD_PAD)
        s = _dot_nt(q_ref[0, :, sl], k_ref[0, :, sl])
        m_prev = m_scr[hh]
        m_cur = jnp.maximum(m_prev, jnp.max(s, axis=-1, keepdims=True))
        alpha = jnp.exp2(m_prev - m_cur)
        p = jnp.exp2(s - m_cur[:, :1])
        m_scr[hh] = m_cur
        acc_scr[hh] = alpha * acc_scr[hh] + _dot(p.astype(BF16), v_ref[0, :, sl])

    @pl.when(j == pl.num_programs(2) - 1)
    def _():
        low = lax.broadcasted_iota(jnp.int32, acc_scr.shape[1:], 1) < V_HEAD

        def head_out(hh):
            acc = acc_scr[hh]
            return acc / acc[:, V_HEAD:V_HEAD + 1]

        for pair in range(MLA_HEADS // 2):
            odd = pltpu.roll(head_out(2 * pair + 1), V_HEAD, 1)
            o_ref[0, :, pair * LANES:(pair + 1) * LANES] = jnp.where(low, head_out(2 * pair), odd).astype(o_ref.dtype)


def _attention(q, k, v, bq, bk):
    bsz, nq, _ = q.shape
    nk = k.shape[1]
    return pl.pallas_call(
        _attn_kernel,
        grid=(bsz, nq // bq, nk // bk),
        in_specs=[pl.BlockSpec((1, bq, QK_PAD_W), lambda b, i, j: (b, i, 0)),
                  pl.BlockSpec((1, bk, QK_PAD_W), lambda b, i, j: (b, j, 0)),
                  pl.BlockSpec((1, bk, QK_PAD_W), lambda b, i, j: (b, j, 0))],
        out_specs=pl.BlockSpec((1, bq, MLA_W), lambda b, i, j: (b, i, 0)),
        out_shape=jax.ShapeDtypeStruct((bsz, nq, MLA_W), BF16),
        scratch_shapes=[pltpu.VMEM((MLA_HEADS, bq, LANES), F32),
                        pltpu.VMEM((MLA_HEADS, bq, LANES), F32)],
        compiler_params=_cparams(("parallel", "parallel", "arbitrary")),
        name="attn",
    )(q, k, v)


def _hgrn_constants(reverse):
    c = HG_CHUNK
    t = np.arange(c)
    mats = [(t[None, :] <= t[:, None])]
    for lvl in range(1, HG_LEVELS + 1):
        g, half = 1 << lvl, 1 << (lvl - 1)
        p = t % g
        mid = t - p + half - 1
        isq = p >= half
        u = t[None, :]
        mats.append(np.where(isq[:, None], (u > mid[:, None]) & (u <= t[:, None]),
                             (u > t[:, None]) & (u <= mid[:, None])))
    mstack = np.stack(mats).astype(np.float32)
    if reverse:
        mstack = mstack[:, ::-1, ::-1]
    mstack = mstack.reshape(-1, c)
    return jnp.asarray(np.concatenate([mstack, mstack], axis=1), BF16)


def _hgrn_kernel(q_ref, k_ref, lf_ref, v_ref, s0_ref, mstack_ref, o_ref, sfin_ref, st_scr, *, reverse):
    c = HG_CHUNK
    step = pl.program_id(1)

    @pl.when(step == 0)
    def _():
        st_scr[...] = s0_ref[0]

    lf_hi, lf_lo = _split_hi_lo(lf_ref[0])
    e = _dot(mstack_ref[...], jnp.concatenate([lf_hi, lf_lo], axis=0))

    last = 0 if reverse else c - 1
    row = lax.broadcasted_iota(jnp.int32, (c, LANES), 0)
    lane = lax.broadcasted_iota(jnp.int32, (c, LANES), 1)
    lane_low = lane < HG_DK
    low_b = jnp.where(lane_low, 1.0, 0.0).astype(BF16)
    high_b = jnp.where(lane_low, 0.0, 1.0).astype(BF16)
    xor = lax.broadcasted_iota(jnp.int32, (c, 2 * c), 0) ^ (lax.broadcasted_iota(jnp.int32, (c, 2 * c), 1) & (c - 1))
    same = [xor < (1 << lvl) for lvl in range(HG_LEVELS)]
    diag_block = (lax.broadcasted_iota(jnp.int32, (LANES, LANES), 0) < HG_DV) == (
        lax.broadcasted_iota(jnp.int32, (LANES, LANES), 1) < HG_DK)
    zeros8 = jnp.zeros((8, LANES), F32)

    def q_side_block(lvl, b):
        return (b % 2 == 1) != reverse

    for pair in range(N_PAIRS):
        sl = slice(pair * LANES, (pair + 1) * LANES)
        q = q_ref[0, :, sl]
        k = k_ref[0, :, sl]
        v = v_ref[0, :, sl]
        a_cat = None
        for lvl in range(HG_LEVELS, -1, -1):
            half = (1 << lvl) // 2
            if lvl == 0:
                lhs, zk = q.astype(BF16), k.astype(BF16)
            elif half >= 8:
                z = jnp.exp(e[lvl * c:(lvl + 1) * c, sl])
                ys, zs = [], []
                for b in range(c // half):
                    rs = slice(b * half, (b + 1) * half)
                    blank = jnp.concatenate([zeros8] * (half // 8), axis=0)
                    if q_side_block(lvl, b):
                        ys.append(q[rs] * z[rs])
                        zs.append(blank)
                    else:
                        ys.append(blank)
                        zs.append(k[rs] * z[rs])
                lhs = jnp.concatenate(ys, axis=0).astype(BF16)
                zk = jnp.concatenate(zs, axis=0).astype(BF16)
            else:
                z = jnp.exp(e[lvl * c:(lvl + 1) * c, sl])
                q_row = ((row & half) == 0) if reverse else ((row & half) != 0)
                y = jnp.where(q_row, q, k) * z
                lhs = y.astype(BF16)
                zk = jnp.where(q_row, 0.0, y).astype(BF16)
            rhs = jnp.concatenate([zk * low_b, zk * high_b], axis=0)
            p = _dot_nt(lhs, rhs)
            a_cat = p if a_cat is None else jnp.where(same[lvl], p, a_cat)
        cum = e[0:c, sl]
        cum_last = cum[last:last + 1]
        v_cat = jnp.concatenate([jnp.where(lane_low, v, 0.0), jnp.where(lane_low, 0.0, v)], axis=0).astype(BF16)
        st = st_scr[pair]
        o = _dot(a_cat.astype(BF16), v_cat) + _dot_nt((q * jnp.exp(cum)).astype(BF16), st.astype(BF16))
        o_ref[0, :, sl] = o
        kd = (k * jnp.exp(cum_last - cum)).astype(BF16)
        upd = _dot_tn(v.astype(BF16), kd)
        st_scr[pair] = jnp.exp(cum_last) * st + jnp.where(diag_block, upd, 0.0)

    @pl.when(step == pl.num_programs(1) - 1)
    def _():
        sfin_ref[0] = st_scr[...]


def _hgrn(qh, kk, lf, vh, s0t, reverse):
    bsz, n, _ = qh.shape
    c = HG_CHUNK
    nc = n // c
    mstack = _hgrn_constants(reverse)
    if reverse:
        tok = pl.BlockSpec((1, c, HG_KW), lambda b, i: (b, nc - 1 - i, 0))
    else:
        tok = pl.BlockSpec((1, c, HG_KW), lambda b, i: (b, i, 0))
    st_spec = pl.BlockSpec((1, N_PAIRS, LANES, LANES), lambda b, i: (b, 0, 0, 0))
    return pl.pallas_call(
        functools.partial(_hgrn_kernel, reverse=reverse),
        grid=(bsz, nc),
        in_specs=[tok, tok, tok, tok, st_spec, _const_spec(mstack.shape)],
        out_specs=[tok, st_spec],
        out_shape=[jax.ShapeDtypeStruct((bsz, n, HG_W), F32),
                   jax.ShapeDtypeStruct((bsz, N_PAIRS, LANES, LANES), F32)],
        scratch_shapes=[pltpu.VMEM((N_PAIRS, LANES, LANES), F32)],
        compiler_params=_cparams(("parallel", "arbitrary")),
        name="hgrn_bwd" if reverse else "hgrn_fwd",
    )(qh, kk, lf, vh, s0t, mstack)


def _state_to_pairs(s):
    b = s.shape[0]
    st = jnp.swapaxes(s, -1, -2).reshape(b, N_PAIRS, 2, HG_DV, HG_DK)
    eye = jnp.eye(2, dtype=s.dtype)
    out = jnp.einsum('bpavk,ac->bpavck', st, eye)
    return out.reshape(b, N_PAIRS, 2 * HG_DV, 2 * HG_DK)


def _pairs_to_state(sp):
    b = sp.shape[0]
    x = sp.reshape(b, N_PAIRS, 2, HG_DV, 2, HG_DK)
    diag = jnp.stack([x[:, :, 0, :, 0, :], x[:, :, 1, :, 1, :]], axis=2)
    return jnp.swapaxes(diag.reshape(b, HG_HEADS, HG_DV, HG_DK), -1, -2)


def _merge_kernel(x_ref, ga_ref, om_ref, of_ref, ob_ref, sg_ref, gm_ref, gh_ref, ghg_ref, hmean_ref,
                  wbm_ref, wbh_ref, wout_ref, x1_ref):
    o = of_ref[0] + ob_ref[0]
    sq_hi, sq_lo = _split_hi_lo(o * o)
    ms = _dot(sq_hi, hmean_ref[...]) + _dot(sq_lo, hmean_ref[...])
    o_hg = (o * lax.rsqrt(ms + EPS) * ghg_ref[...]) * sg_ref[0]
    merged = gm_ref[0] * _dot(om_ref[0], wbm_ref[...]) + gh_ref[0] * _dot(o_hg.astype(BF16), wbh_ref[...])
    out = _dot(merged.astype(BF16), wout_ref[...])
    x1_ref[0] = x_ref[0] + ga_ref[0] * out


def _merge(x, mods, mod_row, o_mla, o_f, o_b, sg, gm, gh, wts, tm):
    bsz, n, _ = x.shape
    g_hg, hmean, w_br_mla, w_br_hg, w_out = wts

    def tok(width):
        return pl.BlockSpec((1, tm, width), lambda b, i: (b, i, 0))

    return pl.pallas_call(
        _merge_kernel,
        grid=(bsz, n // tm),
        in_specs=[tok(D_MODEL), pl.BlockSpec((1, 1, D_MODEL), lambda b, i: (mod_row(b), 0, 2)),
                  tok(MLA_W), tok(HG_W), tok(HG_W), tok(HG_W), tok(D_MODEL), tok(D_MODEL),
                  _const_spec((1, HG_W)), _const_spec(hmean.shape), _const_spec(w_br_mla.shape),
                  _const_spec(w_br_hg.shape), _const_spec(w_out.shape)],
        out_specs=tok(D_MODEL),
        out_shape=jax.ShapeDtypeStruct((bsz, n, D_MODEL), F32),
        compiler_params=_cparams(("parallel", "parallel")),
        name="merge",
    )(x, mods, o_mla, o_f, o_b, sg, gm, gh, g_hg, hmean, w_br_mla, w_br_hg, w_out)


def _ffn_kernel(x_ref, sh_ref, sc_ref, ga_ref, gff_ref, w1_ref, w2_ref, gfin_ref, y_ref, acc_scr, h_scr):
    j = pl.program_id(2)

    @pl.when(j == 0)
    def _():
        h = _rms(x_ref[0], gff_ref[...]) * (1.0 + sc_ref[0]) + sh_ref[0]
        h_scr[...] = h.astype(BF16)
        acc_scr[...] = jnp.zeros(acc_scr.shape, F32)

    a = jnp.maximum(_dot(h_scr[...], w1_ref[...]), 0.0)
    acc_scr[...] += _dot((a * a).astype(BF16), w2_ref[...])

    @pl.when(j == pl.num_programs(2) - 1)
    def _():
        x2 = x_ref[0] + ga_ref[0] * acc_scr[...]
        y_ref[0] = _rms(x2, gfin_ref[...])


def _ffn(x1, mods, mod_row, wts, tm, tf):
    bsz, n, _ = x1.shape
    g_ff, w_ff1, w_ff2, g_final = wts

    def mod(col):
        return pl.BlockSpec((1, 1, D_MODEL), lambda b, i, j: (mod_row(b), 0, col))

    tok = pl.BlockSpec((1, tm, D_MODEL), lambda b, i, j: (b, i, 0))
    return pl.pallas_call(
        _ffn_kernel,
        grid=(bsz, n // tm, D_FF // tf),
        in_specs=[tok, mod(3), mod(4), mod(5), _const_spec((1, D_MODEL)),
                  pl.BlockSpec((D_MODEL, tf), lambda b, i, j: (0, j)),
                  pl.BlockSpec((tf, D_MODEL), lambda b, i, j: (j, 0)),
                  _const_spec((1, D_MODEL))],
        out_specs=tok,
        out_shape=jax.ShapeDtypeStruct((bsz, n, D_MODEL), F32),
        scratch_shapes=[pltpu.VMEM((tm, D_MODEL), F32), pltpu.VMEM((tm, D_MODEL), BF16)],
        compiler_params=_cparams(("parallel", "parallel", "arbitrary")),
        name="ffn",
    )(x1, mods, mods, mods, g_ff, w_ff1, w_ff2, g_final)


def _pad_head_cols(w, widths_in, layout):
    rows = w.shape[0]
    w = w.reshape(rows, MLA_HEADS, sum(width for _, width in widths_in))
    pieces, off = {}, 0
    for name, width in widths_in:
        pieces[name] = w[:, :, off:off + width]
        off += width
    cols = [pieces[name] if name in pieces else jnp.zeros((rows, MLA_HEADS, width), w.dtype) for name, width in layout]
    return jnp.concatenate(cols, axis=-1).reshape(rows, -1)


def _prep_weights(w_in, w_uq, w_ukv):
    segs = list(jnp.split(w_in, np.cumsum((Q_LORA, KV_LORA, QK_ROPE, HG_KW, HG_KW, HG_KW, HG_W, HG_W, D_MODEL)), axis=1))
    segs[2] = jnp.pad(segs[2], ((0, 0), (0, LANES - QK_ROPE)))
    w_in_p = jnp.concatenate(segs, axis=1).astype(BF16)
    w_uq_p = _pad_head_cols(w_uq, (("nope", QK_NOPE), ("rope", QK_ROPE)),
                            (("rope", QK_ROPE), ("zero", HEAD_PAD - QK_ROPE - QK_NOPE), ("nope", QK_NOPE))).astype(BF16)
    kv_in = (("nope", QK_NOPE), ("v", V_HEAD))
    k_cols = _pad_head_cols(w_ukv, kv_in, (("zero", HEAD_PAD - QK_NOPE), ("nope", QK_NOPE)))
    v_cols = _pad_head_cols(w_ukv, kv_in, (("v", V_HEAD), ("zero", HEAD_PAD - V_HEAD)))
    w_ukv_p = jnp.concatenate([k_cols, v_cols], axis=1).astype(BF16)
    return w_in_p, w_uq_p, w_ukv_p


def _rope_tables(n):
    rows = n // GRID_W
    row = jnp.repeat(jnp.arange(rows, dtype=F32), GRID_W)
    col = jnp.tile(jnp.arange(GRID_W, dtype=F32), rows)
    half = QK_ROPE // 2
    inv = ROPE_BASE ** (-jnp.arange(0, half, 2, dtype=F32) / half)
    ang = jnp.concatenate([row[:, None] * inv, col[:, None] * inv], axis=-1)
    cos = jnp.repeat(jnp.cos(ang), 2, axis=-1)
    sin = jnp.repeat(jnp.sin(ang), 2, axis=-1)
    cos = jnp.concatenate([cos, jnp.ones((n, LANES - QK_ROPE), F32)], axis=-1)
    sin = jnp.concatenate([sin, jnp.zeros((n, LANES - QK_ROPE), F32)], axis=-1)
    return cos, sin


def _trunk(x, mods, mod_row, wts, rope_tabs, ctx):
    (inproj_w, merge_w, ffn_w, w_ukv_p) = wts
    bsz, n, _ = x.shape
    tm = 256
    outs = _inproj(x, mods, mod_row, inproj_w, rope_tabs, ctx is None, tm)
    q, k, v, qh, kf, lff, kb, lfb, vh, sg, gm, gh = outs[:12]
    if ctx is None:
        ckv, krope = outs[12:]
        s0f = s0b = jnp.zeros((bsz, N_PAIRS, LANES, LANES), F32)
        bk = n
    else:
        ckv_c, krope_c, state_c = ctx
        kr_pad = jnp.pad(krope_c, ((0, 0), (0, 0), (0, LANES - QK_ROPE)))
        k_c, v_c = _cache_kv(ckv_c, kr_pad, w_ukv_p)
        k = jnp.concatenate([k, k_c], axis=1)
        v = jnp.concatenate([v, v_c], axis=1)
        s0f = _state_to_pairs(state_c[:, 0])
        s0b = _state_to_pairs(state_c[:, 1])
        bk = 512
    o_mla = _attention(q, k, v, min(n, 1024), bk)
    o_f, sf = _hgrn(qh, kf, lff, vh, s0f, False)
    o_b, sb = _hgrn(qh, kb, lfb, vh, s0b, True)
    x1 = _merge(x, mods, mod_row, o_mla, o_f, o_b, sg, gm, gh, merge_w, tm)
    if ctx is None:
        x1 = x1.reshape(1, bsz * n, D_MODEL)
    y = _ffn(x1, mods, mod_row, ffn_w, min(1024, x1.shape[1]), 1024).reshape(bsz, n, D_MODEL)
    if ctx is None:
        state = jnp.stack([_pairs_to_state(sf), _pairs_to_state(sb)], axis=1)
        return y, (ckv, krope, state)
    return y, None


def kernel(x_prompt, x_sample, cache_ckv, cache_krope, state_hgrn, c, c_ctx, w_ada, b_ada, g_norm_mix, g_norm_ff, w_in, g_q_norm, w_uq, g_kv_norm, w_ukv, g_hg_norm, hg_lb_logits, w_br_mla, w_br_hg, w_out, w_ff1, w_ff2, g_final):
    assert w_in.shape[0] == 1, "single-layer trunk"
    dec_b = c.shape[0]
    cc = jnp.concatenate([c, c_ctx[None, :], jnp.zeros((8 - dec_b - 1, D_MODEL), F32)], axis=0)
    mods = _ada(cc, w_ada[0], b_ada[0][None, :]).reshape(8, 1, 6 * D_MODEL)

    w_in_p, w_uq_p, w_ukv_p = _prep_weights(w_in[0], w_uq[0], w_ukv[0])
    inproj_w = (w_in_p, g_norm_mix[0][None], g_q_norm[0][None], w_uq_p, g_kv_norm[0][None], w_ukv_p, hg_lb_logits)
    head_id = np.arange(HG_W) // HG_DV
    hmean = jnp.asarray((head_id[:, None] == head_id[None, :]).astype(np.float32) / HG_DV, BF16)
    merge_w = (g_hg_norm[0][None], hmean, w_br_mla[0].astype(BF16), w_br_hg[0].astype(BF16), w_out[0].astype(BF16))
    ffn_w = (g_norm_ff[0][None], w_ff1[0].astype(BF16), w_ff2[0].astype(BF16), g_final[None])
    wts = (inproj_w, merge_w, ffn_w, w_ukv_p)

    y_prompt, (ckv, krope, state) = _trunk(x_prompt, mods, lambda b: dec_b, wts, None, None)
    rope_tabs = _rope_tables(x_sample.shape[1])
    y_sample, _ = _trunk(x_sample, mods, lambda b: b, wts, rope_tabs,
                         (cache_ckv[:, 0], cache_krope[:, 0], state_hgrn[:, 0]))
    return (y_prompt, y_sample, ckv[:, None], krope[:, None], state[:, None])
```

```python
import functools

import numpy as np
import jax
import jax.numpy as jnp
from jax import lax
from jax.experimental import pallas as pl
from jax.experimental.pallas import tpu as pltpu

D_MODEL = 1024
GRID_W = 64
MLA_HEADS = 8
Q_LORA = 384
KV_LORA = 256
QK_NOPE = 64
QK_ROPE = 32
V_HEAD = 64
MLA_W = MLA_HEADS * V_HEAD
MLA_SCALE = (QK_NOPE + QK_ROPE) ** -0.5
LOG2_E = 1.4426950408889634
HG_HEADS = 8
HG_DK = 64
HG_DV = 64
HG_KW = HG_HEADS * HG_DK
HG_W = HG_HEADS * HG_DV
D_FF = 4 * D_MODEL
ROPE_BASE = 10000.0
EPS = 1e-6

LANES = 128
HEAD_PAD = LANES
QK_PAD_W = MLA_HEADS * HEAD_PAD
N_PAIRS = HG_HEADS // 2
VMEM_LIMIT = 56 * 1024 * 1024

_LAT_W = Q_LORA + KV_LORA
_MIX_SIZES = (HG_KW, HG_KW, HG_KW, HG_W, HG_W, D_MODEL, D_MODEL)
_MIX_OFFS = tuple(int(o) for o in np.cumsum((0,) + _MIX_SIZES))

HG_CHUNK = 128
HG_LEVELS = 7

F32 = jnp.float32
BF16 = jnp.bfloat16


def _cparams(sem):
    return pltpu.CompilerParams(dimension_semantics=sem, vmem_limit_bytes=VMEM_LIMIT)


def _rms(x, g):
    return x * lax.rsqrt(jnp.mean(x * x, axis=-1, keepdims=True) + EPS) * g


def _dot(a, b):
    return jnp.dot(a, b, preferred_element_type=F32)


def _dot_nt(a, b):
    return lax.dot_general(a, b, (((1,), (1,)), ((), ())), preferred_element_type=F32)


def _dot_tn(a, b):
    return lax.dot_general(a, b, (((0,), (0,)), ((), ())), preferred_element_type=F32)


def _split_hi_lo(x):
    hi = x.astype(BF16)
    lo = (x - hi.astype(F32)).astype(BF16)
    return hi, lo


def _ada_kernel(c_ref, w_ref, b_ref, o_ref):
    c = c_ref[...]
    a = c * jax.nn.sigmoid(c)
    a_hi, a_lo = _split_hi_lo(a)
    w_hi, w_lo = _split_hi_lo(w_ref[...])
    o_ref[...] = _dot(a_hi, w_hi) + _dot(a_hi, w_lo) + _dot(a_lo, w_hi) + b_ref[...]


def _ada(cc, w_ada, b_ada):
    rows, tn = cc.shape[0], 1536
    n = w_ada.shape[1]
    return pl.pallas_call(
        _ada_kernel,
        grid=(n // tn,),
        in_specs=[pl.BlockSpec((rows, D_MODEL), lambda j: (0, 0)),
                  pl.BlockSpec((D_MODEL, tn), lambda j: (0, j)),
                  pl.BlockSpec((1, tn), lambda j: (0, j))],
        out_specs=pl.BlockSpec((rows, tn), lambda j: (0, j)),
        out_shape=jax.ShapeDtypeStruct((rows, n), F32),
        compiler_params=_cparams(("arbitrary",)),
        name="ada",
    )(cc, w_ada, b_ada)


def _rope_tile(blk, cos, sin, even):
    rot = jnp.where(even, -pltpu.roll(blk, LANES - 1, 1), pltpu.roll(blk, 1, 1))
    return blk * cos + rot * sin


def _store_kv(ckv, kr, wukv_ref, k_ref, v_ref):
    kv = _dot(ckv.astype(BF16), wukv_ref[...])
    one_lane = ((lax.broadcasted_iota(jnp.int32, (1, QK_PAD_W), 1) & (HEAD_PAD - 1)) == V_HEAD).astype(F32)
    for hh in range(MLA_HEADS):
        sl = slice(hh * HEAD_PAD, (hh + 1) * HEAD_PAD)
        k_ref[0, :, sl] = (kv[:, sl] + kr).astype(BF16)
        vsl = slice(QK_PAD_W + hh * HEAD_PAD, QK_PAD_W + (hh + 1) * HEAD_PAD)
        v_ref[0, sl, :] = (kv[:, vsl] + one_lane[:, sl]).T.astype(BF16)


def _inproj_kernel(*refs, rope, cache_out):
    (x_ref, sh_ref, sc_ref, gmix_ref, wlat_ref, wkr_ref, wmix_ref, gq_ref, wuq_ref, gkv_ref, wukv_ref, lbl_ref) = refs[:12]
    refs = refs[12:]
    if rope:
        cos_ref, sin_ref = refs[:2]
        refs = refs[2:]
    (q_ref, k_ref, v_ref, qh_ref, kf_ref, lff_ref, kb_ref, lfb_ref, vh_ref, sg_ref, gm_ref, gh_ref) = refs[:12]
    refs = refs[12:]

    x = x_ref[0]
    h = _rms(x, gmix_ref[...]) * (1.0 + sc_ref[0]) + sh_ref[0]
    hb = h.astype(BF16)

    def proj(i):
        if i == 0:
            return _dot(hb, wlat_ref[:, :Q_LORA])
        if i == 1:
            return _dot(hb, wlat_ref[:, Q_LORA:])
        if i == 2:
            return _dot(hb, wkr_ref[...])
        return _dot(hb, wmix_ref[:, _MIX_OFFS[i - 3]:_MIX_OFFS[i - 2]])

    if rope:
        cos = cos_ref[...]
        sin = sin_ref[...]
        even = (lax.broadcasted_iota(jnp.int32, cos.shape, 1) & 1) == 0

    qn = _rms(proj(0), gq_ref[...]).astype(BF16)
    q = _dot(qn, wuq_ref[...]) * (MLA_SCALE * LOG2_E)
    for hh in range(MLA_HEADS):
        blk = q[:, hh * HEAD_PAD:(hh + 1) * HEAD_PAD]
        if rope:
            blk = _rope_tile(blk, cos, sin, even)
        q_ref[0, hh * HEAD_PAD:(hh + 1) * HEAD_PAD, :] = blk.T.astype(BF16)

    ckv = _rms(proj(1), gkv_ref[...])
    kr = proj(2)
    if cache_out:
        ckv_ref, krope_ref = refs
        ckv_ref[0] = ckv
        krope_ref[0] = kr[:, :QK_ROPE]
    if rope:
        kr = _rope_tile(kr, cos, sin, even)
    _store_kv(ckv, kr, wukv_ref, k_ref, v_ref)

    hq = proj(3)
    qh_ref[0] = hq * jax.nn.sigmoid(hq) * (HG_DK ** -0.5)
    l0, l1 = lbl_ref[0], lbl_ref[1]
    lmax = jnp.maximum(l0, l1)
    e0, e1 = jnp.exp(l0 - lmax), jnp.exp(l1 - lmax)
    lb = e0 / (e0 + e1)
    for d, (k_out, lf_out) in enumerate(((kf_ref, lff_ref), (kb_ref, lfb_ref))):
        lbd = lb[d:d + 1]
        f = lbd + (1.0 - lbd) * jax.nn.sigmoid(proj(4 + d))
        k_out[0] = 1.0 - f
        lf_out[0] = jnp.log(f)
    vh_ref[0] = proj(6)
    hg = proj(7)
    sg_ref[0] = hg * jax.nn.sigmoid(hg)
    gm_ref[0] = jax.nn.sigmoid(proj(8))
    gh_ref[0] = jax.nn.sigmoid(proj(9))


def _const_spec(shape):
    return pl.BlockSpec(shape, lambda *_: (0,) * len(shape))


def _inproj(x, mods, mod_row, wts, rope_tabs, cache_out, tm, kv_rows):
    bsz, n, _ = x.shape
    rope = rope_tabs is not None
    (w_lat, w_kr, w_mix), g_mix, g_q, w_uq_p, g_kv, w_ukv_p, lb_logits = wts

    def tok(width):
        return pl.BlockSpec((1, tm, width), lambda b, i: (b, i, 0))

    def mod(col):
        return pl.BlockSpec((1, 1, D_MODEL), lambda b, i: (mod_row(b), 0, col))

    in_specs = [tok(D_MODEL), mod(0), mod(1), _const_spec((1, D_MODEL)), _const_spec(w_lat.shape),
                _const_spec(w_kr.shape), _const_spec(w_mix.shape),
                _const_spec((1, Q_LORA)), _const_spec(w_uq_p.shape), _const_spec((1, KV_LORA)),
                _const_spec(w_ukv_p.shape), _const_spec(lb_logits.shape)]
    args = [x, mods, mods, g_mix, w_lat, w_kr, w_mix, g_q, w_uq_p, g_kv, w_ukv_p, lb_logits]
    if rope:
        in_specs += [pl.BlockSpec((tm, LANES), lambda b, i: (i, 0))] * 2
        args += list(rope_tabs)

    widths = [(QK_PAD_W, BF16)] * 3 + [(HG_KW, F32)] * 7 + [(D_MODEL, F32)] * 2
    if cache_out:
        widths += [(KV_LORA, F32), (QK_ROPE, F32)]
    out_specs = [tok(w) for w, _ in widths]
    out_shape = [jax.ShapeDtypeStruct((bsz, n, w), dt) for w, dt in widths]
    feat_major = pl.BlockSpec((1, QK_PAD_W, tm), lambda b, i: (b, 0, i))
    out_specs[0], out_shape[0] = feat_major, jax.ShapeDtypeStruct((bsz, QK_PAD_W, n), BF16)
    out_shape[1] = jax.ShapeDtypeStruct((bsz, kv_rows, QK_PAD_W), BF16)
    out_specs[2], out_shape[2] = feat_major, jax.ShapeDtypeStruct((bsz, QK_PAD_W, kv_rows), BF16)
    return pl.pallas_call(
        functools.partial(_inproj_kernel, rope=rope, cache_out=cache_out),
        grid=(bsz, n // tm),
        in_specs=in_specs, out_specs=out_specs, out_shape=out_shape,
        compiler_params=_cparams(("parallel", "parallel")),
        name="inproj_rope" if rope else "inproj",
    )(*args)


def _cache_kv_kernel(ckv_ref, kr_ref, wukv_ref, k_all_ref, v_all_ref, k_ref, v_ref):
    del k_all_ref, v_all_ref
    _store_kv(ckv_ref[0], kr_ref[0], wukv_ref, k_ref, v_ref)


def _cache_kv(ckv_c, krope_c_pad, w_ukv_p, k_all, v_all):
    bsz, past, _ = ckv_c.shape
    tail = (k_all.shape[1] - past) // past
    assert (tail + 1) * past == k_all.shape[1] == v_all.shape[2]
    k_spec = pl.BlockSpec((1, past, QK_PAD_W), lambda b: (b, tail, 0))
    v_spec = pl.BlockSpec((1, QK_PAD_W, past), lambda b: (b, 0, tail))
    return pl.pallas_call(
        _cache_kv_kernel,
        grid=(bsz,),
        in_specs=[pl.BlockSpec((1, past, KV_LORA), lambda b: (b, 0, 0)),
                  pl.BlockSpec((1, past, LANES), lambda b: (b, 0, 0)),
                  _const_spec(w_ukv_p.shape),
                  pl.BlockSpec(memory_space=pl.ANY), pl.BlockSpec(memory_space=pl.ANY)],
        out_specs=[k_spec, v_spec],
        out_shape=[jax.ShapeDtypeStruct(k_all.shape, BF16), jax.ShapeDtypeStruct(v_all.shape, BF16)],
        input_output_aliases={3: 0, 4: 1},
        compiler_params=_cparams(("parallel",)),
        name="cache_kv",
    )(ckv_c, krope_c_pad, w_ukv_p, k_all, v_all)


def _attn_kernel(q_ref, k_ref, v_ref, o_ref, m_scr, acc_scr):
    j = pl.program_id(2)

    @pl.when(j == 0)
    def _():
        m_scr[...] = jnp.full(m_scr.shape, -jnp.inf, F32)
        acc_scr[...] = jnp.zeros(acc_scr.shape, F32)

    for hh in range(MLA_HEADS):
        sl = slice(hh * HEAD_PAD, (hh + 1) * HEAD_PAD)
        s = _dot(k_ref[0, :, sl], q_ref[0, sl, :])
        m_prev = m_scr[hh]
        m_cur = jnp.maximum(m_prev, jnp.max(s, axis=0, keepdims=True))
        alpha = jnp.exp2(m_prev - m_cur)
        p = jnp.exp2(s - m_cur[:1])
        m_scr[hh] = m_cur
        acc_scr[hh] = alpha[:1] * acc_scr[hh] + _dot(v_ref[0, sl, :], p.astype(BF16))

    @pl.when(j == pl.num_programs(2) - 1)
    def _():
        low = lax.broadcasted_iota(jnp.int32, (acc_scr.shape[2], LANES), 1) < V_HEAD

        def head_out(hh):
            acc = acc_scr[hh].T
            return acc / acc[:, V_HEAD:V_HEAD + 1]

        for pair in range(MLA_HEADS // 2):
            odd = pltpu.roll(head_out(2 * pair + 1), V_HEAD, 1)
            o_ref[0, :, pair * LANES:(pair + 1) * LANES] = jnp.where(low, head_out(2 * pair), odd).astype(o_ref.dtype)


def _attention(q, k, v, bq, bk):
    bsz, _, nq = q.shape
    nk = k.shape[1]
    return pl.pallas_call(
        _attn_kernel,
        grid=(bsz, nq // bq, nk // bk),
        in_specs=[pl.BlockSpec((1, QK_PAD_W, bq), lambda b, i, j: (b, 0, i)),
                  pl.BlockSpec((1, bk, QK_PAD_W), lambda b, i, j: (b, j, 0)),
                  pl.BlockSpec((1, QK_PAD_W, bk), lambda b, i, j: (b, 0, j))],
        out_specs=pl.BlockSpec((1, bq, MLA_W), lambda b, i, j: (b, i, 0)),
        out_shape=jax.ShapeDtypeStruct((bsz, nq, MLA_W), BF16),
        scratch_shapes=[pltpu.VMEM((MLA_HEADS, 8, bq), F32),
                        pltpu.VMEM((MLA_HEADS, HEAD_PAD, bq), F32)],
        compiler_params=_cparams(("parallel", "parallel", "arbitrary")),
        name="attn",
    )(q, k, v)


def _hgrn_constants(reverse):
    c = HG_CHUNK
    t = np.arange(c)
    mats = [(t[None, :] <= t[:, None])]
    for lvl in range(1, HG_LEVELS + 1):
        g, half = 1 << lvl, 1 << (lvl - 1)
        p = t % g
        mid = t - p + half - 1
        isq = p >= half
        u = t[None, :]
        mats.append(np.where(isq[:, None], (u > mid[:, None]) & (u <= t[:, None]),
                             (u > t[:, None]) & (u <= mid[:, None])))
    mstack = np.stack(mats).astype(np.float32)
    if reverse:
        mstack = mstack[:, ::-1, ::-1]
    mstack = mstack.reshape(-1, c)
    return jnp.asarray(np.concatenate([mstack, mstack], axis=1), BF16)


def _hgrn_kernel(q_ref, k_ref, lf_ref, v_ref, s0_ref, mstack_ref, o_ref, *rest, reverse, emit_state):
    st_scr = rest[-1]
    c = HG_CHUNK
    step = pl.program_id(1)

    @pl.when(step == 0)
    def _():
        st_scr[...] = s0_ref[0]

    lf_hi, lf_lo = _split_hi_lo(lf_ref[0])
    e = _dot(mstack_ref[...], jnp.concatenate([lf_hi, lf_lo], axis=0))

    last = 0 if reverse else c - 1
    row = lax.broadcasted_iota(jnp.int32, (c, LANES), 0)
    lane = lax.broadcasted_iota(jnp.int32, (c, LANES), 1)
    lane_low = lane < HG_DK
    low_b = jnp.where(lane_low, 1.0, 0.0).astype(BF16)
    high_b = jnp.where(lane_low, 0.0, 1.0).astype(BF16)
    xor = lax.broadcasted_iota(jnp.int32, (c, c), 0) ^ lax.broadcasted_iota(jnp.int32, (c, c), 1)
    same = [xor < (1 << lvl) for lvl in range(HG_LEVELS)]
    diag_block = (lax.broadcasted_iota(jnp.int32, (LANES, LANES), 0) < HG_DV) == (
        lax.broadcasted_iota(jnp.int32, (LANES, LANES), 1) < HG_DK)
    zeros8 = jnp.zeros((8, LANES), F32)

    def q_side_block(lvl, b):
        return (b % 2 == 1) != reverse

    for pair in range(N_PAIRS):
        sl = slice(pair * LANES, (pair + 1) * LANES)
        q = q_ref[0, :, sl]
        k = k_ref[0, :, sl]
        v = v_ref[0, :, sl]
        a_cat = None
        for lvl in range(HG_LEVELS, -1, -1):
            half = (1 << lvl) // 2
            if lvl == 0:
                lhs, zk = q.astype(BF16), k.astype(BF16)
            elif half >= 8:
                z = jnp.exp(e[lvl * c:(lvl + 1) * c, sl])
                ys, zs = [], []
                for b in range(c // half):
                    rs = slice(b * half, (b + 1) * half)
                    blank = jnp.concatenate([zeros8] * (half // 8), axis=0)
                    if q_side_block(lvl, b):
                        ys.append(q[rs] * z[rs])
                        zs.append(blank)
                    else:
                        ys.append(blank)
                        zs.append(k[rs] * z[rs])
                lhs = jnp.concatenate(ys, axis=0).astype(BF16)
                zk = jnp.concatenate(zs, axis=0).astype(BF16)
            else:
                z = jnp.exp(e[lvl * c:(lvl + 1) * c, sl])
                q_row = ((row & half) == 0) if reverse else ((row & half) != 0)
                y = jnp.where(q_row, q, k) * z
                lhs = y.astype(BF16)
                zk = jnp.where(q_row, 0.0, y).astype(BF16)
            rhs = jnp.concatenate([zk * low_b, zk * high_b], axis=0)
            p = _dot_nt(lhs, rhs)
            if a_cat is None:
                a_cat = (p[:, :c], p[:, c:])
            else:
                a_cat = tuple(jnp.where(same[lvl], p_h, a_h) for p_h, a_h in zip((p[:, :c], p[:, c:]), a_cat))
        a_cat = jnp.concatenate(a_cat, axis=1)
        cum = e[0:c, sl]
        cum_last = cum[last:last + 1]
        v_cat = jnp.concatenate([jnp.where(lane_low, v, 0.0), jnp.where(lane_low, 0.0, v)], axis=0).astype(BF16)
        st = st_scr[pair]
        o = _dot(a_cat.astype(BF16), v_cat) + _dot_nt((q * jnp.exp(cum)).astype(BF16), st.astype(BF16))
        o_ref[0, :, sl] = o
        kd = (k * jnp.exp(cum_last - cum)).astype(BF16)
        upd = _dot_tn(v.astype(BF16), kd)
        st_scr[pair] = jnp.exp(cum_last) * st + jnp.where(diag_block, upd, 0.0)

    if emit_state:
        sfin_ref = rest[0]

        @pl.when(step == pl.num_programs(1) - 1)
        def _():
            for pair in range(N_PAIRS):
                s_pair = st_scr[pair].T
                sfin_ref[0, 2 * pair] = s_pair[:HG_DK, :HG_DV]
                sfin_ref[0, 2 * pair + 1] = pltpu.roll(s_pair, HG_DV, 1)[HG_DK:, :HG_DV]


def _hgrn(qh, kk, lf, vh, s0t, reverse, emit_state):
    bsz, n, _ = qh.shape
    c = HG_CHUNK
    nc = n // c
    mstack = _hgrn_constants(reverse)
    if reverse:
        tok = pl.BlockSpec((1, c, HG_KW), lambda b, i: (b, nc - 1 - i, 0))
    else:
        tok = pl.BlockSpec((1, c, HG_KW), lambda b, i: (b, i, 0))
    st_spec = pl.BlockSpec((1, N_PAIRS, LANES, LANES), lambda b, i: (b, 0, 0, 0))
    out_specs = [tok]
    out_shape = [jax.ShapeDtypeStruct((bsz, n, HG_W), F32)]
    if emit_state:
        out_specs.append(pl.BlockSpec((1, HG_HEADS, HG_DK, HG_DV), lambda b, i: (b, 0, 0, 0)))
        out_shape.append(jax.ShapeDtypeStruct((bsz, HG_HEADS, HG_DK, HG_DV), F32))
    return pl.pallas_call(
        functools.partial(_hgrn_kernel, reverse=reverse, emit_state=emit_state),
        grid=(bsz, nc),
        in_specs=[tok, tok, tok, tok, st_spec, _const_spec(mstack.shape)],
        out_specs=out_specs, out_shape=out_shape,
        scratch_shapes=[pltpu.VMEM((N_PAIRS, LANES, LANES), F32)],
        compiler_params=_cparams(("parallel", "arbitrary")),
        name="hgrn_bwd" if reverse else "hgrn_fwd",
    )(qh, kk, lf, vh, s0t, mstack)


def _state_to_pairs(s):
    b = s.shape[0]
    st = jnp.swapaxes(s, -1, -2).reshape(b, N_PAIRS, 2, HG_DV, HG_DK)
    eye = jnp.eye(2, dtype=s.dtype)
    out = jnp.einsum('bpavk,ac->bpavck', st, eye)
    return out.reshape(b, N_PAIRS, 2 * HG_DV, 2 * HG_DK)


def _merge_kernel(x_ref, ga_ref, om_ref, of_ref, ob_ref, sg_ref, gm_ref, gh_ref, ghg_ref, hmean_ref,
                  wbm_ref, wbh_ref, wout_ref, x1_ref):
    o = of_ref[0] + ob_ref[0]
    sq_hi, sq_lo = _split_hi_lo(o * o)
    ms = _dot(sq_hi, hmean_ref[...]) + _dot(sq_lo, hmean_ref[...])
    o_hg = (o * lax.rsqrt(ms + EPS) * ghg_ref[...]) * sg_ref[0]
    merged = gm_ref[0] * _dot(om_ref[0], wbm_ref[...]) + gh_ref[0] * _dot(o_hg.astype(BF16), wbh_ref[...])
    out = _dot(merged.astype(BF16), wout_ref[...])
    x1_ref[0] = x_ref[0] + ga_ref[0] * out


def _merge(x, mods, mod_row, o_mla, o_f, o_b, sg, gm, gh, wts, tm):
    bsz, n, _ = x.shape
    g_hg, hmean, w_br_mla, w_br_hg, w_out = wts

    def tok(width):
        return pl.BlockSpec((1, tm, width), lambda b, i: (b, i, 0))

    return pl.pallas_call(
        _merge_kernel,
        grid=(bsz, n // tm),
        in_specs=[tok(D_MODEL), pl.BlockSpec((1, 1, D_MODEL), lambda b, i: (mod_row(b), 0, 2)),
                  tok(MLA_W), tok(HG_W), tok(HG_W), tok(HG_W), tok(D_MODEL), tok(D_MODEL),
                  _const_spec((1, HG_W)), _const_spec(hmean.shape), _const_spec(w_br_mla.shape),
                  _const_spec(w_br_hg.shape), _const_spec(w_out.shape)],
        out_specs=tok(D_MODEL),
        out_shape=jax.ShapeDtypeStruct((bsz, n, D_MODEL), F32),
        compiler_params=_cparams(("parallel", "parallel")),
        name="merge",
    )(x, mods, o_mla, o_f, o_b, sg, gm, gh, g_hg, hmean, w_br_mla, w_br_hg, w_out)


def _ffn_kernel(x_ref, sh_ref, sc_ref, ga_ref, gff_ref, w1_ref, w2_ref, gfin_ref, y_ref, acc_scr, h_scr):
    j = pl.program_id(2)

    @pl.when(j == 0)
    def _():
        h = _rms(x_ref[0], gff_ref[...]) * (1.0 + sc_ref[0]) + sh_ref[0]
        h_scr[...] = h.astype(BF16)
        acc_scr[...] = jnp.zeros(acc_scr.shape, F32)

    a = jnp.maximum(_dot(h_scr[...], w1_ref[...]), 0.0)
    acc_scr[...] += _dot((a * a).astype(BF16), w2_ref[...])

    @pl.when(j == pl.num_programs(2) - 1)
    def _():
        x2 = x_ref[0] + ga_ref[0] * acc_scr[...]
        y_ref[0] = _rms(x2, gfin_ref[...])


def _ffn(x1, mods, mod_row, wts, tm, tf):
    bsz, n, _ = x1.shape
    g_ff, w_ff1, w_ff2, g_final = wts

    def mod(col):
        return pl.BlockSpec((1, 1, D_MODEL), lambda b, i, j: (mod_row(b), 0, col))

    tok = pl.BlockSpec((1, tm, D_MODEL), lambda b, i, j: (b, i, 0))
    return pl.pallas_call(
        _ffn_kernel,
        grid=(bsz, n // tm, D_FF // tf),
        in_specs=[tok, mod(3), mod(4), mod(5), _const_spec((1, D_MODEL)),
                  pl.BlockSpec((D_MODEL, tf), lambda b, i, j: (0, j)),
                  pl.BlockSpec((tf, D_MODEL), lambda b, i, j: (j, 0)),
                  _const_spec((1, D_MODEL))],
        out_specs=tok,
        out_shape=jax.ShapeDtypeStruct((bsz, n, D_MODEL), F32),
        scratch_shapes=[pltpu.VMEM((tm, D_MODEL), F32), pltpu.VMEM((tm, D_MODEL), BF16)],
        compiler_params=_cparams(("parallel", "parallel", "arbitrary")),
        name="ffn",
    )(x1, mods, mods, mods, g_ff, w_ff1, w_ff2, g_final)


def _pad_head_cols(w, widths_in, layout):
    rows = w.shape[0]
    w = w.reshape(rows, MLA_HEADS, sum(width for _, width in widths_in))
    pieces, off = {}, 0
    for name, width in widths_in:
        pieces[name] = w[:, :, off:off + width]
        off += width
    cols = [pieces[name] if name in pieces else jnp.zeros((rows, MLA_HEADS, width), w.dtype) for name, width in layout]
    return jnp.concatenate(cols, axis=-1).reshape(rows, -1)


def _prep_weights(w_in, w_uq, w_ukv):
    w_lat = w_in[:, :_LAT_W].astype(BF16)
    w_kr = jnp.pad(w_in[:, _LAT_W:_LAT_W + QK_ROPE].astype(BF16), ((0, 0), (0, LANES - QK_ROPE)))
    w_mix = w_in[:, _LAT_W + QK_ROPE:].astype(BF16)
    w_in_p = (w_lat, w_kr, w_mix)
    w_uq_p = _pad_head_cols(w_uq, (("nope", QK_NOPE), ("rope", QK_ROPE)),
                            (("rope", QK_ROPE), ("zero", HEAD_PAD - QK_ROPE - QK_NOPE), ("nope", QK_NOPE))).astype(BF16)
    kv_in = (("nope", QK_NOPE), ("v", V_HEAD))
    k_cols = _pad_head_cols(w_ukv, kv_in, (("zero", HEAD_PAD - QK_NOPE), ("nope", QK_NOPE)))
    v_cols = _pad_head_cols(w_ukv, kv_in, (("v", V_HEAD), ("zero", HEAD_PAD - V_HEAD)))
    w_ukv_p = jnp.concatenate([k_cols, v_cols], axis=1).astype(BF16)
    return w_in_p, w_uq_p, w_ukv_p


def _rope_tables(n):
    rows = n // GRID_W
    row = jnp.repeat(jnp.arange(rows, dtype=F32), GRID_W)
    col = jnp.tile(jnp.arange(GRID_W, dtype=F32), rows)
    half = QK_ROPE // 2
    inv = ROPE_BASE ** (-jnp.arange(0, half, 2, dtype=F32) / half)
    ang = jnp.concatenate([row[:, None] * inv, col[:, None] * inv], axis=-1)
    cos = jnp.repeat(jnp.cos(ang), 2, axis=-1)
    sin = jnp.repeat(jnp.sin(ang), 2, axis=-1)
    cos = jnp.concatenate([cos, jnp.ones((n, LANES - QK_ROPE), F32)], axis=-1)
    sin = jnp.concatenate([sin, jnp.zeros((n, LANES - QK_ROPE), F32)], axis=-1)
    return cos, sin


def _trunk(x, mods, mod_row, wts, rope_tabs, ctx):
    (inproj_w, merge_w, ffn_w, w_ukv_p) = wts
    bsz, n, _ = x.shape
    tm = 256
    past = 0 if ctx is None else ctx[0].shape[1]
    outs = _inproj(x, mods, mod_row, inproj_w, rope_tabs, ctx is None, tm, n + past)
    q, k, v, qh, kf, lff, kb, lfb, vh, sg, gm, gh = outs[:12]
    if ctx is None:
        ckv, krope = outs[12:]
        s0f = s0b = jnp.zeros((bsz, N_PAIRS, LANES, LANES), F32)
        bk = n
    else:
        ckv_c, krope_c, state_c = ctx
        kr_pad = jnp.pad(krope_c, ((0, 0), (0, 0), (0, LANES - QK_ROPE)))
        k, v = _cache_kv(ckv_c, kr_pad, w_ukv_p, k, v)
        s0f = _state_to_pairs(state_c[:, 0])
        s0b = _state_to_pairs(state_c[:, 1])
        bk = past
    o_mla = _attention(q, k, v, min(n, 1024), bk)
    o_f, *sf = _hgrn(qh, kf, lff, vh, s0f, False, ctx is None)
    o_b, *sb = _hgrn(qh, kb, lfb, vh, s0b, True, ctx is None)
    x1 = _merge(x, mods, mod_row, o_mla, o_f, o_b, sg, gm, gh, merge_w, tm)
    if ctx is None:
        x1 = x1.reshape(1, bsz * n, D_MODEL)
    y = _ffn(x1, mods, mod_row, ffn_w, min(1024, x1.shape[1]), 1024).reshape(bsz, n, D_MODEL)
    if ctx is None:
        return y, (ckv, krope, jnp.stack([sf[0], sb[0]], axis=1))
    return y, None


def kernel(x_prompt, x_sample, cache_ckv, cache_krope, state_hgrn, c, c_ctx, w_ada, b_ada, g_norm_mix, g_norm_ff, w_in, g_q_norm, w_uq, g_kv_norm, w_ukv, g_hg_norm, hg_lb_logits, w_br_mla, w_br_hg, w_out, w_ff1, w_ff2, g_final):
    assert w_in.shape[0] == 1, "single-layer trunk"
    dec_b = c.shape[0]
    cc = jnp.concatenate([c, c_ctx[None, :], jnp.zeros((8 - dec_b - 1, D_MODEL), F32)], axis=0)
    mods = _ada(cc, w_ada[0], b_ada[0][None, :]).reshape(8, 1, 6 * D_MODEL)

    w_in_p, w_uq_p, w_ukv_p = _prep_weights(w_in[0], w_uq[0], w_ukv[0])
    inproj_w = (w_in_p, g_norm_mix[0][None], g_q_norm[0][None], w_uq_p, g_kv_norm[0][None], w_ukv_p, hg_lb_logits)
    head_id = np.arange(HG_W) // HG_DV
    hmean = jnp.asarray((head_id[:, None] == head_id[None, :]).astype(np.float32) / HG_DV, BF16)
    merge_w = (g_hg_norm[0][None], hmean, w_br_mla[0].astype(BF16), w_br_hg[0].astype(BF16), w_out[0].astype(BF16))
    ffn_w = (g_norm_ff[0][None], w_ff1[0].astype(BF16), w_ff2[0].astype(BF16), g_final[None])
    wts = (inproj_w, merge_w, ffn_w, w_ukv_p)

    y_prompt, (ckv, krope, state) = _trunk(x_prompt, mods, lambda b: dec_b, wts, None, None)
    rope_tabs = _rope_tables(x_sample.shape[1])
    y_sample, _ = _trunk(x_sample, mods, lambda b: b, wts, rope_tabs,
                         (cache_ckv[:, 0], cache_krope[:, 0], state_hgrn[:, 0]))
    return (y_prompt, y_sample, ckv[:, None], krope[:, None], state[:, None])
```

```python
import functools

import numpy as np
import jax
import jax.numpy as jnp
from jax import lax
from jax.experimental import pallas as pl
from jax.experimental.pallas import tpu as pltpu

D_MODEL = 1024
GRID_W = 64
MLA_HEADS = 8
Q_LORA = 384
KV_LORA = 256
QK_NOPE = 64
QK_ROPE = 32
V_HEAD = 64
MLA_W = MLA_HEADS * V_HEAD
MLA_SCALE = (QK_NOPE + QK_ROPE) ** -0.5
LOG2_E = 1.4426950408889634
HG_HEADS = 8
HG_DK = 64
HG_DV = 64
HG_KW = HG_HEADS * HG_DK
HG_W = HG_HEADS * HG_DV
D_FF = 4 * D_MODEL
ROPE_BASE = 10000.0
EPS = 1e-6

LANES = 128
HEAD_PAD = LANES
QK_PAD_W = MLA_HEADS * HEAD_PAD
N_PAIRS = HG_HEADS // 2
VMEM_LIMIT = 56 * 1024 * 1024

_LAT_W = Q_LORA + KV_LORA
_MIX_SIZES = (HG_KW, HG_KW, HG_KW, HG_W, HG_W, D_MODEL, D_MODEL)
_MIX_OFFS = tuple(int(o) for o in np.cumsum((0,) + _MIX_SIZES))

HG_CHUNK = 128
HG_LEVELS = 7

F32 = jnp.float32
BF16 = jnp.bfloat16


def _cparams(sem):
    return pltpu.CompilerParams(dimension_semantics=sem, vmem_limit_bytes=VMEM_LIMIT)


def _rms(x, g):
    return x * lax.rsqrt(jnp.mean(x * x, axis=-1, keepdims=True) + EPS) * g


def _dot(a, b):
    return jnp.dot(a, b, preferred_element_type=F32)


def _dot_nt(a, b):
    return lax.dot_general(a, b, (((1,), (1,)), ((), ())), preferred_element_type=F32)


def _dot_tn(a, b):
    return lax.dot_general(a, b, (((0,), (0,)), ((), ())), preferred_element_type=F32)


def _split_hi_lo(x):
    hi = x.astype(BF16)
    lo = (x - hi.astype(F32)).astype(BF16)
    return hi, lo


def _ada_kernel(c_ref, w_ref, b_ref, o_ref):
    c = c_ref[...]
    a = c * jax.nn.sigmoid(c)
    a_hi, a_lo = _split_hi_lo(a)
    w_hi, w_lo = _split_hi_lo(w_ref[...])
    o_ref[...] = _dot(a_hi, w_hi) + _dot(a_hi, w_lo) + _dot(a_lo, w_hi) + b_ref[...]


def _ada(cc, w_ada, b_ada):
    rows, tn = cc.shape[0], 1536
    n = w_ada.shape[1]
    return pl.pallas_call(
        _ada_kernel,
        grid=(n // tn,),
        in_specs=[pl.BlockSpec((rows, D_MODEL), lambda j: (0, 0)),
                  pl.BlockSpec((D_MODEL, tn), lambda j: (0, j)),
                  pl.BlockSpec((1, tn), lambda j: (0, j))],
        out_specs=pl.BlockSpec((rows, tn), lambda j: (0, j)),
        out_shape=jax.ShapeDtypeStruct((rows, n), F32),
        compiler_params=_cparams(("arbitrary",)),
        name="ada",
    )(cc, w_ada, b_ada)


def _rope_tile(blk, cos, sin, even):
    rot = jnp.where(even, -pltpu.roll(blk, LANES - 1, 1), pltpu.roll(blk, 1, 1))
    return blk * cos + rot * sin


def _run_staggered(stages):
    pending = stages[0][0]()
    for i, (_, consume) in enumerate(stages):
        current = pending
        if i + 1 < len(stages):
            pending = stages[i + 1][0]()
        consume(current)


def _store_kv(kv, kr, k_ref, v_ref):
    one_lane = ((lax.broadcasted_iota(jnp.int32, (1, QK_PAD_W), 1) & (HEAD_PAD - 1)) == V_HEAD).astype(F32)
    for hh in range(MLA_HEADS):
        sl = slice(hh * HEAD_PAD, (hh + 1) * HEAD_PAD)
        k_ref[0, :, sl] = (kv[:, sl] + kr).astype(BF16)
        vsl = slice(QK_PAD_W + hh * HEAD_PAD, QK_PAD_W + (hh + 1) * HEAD_PAD)
        v_ref[0, sl, :] = (kv[:, vsl] + one_lane[:, sl]).T.astype(BF16)


def _inproj_kernel(*refs, rope, cache_out):
    (x_ref, sh_ref, sc_ref, gmix_ref, wlat_ref, wkr_ref, wmix_ref, gq_ref, wuq_ref, gkv_ref, wukv_ref, lbl_ref) = refs[:12]
    refs = refs[12:]
    if rope:
        cos_ref, sin_ref = refs[:2]
        refs = refs[2:]
    (q_ref, k_ref, v_ref, qh_ref, kf_ref, lff_ref, kb_ref, lfb_ref, vh_ref, sg_ref, gm_ref, gh_ref) = refs[:12]
    refs = refs[12:]

    x = x_ref[0]
    h = _rms(x, gmix_ref[...]) * (1.0 + sc_ref[0]) + sh_ref[0]
    hb = h.astype(BF16)

    def proj(i):
        if i == 0:
            return _dot(hb, wlat_ref[:, :Q_LORA])
        if i == 1:
            return _dot(hb, wlat_ref[:, Q_LORA:])
        if i == 2:
            return _dot(hb, wkr_ref[...])
        return _dot(hb, wmix_ref[:, _MIX_OFFS[i - 3]:_MIX_OFFS[i - 2]])

    if rope:
        cos = cos_ref[...]
        sin = sin_ref[...]
        even = (lax.broadcasted_iota(jnp.int32, cos.shape, 1) & 1) == 0
    kept = {}

    def use_q_lat(y):
        kept["qn"] = _rms(y, gq_ref[...]).astype(BF16)

    def use_kv_lat(y):
        kept["ckv"] = _rms(y, gkv_ref[...])
        if cache_out:
            refs[0][0] = kept["ckv"]

    def use_k_rope(kr):
        if cache_out:
            refs[1][0] = kr[:, :QK_ROPE]
        kept["kr"] = _rope_tile(kr, cos, sin, even) if rope else kr

    def use_q(q):
        q = q * (MLA_SCALE * LOG2_E)
        for hh in range(MLA_HEADS):
            blk = q[:, hh * HEAD_PAD:(hh + 1) * HEAD_PAD]
            if rope:
                blk = _rope_tile(blk, cos, sin, even)
            q_ref[0, hh * HEAD_PAD:(hh + 1) * HEAD_PAD, :] = blk.T.astype(BF16)

    def use_hq(hq):
        qh_ref[0] = hq * jax.nn.sigmoid(hq) * (HG_DK ** -0.5)

    def use_forget(d, k_out, lf_out):
        def use(z):
            l0, l1 = lbl_ref[0, d:d + 1], lbl_ref[1, d:d + 1]
            lmax = jnp.maximum(l0, l1)
            e0, e1 = jnp.exp(l0 - lmax), jnp.exp(l1 - lmax)
            lbd = e0 / (e0 + e1)
            f = lbd + (1.0 - lbd) * jax.nn.sigmoid(z)
            k_out[0] = 1.0 - f
            lf_out[0] = jnp.log(f)
        return use

    def use_hi(y):
        vh_ref[0] = y

    def use_hg(hg):
        sg_ref[0] = hg * jax.nn.sigmoid(hg)

    def use_gate(out_ref):
        def use(y):
            out_ref[0] = jax.nn.sigmoid(y)
        return use

    stages = [
        (lambda: proj(0), use_q_lat),
        (lambda: proj(1), use_kv_lat),
        (lambda: proj(2), use_k_rope),
        (lambda: _dot(kept["qn"], wuq_ref[...]), use_q),
        (lambda: _dot(kept["ckv"].astype(BF16), wukv_ref[...]), lambda kv: _store_kv(kv, kept["kr"], k_ref, v_ref)),
        (lambda: proj(3), use_hq),
        (lambda: proj(4), use_forget(0, kf_ref, lff_ref)),
        (lambda: proj(5), use_forget(1, kb_ref, lfb_ref)),
        (lambda: proj(6), use_hi),
        (lambda: proj(7), use_hg),
        (lambda: proj(8), use_gate(gm_ref)),
        (lambda: proj(9), use_gate(gh_ref)),
    ]
    _run_staggered(stages)


def _const_spec(shape):
    return pl.BlockSpec(shape, lambda *_: (0,) * len(shape))


def _inproj(x, mods, mod_row, wts, rope_tabs, cache_out, tm, kv_rows):
    bsz, n, _ = x.shape
    rope = rope_tabs is not None
    (w_lat, w_kr, w_mix), g_mix, g_q, w_uq_p, g_kv, w_ukv_p, lb_logits = wts

    def tok(width):
        return pl.BlockSpec((1, tm, width), lambda b, i: (b, i, 0))

    def mod(col):
        return pl.BlockSpec((1, 1, D_MODEL), lambda b, i: (mod_row(b), 0, col))

    in_specs = [tok(D_MODEL), mod(0), mod(1), _const_spec((1, D_MODEL)), _const_spec(w_lat.shape),
                _const_spec(w_kr.shape), _const_spec(w_mix.shape),
                _const_spec((1, Q_LORA)), _const_spec(w_uq_p.shape), _const_spec((1, KV_LORA)),
                _const_spec(w_ukv_p.shape), _const_spec(lb_logits.shape)]
    args = [x, mods, mods, g_mix, w_lat, w_kr, w_mix, g_q, w_uq_p, g_kv, w_ukv_p, lb_logits]
    if rope:
        in_specs += [pl.BlockSpec((tm, LANES), lambda b, i: (i, 0))] * 2
        args += list(rope_tabs)

    widths = [(QK_PAD_W, BF16)] * 3 + [(HG_KW, F32)] * 7 + [(D_MODEL, F32)] * 2
    if cache_out:
        widths += [(KV_LORA, F32), (QK_ROPE, F32)]
    out_specs = [tok(w) for w, _ in widths]
    out_shape = [jax.ShapeDtypeStruct((bsz, n, w), dt) for w, dt in widths]
    feat_major = pl.BlockSpec((1, QK_PAD_W, tm), lambda b, i: (b, 0, i))
    out_specs[0], out_shape[0] = feat_major, jax.ShapeDtypeStruct((bsz, QK_PAD_W, n), BF16)
    out_shape[1] = jax.ShapeDtypeStruct((bsz, kv_rows, QK_PAD_W), BF16)
    out_specs[2], out_shape[2] = feat_major, jax.ShapeDtypeStruct((bsz, QK_PAD_W, kv_rows), BF16)
    return pl.pallas_call(
        functools.partial(_inproj_kernel, rope=rope, cache_out=cache_out),
        grid=(bsz, n // tm),
        in_specs=in_specs, out_specs=out_specs, out_shape=out_shape,
        compiler_params=_cparams(("parallel", "parallel")),
        name="inproj_rope" if rope else "inproj",
    )(*args)


def _cache_kv_kernel(ckv_ref, kr_ref, wukv_ref, k_all_ref, v_all_ref, k_ref, v_ref):
    del k_all_ref, v_all_ref
    _store_kv(_dot(ckv_ref[0].astype(BF16), wukv_ref[...]), kr_ref[0], k_ref, v_ref)


def _cache_kv(ckv_c, krope_c_pad, w_ukv_p, k_all, v_all):
    bsz, past, _ = ckv_c.shape
    tail = (k_all.shape[1] - past) // past
    assert (tail + 1) * past == k_all.shape[1] == v_all.shape[2]
    k_spec = pl.BlockSpec((1, past, QK_PAD_W), lambda b: (b, tail, 0))
    v_spec = pl.BlockSpec((1, QK_PAD_W, past), lambda b: (b, 0, tail))
    return pl.pallas_call(
        _cache_kv_kernel,
        grid=(bsz,),
        in_specs=[pl.BlockSpec((1, past, KV_LORA), lambda b: (b, 0, 0)),
                  pl.BlockSpec((1, past, LANES), lambda b: (b, 0, 0)),
                  _const_spec(w_ukv_p.shape),
                  pl.BlockSpec(memory_space=pl.ANY), pl.BlockSpec(memory_space=pl.ANY)],
        out_specs=[k_spec, v_spec],
        out_shape=[jax.ShapeDtypeStruct(k_all.shape, BF16), jax.ShapeDtypeStruct(v_all.shape, BF16)],
        input_output_aliases={3: 0, 4: 1},
        compiler_params=_cparams(("parallel",)),
        name="cache_kv",
    )(ckv_c, krope_c_pad, w_ukv_p, k_all, v_all)


def _attn_kernel(q_ref, k_ref, v_ref, o_ref, m_scr, acc_scr):
    j = pl.program_id(2)

    @pl.when(j == 0)
    def _():
        m_scr[...] = jnp.full(m_scr.shape, -jnp.inf, F32)
        acc_scr[...] = jnp.zeros(acc_scr.shape, F32)

    def scores(hh):
        sl = slice(hh * HEAD_PAD, (hh + 1) * HEAD_PAD)
        return _dot(k_ref[0, :, sl], q_ref[0, sl, :])

    ahead = 2
    pending = [scores(hh) for hh in range(ahead)]
    for hh in range(MLA_HEADS):
        sl = slice(hh * HEAD_PAD, (hh + 1) * HEAD_PAD)
        s = pending.pop(0)
        if hh + ahead < MLA_HEADS:
            pending.append(scores(hh + ahead))
        m_prev = m_scr[hh]
        m_cur = jnp.maximum(m_prev, jnp.max(s, axis=0, keepdims=True))
        alpha = jnp.exp2(m_prev - m_cur)
        p = jnp.exp2(s - m_cur[:1])
        m_scr[hh] = m_cur
        acc_scr[hh] = alpha[:1] * acc_scr[hh] + _dot(v_ref[0, sl, :], p.astype(BF16))

    @pl.when(j == pl.num_programs(2) - 1)
    def _():
        def head_out(hh):
            acc = acc_scr[hh]
            return acc[:V_HEAD] / acc[V_HEAD:V_HEAD + 1]

        for pair in range(MLA_HEADS // 2):
            both = jnp.concatenate([head_out(2 * pair), head_out(2 * pair + 1)], axis=0)
            o_ref[0, :, pair * LANES:(pair + 1) * LANES] = both.T.astype(o_ref.dtype)


def _attention(q, k, v, bq, bk):
    bsz, _, nq = q.shape
    nk = k.shape[1]
    return pl.pallas_call(
        _attn_kernel,
        grid=(bsz, nq // bq, nk // bk),
        in_specs=[pl.BlockSpec((1, QK_PAD_W, bq), lambda b, i, j: (b, 0, i)),
                  pl.BlockSpec((1, bk, QK_PAD_W), lambda b, i, j: (b, j, 0)),
                  pl.BlockSpec((1, QK_PAD_W, bk), lambda b, i, j: (b, 0, j))],
        out_specs=pl.BlockSpec((1, bq, MLA_W), lambda b, i, j: (b, i, 0)),
        out_shape=jax.ShapeDtypeStruct((bsz, nq, MLA_W), BF16),
        scratch_shapes=[pltpu.VMEM((MLA_HEADS, 8, bq), F32),
                        pltpu.VMEM((MLA_HEADS, HEAD_PAD, bq), F32)],
        compiler_params=_cparams(("parallel", "parallel", "arbitrary")),
        name="attn",
    )(q, k, v)


def _hgrn_constants(reverse):
    c = HG_CHUNK
    t = np.arange(c)
    mats = [(t[None, :] <= t[:, None])]
    for lvl in range(1, HG_LEVELS + 1):
        g, half = 1 << lvl, 1 << (lvl - 1)
        p = t % g
        mid = t - p + half - 1
        isq = p >= half
        u = t[None, :]
        mats.append(np.where(isq[:, None], (u > mid[:, None]) & (u <= t[:, None]),
                             (u > t[:, None]) & (u <= mid[:, None])))
    mstack = np.stack(mats).astype(np.float32)
    if reverse:
        mstack = mstack[:, ::-1, ::-1]
    mstack = mstack.reshape(-1, c)
    return jnp.asarray(np.concatenate([mstack, mstack], axis=1), BF16)


def _hgrn_kernel(qf_ref, qb_ref, kf_ref, lff_ref, kb_ref, lfb_ref, vf_ref, vb_ref, s0_ref, mstack_ref,
                 of_ref, ob_ref, *rest, emit_state):
    st_scr = rest[-1]
    c = HG_CHUNK
    step = pl.program_id(1)

    @pl.when(step == 0)
    def _():
        st_scr[...] = s0_ref[0]

    dirs = ((qf_ref, kf_ref, lff_ref, vf_ref, of_ref, False), (qb_ref, kb_ref, lfb_ref, vb_ref, ob_ref, True))
    exps = []
    for d, (_, _, lf_ref, _, _, _) in enumerate(dirs):
        lf_hi, lf_lo = _split_hi_lo(lf_ref[0])
        exps.append(_dot(mstack_ref[d], jnp.concatenate([lf_hi, lf_lo], axis=0)))

    for pair in range(N_PAIRS):
        for d, (q_ref, k_ref, _, v_ref, o_ref, reverse) in enumerate(dirs):
            _hgrn_pair(q_ref, k_ref, v_ref, o_ref, exps[d], st_scr.at[d], pair, reverse)

    if emit_state:
        sfin_ref = rest[0]

        @pl.when(step == pl.num_programs(1) - 1)
        def _():
            for d in range(2):
                for pair in range(N_PAIRS):
                    s_pair = st_scr[d, pair].T
                    sfin_ref[0, d, 2 * pair] = s_pair[:HG_DK, :HG_DV]
                    sfin_ref[0, d, 2 * pair + 1] = pltpu.roll(s_pair, HG_DV, 1)[HG_DK:, :HG_DV]


def _hgrn_pair(q_ref, k_ref, v_ref, o_ref, e, st_scr, pair, reverse):
    c = HG_CHUNK
    last = 0 if reverse else c - 1
    row = lax.broadcasted_iota(jnp.int32, (c, LANES), 0)
    lane_low = lax.broadcasted_iota(jnp.int32, (c, LANES), 1) < HG_DK
    blank_t = jnp.zeros((HG_DK, c), BF16)
    xor = lax.broadcasted_iota(jnp.int32, (c, c), 0) ^ lax.broadcasted_iota(jnp.int32, (c, c), 1)
    diag_block = (lax.broadcasted_iota(jnp.int32, (LANES, LANES), 0) < HG_DV) == (
        lax.broadcasted_iota(jnp.int32, (LANES, LANES), 1) < HG_DK)
    zeros8 = jnp.zeros((8, LANES), F32)

    def q_side_block(b):
        return (b % 2 == 1) != reverse

    sl = slice(pair * LANES, (pair + 1) * LANES)
    q = q_ref[0, :, sl]
    k = k_ref[0, :, sl]
    v = v_ref[0, :, sl]
    products = []
    for lvl in range(HG_LEVELS, -1, -1):
        half = (1 << lvl) // 2
        if lvl == 0:
            lhs, zk = q.astype(BF16), k
        elif half >= 8:
            z = jnp.exp(e[lvl * c:(lvl + 1) * c, sl])
            ys, zs = [], []
            for b in range(c // half):
                rs = slice(b * half, (b + 1) * half)
                blank = jnp.concatenate([zeros8] * (half // 8), axis=0)
                if q_side_block(b):
                    ys.append(q[rs] * z[rs])
                    zs.append(blank)
                else:
                    ys.append(blank)
                    zs.append(k[rs] * z[rs])
            lhs = jnp.concatenate(ys, axis=0).astype(BF16)
            zk = jnp.concatenate(zs, axis=0)
        else:
            z = jnp.exp(e[lvl * c:(lvl + 1) * c, sl])
            q_row = ((row & half) == 0) if reverse else ((row & half) != 0)
            y = jnp.where(q_row, q, k) * z
            lhs = y.astype(BF16)
            zk = jnp.where(q_row, 0.0, y)
        zt = zk.T.astype(BF16)
        rhs_t = jnp.concatenate([jnp.concatenate([zt[:HG_DK], blank_t], axis=0),
                                 jnp.concatenate([blank_t, zt[HG_DK:]], axis=0)], axis=1)
        products.append(_dot(lhs, rhs_t))
    a_cat = None
    for lvl, p in zip(range(HG_LEVELS, -1, -1), products):
        halves = (p[:, :c], p[:, c:])
        if a_cat is None:
            a_cat = halves
        else:
            same = xor < (1 << lvl)
            a_cat = tuple(jnp.where(same, p_h, a_h) for p_h, a_h in zip(halves, a_cat))
    a_cat = jnp.concatenate(a_cat, axis=1)
    cum = e[0:c, sl]
    cum_last = cum[last:last + 1]
    v_cat = jnp.concatenate([jnp.where(lane_low, v, 0.0), jnp.where(lane_low, 0.0, v)], axis=0).astype(BF16)
    st = st_scr[pair]
    o = _dot(a_cat.astype(BF16), v_cat) + _dot_nt((q * jnp.exp(cum)).astype(BF16), st.astype(BF16))
    o_ref[0, :, sl] = o
    kd = (k * jnp.exp(cum_last - cum)).astype(BF16)
    upd = _dot_tn(v.astype(BF16), kd)
    st_scr[pair] = jnp.exp(cum_last) * st + jnp.where(diag_block, upd, 0.0)


def _hgrn(qh, kf, lff, kb, lfb, vh, s0t, emit_state):
    bsz, n, _ = qh.shape
    c = HG_CHUNK
    nc = n // c
    mstack = jnp.stack([_hgrn_constants(False), _hgrn_constants(True)])
    fwd = pl.BlockSpec((1, c, HG_KW), lambda b, i: (b, i, 0))
    bwd = pl.BlockSpec((1, c, HG_KW), lambda b, i: (b, nc - 1 - i, 0))
    st_spec = pl.BlockSpec((1, 2, N_PAIRS, LANES, LANES), lambda b, i: (b, 0, 0, 0, 0))
    out_specs = [fwd, bwd]
    out_shape = [jax.ShapeDtypeStruct((bsz, n, HG_W), F32)] * 2
    if emit_state:
        out_specs.append(pl.BlockSpec((1, 2, HG_HEADS, HG_DK, HG_DV), lambda b, i: (b, 0, 0, 0, 0)))
        out_shape.append(jax.ShapeDtypeStruct((bsz, 2, HG_HEADS, HG_DK, HG_DV), F32))
    return pl.pallas_call(
        functools.partial(_hgrn_kernel, emit_state=emit_state),
        grid=(bsz, nc),
        in_specs=[fwd, bwd, fwd, fwd, bwd, bwd, fwd, bwd, st_spec, _const_spec(mstack.shape)],
        out_specs=out_specs, out_shape=out_shape,
        scratch_shapes=[pltpu.VMEM((2, N_PAIRS, LANES, LANES), F32)],
        compiler_params=_cparams(("parallel", "arbitrary")),
        name="hgrn",
    )(qh, qh, kf, lff, kb, lfb, vh, vh, s0t, mstack)


def _state_to_pairs(s):
    b = s.shape[0]
    st = jnp.swapaxes(s, -1, -2).reshape(b, N_PAIRS, 2, HG_DV, HG_DK)
    eye = jnp.eye(2, dtype=s.dtype)
    out = jnp.einsum('bpavk,ac->bpavck', st, eye)
    return out.reshape(b, N_PAIRS, 2 * HG_DV, 2 * HG_DK)


def _merge_kernel(x_ref, ga_ref, om_ref, of_ref, ob_ref, sg_ref, gm_ref, gh_ref, ghg_ref, hmean_ref,
                  wbm_ref, wbh_ref, wout_ref, x1_ref):
    o = of_ref[0] + ob_ref[0]
    sq_hi, sq_lo = _split_hi_lo(o * o)
    ms = _dot(sq_hi, hmean_ref[...]) + _dot(sq_lo, hmean_ref[...])
    o_hg = (o * lax.rsqrt(ms + EPS) * ghg_ref[...]) * sg_ref[0]
    merged = gm_ref[0] * _dot(om_ref[0], wbm_ref[...]) + gh_ref[0] * _dot(o_hg.astype(BF16), wbh_ref[...])
    out = _dot(merged.astype(BF16), wout_ref[...])
    x1_ref[0] = x_ref[0] + ga_ref[0] * out


def _merge(x, mods, mod_row, o_mla, o_f, o_b, sg, gm, gh, wts, tm):
    bsz, n, _ = x.shape
    g_hg, hmean, w_br_mla, w_br_hg, w_out = wts

    def tok(width):
        return pl.BlockSpec((1, tm, width), lambda b, i: (b, i, 0))

    return pl.pallas_call(
        _merge_kernel,
        grid=(bsz, n // tm),
        in_specs=[tok(D_MODEL), pl.BlockSpec((1, 1, D_MODEL), lambda b, i: (mod_row(b), 0, 2)),
                  tok(MLA_W), tok(HG_W), tok(HG_W), tok(HG_W), tok(D_MODEL), tok(D_MODEL),
                  _const_spec((1, HG_W)), _const_spec(hmean.shape), _const_spec(w_br_mla.shape),
                  _const_spec(w_br_hg.shape), _const_spec(w_out.shape)],
        out_specs=tok(D_MODEL),
        out_shape=jax.ShapeDtypeStruct((bsz, n, D_MODEL), F32),
        compiler_params=_cparams(("parallel", "parallel")),
        name="merge",
    )(x, mods, o_mla, o_f, o_b, sg, gm, gh, g_hg, hmean, w_br_mla, w_br_hg, w_out)


def _ffn_kernel(x_ref, sh_ref, sc_ref, ga_ref, gff_ref, w1_ref, w2_ref, gfin_ref, y_ref, acc_scr, h_scr):
    j = pl.program_id(2)

    @pl.when(j == 0)
    def _():
        h = _rms(x_ref[0], gff_ref[...]) * (1.0 + sc_ref[0]) + sh_ref[0]
        h_scr[...] = h.astype(BF16)
        acc_scr[...] = jnp.zeros(acc_scr.shape, F32)

    a = jnp.maximum(_dot(h_scr[...], w1_ref[...]), 0.0)
    acc_scr[...] += _dot((a * a).astype(BF16), w2_ref[...])

    @pl.when(j == pl.num_programs(2) - 1)
    def _():
        x2 = x_ref[0] + ga_ref[0] * acc_scr[...]
        y_ref[0] = _rms(x2, gfin_ref[...])


def _ffn(x1, mods, mod_row, wts, tm, tf):
    bsz, n, _ = x1.shape
    g_ff, w_ff1, w_ff2, g_final = wts

    def mod(col):
        return pl.BlockSpec((1, 1, D_MODEL), lambda b, i, j: (mod_row(b), 0, col))

    tok = pl.BlockSpec((1, tm, D_MODEL), lambda b, i, j: (b, i, 0))
    return pl.pallas_call(
        _ffn_kernel,
        grid=(bsz, n // tm, D_FF // tf),
        in_specs=[tok, mod(3), mod(4), mod(5), _const_spec((1, D_MODEL)),
                  pl.BlockSpec((D_MODEL, tf), lambda b, i, j: (0, j)),
                  pl.BlockSpec((tf, D_MODEL), lambda b, i, j: (j, 0)),
                  _const_spec((1, D_MODEL))],
        out_specs=tok,
        out_shape=jax.ShapeDtypeStruct((bsz, n, D_MODEL), F32),
        scratch_shapes=[pltpu.VMEM((tm, D_MODEL), F32), pltpu.VMEM((tm, D_MODEL), BF16)],
        compiler_params=_cparams(("parallel", "parallel", "arbitrary")),
        name="ffn",
    )(x1, mods, mods, mods, g_ff, w_ff1, w_ff2, g_final)


def _pad_head_cols(w, widths_in, layout):
    rows = w.shape[0]
    w = w.reshape(rows, MLA_HEADS, sum(width for _, width in widths_in))
    pieces, off = {}, 0
    for name, width in widths_in:
        pieces[name] = w[:, :, off:off + width]
        off += width
    cols = [pieces[name] if name in pieces else jnp.zeros((rows, MLA_HEADS, width), w.dtype) for name, width in layout]
    return jnp.concatenate(cols, axis=-1).reshape(rows, -1)


def _prep_weights(w_in, w_uq, w_ukv):
    w_lat = w_in[:, :_LAT_W].astype(BF16)
    w_kr = jnp.pad(w_in[:, _LAT_W:_LAT_W + QK_ROPE].astype(BF16), ((0, 0), (0, LANES - QK_ROPE)))
    w_mix = w_in[:, _LAT_W + QK_ROPE:].astype(BF16)
    w_in_p = (w_lat, w_kr, w_mix)
    w_uq_p = _pad_head_cols(w_uq, (("nope", QK_NOPE), ("rope", QK_ROPE)),
                            (("rope", QK_ROPE), ("zero", HEAD_PAD - QK_ROPE - QK_NOPE), ("nope", QK_NOPE))).astype(BF16)
    kv_in = (("nope", QK_NOPE), ("v", V_HEAD))
    k_cols = _pad_head_cols(w_ukv, kv_in, (("zero", HEAD_PAD - QK_NOPE), ("nope", QK_NOPE)))
    v_cols = _pad_head_cols(w_ukv, kv_in, (("v", V_HEAD), ("zero", HEAD_PAD - V_HEAD)))
    w_ukv_p = jnp.concatenate([k_cols, v_cols], axis=1).astype(BF16)
    return w_in_p, w_uq_p, w_ukv_p


def _rope_tables(n):
    rows = n // GRID_W
    row = jnp.repeat(jnp.arange(rows, dtype=F32), GRID_W)
    col = jnp.tile(jnp.arange(GRID_W, dtype=F32), rows)
    half = QK_ROPE // 2
    inv = ROPE_BASE ** (-jnp.arange(0, half, 2, dtype=F32) / half)
    ang = jnp.concatenate([row[:, None] * inv, col[:, None] * inv], axis=-1)
    cos = jnp.repeat(jnp.cos(ang), 2, axis=-1)
    sin = jnp.repeat(jnp.sin(ang), 2, axis=-1)
    cos = jnp.concatenate([cos, jnp.ones((n, LANES - QK_ROPE), F32)], axis=-1)
    sin = jnp.concatenate([sin, jnp.zeros((n, LANES - QK_ROPE), F32)], axis=-1)
    return cos, sin


def _trunk(x, mods, mod_row, wts, rope_tabs, ctx):
    (inproj_w, merge_w, ffn_w, w_ukv_p) = wts
    bsz, n, _ = x.shape
    tm = 256
    past = 0 if ctx is None else ctx[0].shape[1]
    outs = _inproj(x, mods, mod_row, inproj_w, rope_tabs, ctx is None, tm, n + past)
    q, k, v, qh, kf, lff, kb, lfb, vh, sg, gm, gh = outs[:12]
    if ctx is None:
        ckv, krope = outs[12:]
        s0 = jnp.zeros((bsz, 2, N_PAIRS, LANES, LANES), F32)
        bk = n
    else:
        ckv_c, krope_c, state_c = ctx
        kr_pad = jnp.pad(krope_c, ((0, 0), (0, 0), (0, LANES - QK_ROPE)))
        k, v = _cache_kv(ckv_c, kr_pad, w_ukv_p, k, v)
        s0 = jnp.stack([_state_to_pairs(state_c[:, 0]), _state_to_pairs(state_c[:, 1])], axis=1)
        bk = past
    o_mla = _attention(q, k, v, min(n, 1024), bk)
    o_f, o_b, *state = _hgrn(qh, kf, lff, kb, lfb, vh, s0, ctx is None)
    x1 = _merge(x, mods, mod_row, o_mla, o_f, o_b, sg, gm, gh, merge_w, tm)
    if ctx is None:
        x1 = x1.reshape(1, bsz * n, D_MODEL)
    y = _ffn(x1, mods, mod_row, ffn_w, min(1024, x1.shape[1]), 1024).reshape(bsz, n, D_MODEL)
    if ctx is None:
        return y, (ckv, krope, state[0])
    return y, None


def kernel(x_prompt, x_sample, cache_ckv, cache_krope, state_hgrn, c, c_ctx, w_ada, b_ada, g_norm_mix, g_norm_ff, w_in, g_q_norm, w_uq, g_kv_norm, w_ukv, g_hg_norm, hg_lb_logits, w_br_mla, w_br_hg, w_out, w_ff1, w_ff2, g_final):
    assert w_in.shape[0] == 1, "single-layer trunk"
    dec_b = c.shape[0]
    cc = jnp.concatenate([c, c_ctx[None, :], jnp.zeros((8 - dec_b - 1, D_MODEL), F32)], axis=0)
    mods = _ada(cc, w_ada[0], b_ada[0][None, :]).reshape(8, 1, 6 * D_MODEL)

    w_in_p, w_uq_p, w_ukv_p = _prep_weights(w_in[0], w_uq[0], w_ukv[0])
    inproj_w = (w_in_p, g_norm_mix[0][None], g_q_norm[0][None], w_uq_p, g_kv_norm[0][None], w_ukv_p, hg_lb_logits)
    head_id = np.arange(HG_W) // HG_DV
    hmean = jnp.asarray((head_id[:, None] == head_id[None, :]).astype(np.float32) / HG_DV, BF16)
    merge_w = (g_hg_norm[0][None], hmean, w_br_mla[0].astype(BF16), w_br_hg[0].astype(BF16), w_out[0].astype(BF16))
    ffn_w = (g_norm_ff[0][None], w_ff1[0].astype(BF16), w_ff2[0].astype(BF16), g_final[None])
    wts = (inproj_w, merge_w, ffn_w, w_ukv_p)

    y_prompt, (ckv, krope, state) = _trunk(x_prompt, mods, lambda b: dec_b, wts, None, None)
    rope_tabs = _rope_tables(x_sample.shape[1])
    y_sample, _ = _trunk(x_sample, mods, lambda b: b, wts, rope_tabs,
                         (cache_ckv[:, 0], cache_krope[:, 0], state_hgrn[:, 0]))
    return (y_prompt, y_sample, ckv[:, None], krope[:, None], state[:, None])
```

```python
import functools

import numpy as np
import jax
import jax.numpy as jnp
from jax import lax
from jax.experimental import pallas as pl
from jax.experimental.pallas import tpu as pltpu

D_MODEL = 1024
GRID_W = 64
MLA_HEADS = 8
Q_LORA = 384
KV_LORA = 256
QK_NOPE = 64
QK_ROPE = 32
V_HEAD = 64
MLA_W = MLA_HEADS * V_HEAD
MLA_SCALE = (QK_NOPE + QK_ROPE) ** -0.5
LOG2_E = 1.4426950408889634
HG_HEADS = 8
HG_DK = 64
HG_DV = 64
HG_KW = HG_HEADS * HG_DK
HG_W = HG_HEADS * HG_DV
D_FF = 4 * D_MODEL
ROPE_BASE = 10000.0
EPS = 1e-6

LANES = 128
HEAD_PAD = LANES
QK_PAD_W = MLA_HEADS * HEAD_PAD
N_PAIRS = HG_HEADS // 2
VMEM_LIMIT = 56 * 1024 * 1024

_LAT_W = Q_LORA + KV_LORA
_MIX_SIZES = (HG_KW, HG_KW, HG_KW, HG_W, HG_W, D_MODEL, D_MODEL)
_MIX_OFFS = tuple(int(o) for o in np.cumsum((0,) + _MIX_SIZES))

HG_CHUNK = 128
HG_LEVELS = 7

F32 = jnp.float32
BF16 = jnp.bfloat16


def _cparams(sem):
    return pltpu.CompilerParams(dimension_semantics=sem, vmem_limit_bytes=VMEM_LIMIT)


def _rms(x, g):
    return x * lax.rsqrt(jnp.mean(x * x, axis=-1, keepdims=True) + EPS) * g


def _dot(a, b):
    return jnp.dot(a, b, preferred_element_type=F32)


def _dot_nt(a, b):
    return lax.dot_general(a, b, (((1,), (1,)), ((), ())), preferred_element_type=F32)


def _dot_tn(a, b):
    return lax.dot_general(a, b, (((0,), (0,)), ((), ())), preferred_element_type=F32)


def _split_hi_lo(x):
    hi = x.astype(BF16)
    lo = (x - hi.astype(F32)).astype(BF16)
    return hi, lo


def _ada_kernel(c_ref, w_ref, b_ref, o_ref):
    c = c_ref[...]
    a = c * jax.nn.sigmoid(c)
    a_hi, a_lo = _split_hi_lo(a)
    w_hi, w_lo = _split_hi_lo(w_ref[...])
    o_ref[...] = _dot(a_hi, w_hi) + _dot(a_hi, w_lo) + _dot(a_lo, w_hi) + b_ref[...]


def _ada(cc, w_ada, b_ada):
    rows, tn = cc.shape[0], 1536
    n = w_ada.shape[1]
    return pl.pallas_call(
        _ada_kernel,
        grid=(n // tn,),
        in_specs=[pl.BlockSpec((rows, D_MODEL), lambda j: (0, 0)),
                  pl.BlockSpec((D_MODEL, tn), lambda j: (0, j)),
                  pl.BlockSpec((1, tn), lambda j: (0, j))],
        out_specs=pl.BlockSpec((rows, tn), lambda j: (0, j)),
        out_shape=jax.ShapeDtypeStruct((rows, n), F32),
        compiler_params=_cparams(("arbitrary",)),
        name="ada",
    )(cc, w_ada, b_ada)


def _rope_tile(blk, cos, sin, even):
    rot = jnp.where(even, -pltpu.roll(blk, LANES - 1, 1), pltpu.roll(blk, 1, 1))
    return blk * cos + rot * sin


def _run_staggered(stages):
    pending = stages[0][0]()
    for i, (_, consume) in enumerate(stages):
        current = pending
        if i + 1 < len(stages):
            pending = stages[i + 1][0]()
        consume(current)


def _store_kv(kv, kr, k_ref, v_ref):
    one_lane = ((lax.broadcasted_iota(jnp.int32, (1, QK_PAD_W), 1) & (HEAD_PAD - 1)) == V_HEAD).astype(F32)
    for hh in range(MLA_HEADS):
        sl = slice(hh * HEAD_PAD, (hh + 1) * HEAD_PAD)
        k_ref[0, :, sl] = (kv[:, sl] + kr).astype(BF16)
        vsl = slice(QK_PAD_W + hh * HEAD_PAD, QK_PAD_W + (hh + 1) * HEAD_PAD)
        v_ref[0, sl, :] = (kv[:, vsl] + one_lane[:, sl]).T.astype(BF16)


def _inproj_kernel(*refs, rope, cache_out):
    (x_ref, sh_ref, sc_ref, gmix_ref, wlat_ref, wkr_ref, wmix_ref, gq_ref, wuq_ref, gkv_ref, wukv_ref, lbl_ref) = refs[:12]
    refs = refs[12:]
    if rope:
        cos_ref, sin_ref = refs[:2]
        refs = refs[2:]
    (q_ref, k_ref, v_ref, qh_ref, kf_ref, lff_ref, kb_ref, lfb_ref, vh_ref, sg_ref, gm_ref, gh_ref) = refs[:12]
    refs = refs[12:]

    x = x_ref[0]
    h = _rms(x, gmix_ref[...]) * (1.0 + sc_ref[0]) + sh_ref[0]
    hb = h.astype(BF16)

    def proj(i):
        if i == 0:
            return _dot(hb, wlat_ref[:, :Q_LORA])
        if i == 1:
            return _dot(hb, wlat_ref[:, Q_LORA:])
        if i == 2:
            return _dot(hb, wkr_ref[...])
        return _dot(hb, wmix_ref[:, _MIX_OFFS[i - 3]:_MIX_OFFS[i - 2]])

    if rope:
        cos = cos_ref[...]
        sin = sin_ref[...]
        even = (lax.broadcasted_iota(jnp.int32, cos.shape, 1) & 1) == 0
    kept = {}

    def use_q_lat(y):
        kept["qn"] = _rms(y, gq_ref[...]).astype(BF16)

    def use_kv_lat(y):
        kept["ckv"] = _rms(y, gkv_ref[...])
        if cache_out:
            refs[0][0] = kept["ckv"]

    def use_k_rope(kr):
        if cache_out:
            refs[1][0] = kr[:, :QK_ROPE]
        kept["kr"] = _rope_tile(kr, cos, sin, even) if rope else kr

    def use_q(q):
        q = q * (MLA_SCALE * LOG2_E)
        for hh in range(MLA_HEADS):
            blk = q[:, hh * HEAD_PAD:(hh + 1) * HEAD_PAD]
            if rope:
                blk = _rope_tile(blk, cos, sin, even)
            q_ref[0, hh * HEAD_PAD:(hh + 1) * HEAD_PAD, :] = blk.T.astype(BF16)

    def use_hq(hq):
        qh_ref[0] = hq * jax.nn.sigmoid(hq) * (HG_DK ** -0.5)

    def use_forget(d, k_out, lf_out):
        def use(z):
            l0, l1 = lbl_ref[0, d:d + 1], lbl_ref[1, d:d + 1]
            lmax = jnp.maximum(l0, l1)
            e0, e1 = jnp.exp(l0 - lmax), jnp.exp(l1 - lmax)
            lbd = e0 / (e0 + e1)
            f = lbd + (1.0 - lbd) * jax.nn.sigmoid(z)
            k_out[0] = 1.0 - f
            lf_out[0] = jnp.log(f)
        return use

    def use_hi(y):
        vh_ref[0] = y.astype(vh_ref.dtype)

    def use_hg(hg):
        sg_ref[0] = (hg * jax.nn.sigmoid(hg)).astype(sg_ref.dtype)

    def use_gate(out_ref):
        def use(y):
            out_ref[0] = jax.nn.sigmoid(y).astype(out_ref.dtype)
        return use

    stages = [
        (lambda: proj(0), use_q_lat),
        (lambda: proj(1), use_kv_lat),
        (lambda: proj(2), use_k_rope),
        (lambda: _dot(kept["qn"], wuq_ref[...]), use_q),
        (lambda: _dot(kept["ckv"].astype(BF16), wukv_ref[...]), lambda kv: _store_kv(kv, kept["kr"], k_ref, v_ref)),
        (lambda: proj(3), use_hq),
        (lambda: proj(4), use_forget(0, kf_ref, lff_ref)),
        (lambda: proj(5), use_forget(1, kb_ref, lfb_ref)),
        (lambda: proj(6), use_hi),
        (lambda: proj(7), use_hg),
        (lambda: proj(8), use_gate(gm_ref)),
        (lambda: proj(9), use_gate(gh_ref)),
    ]
    _run_staggered(stages)


def _const_spec(shape):
    return pl.BlockSpec(shape, lambda *_: (0,) * len(shape), pipeline_mode=pl.Buffered(1))


def _inproj(x, mods, mod_row, wts, rope_tabs, cache_out, tm, kv_rows):
    bsz, n, _ = x.shape
    rope = rope_tabs is not None
    (w_lat, w_kr, w_mix), g_mix, g_q, w_uq_p, g_kv, w_ukv_p, lb_logits = wts

    def tok(width):
        return pl.BlockSpec((1, tm, width), lambda b, i: (b, i, 0))

    def mod(col):
        return pl.BlockSpec((1, 1, D_MODEL), lambda b, i: (mod_row(b), 0, col))

    in_specs = [tok(D_MODEL), mod(0), mod(1), _const_spec((1, D_MODEL)), _const_spec(w_lat.shape),
                _const_spec(w_kr.shape), _const_spec(w_mix.shape),
                _const_spec((1, Q_LORA)), _const_spec(w_uq_p.shape), _const_spec((1, KV_LORA)),
                _const_spec(w_ukv_p.shape), _const_spec(lb_logits.shape)]
    args = [x, mods, mods, g_mix, w_lat, w_kr, w_mix, g_q, w_uq_p, g_kv, w_ukv_p, lb_logits]
    if rope:
        in_specs += [pl.BlockSpec((tm, LANES), lambda b, i: (i, 0))] * 2
        args += list(rope_tabs)

    widths = [(QK_PAD_W, BF16)] * 3 + [(HG_KW, F32)] * 5 + [(HG_W, BF16)] * 2 + [(D_MODEL, BF16)] * 2
    if cache_out:
        widths += [(KV_LORA, F32), (QK_ROPE, F32)]
    out_specs = [tok(w) for w, _ in widths]
    out_shape = [jax.ShapeDtypeStruct((bsz, n, w), dt) for w, dt in widths]
    feat_major = pl.BlockSpec((1, QK_PAD_W, tm), lambda b, i: (b, 0, i))
    out_specs[0], out_shape[0] = feat_major, jax.ShapeDtypeStruct((bsz, QK_PAD_W, n), BF16)
    out_shape[1] = jax.ShapeDtypeStruct((bsz, kv_rows, QK_PAD_W), BF16)
    out_specs[2], out_shape[2] = feat_major, jax.ShapeDtypeStruct((bsz, QK_PAD_W, kv_rows), BF16)
    return pl.pallas_call(
        functools.partial(_inproj_kernel, rope=rope, cache_out=cache_out),
        grid=(bsz, n // tm),
        in_specs=in_specs, out_specs=out_specs, out_shape=out_shape,
        compiler_params=_cparams(("parallel", "parallel")),
        name="inproj_rope" if rope else "inproj",
    )(*args)


def _cache_kv_kernel(ckv_ref, kr_ref, wukv_ref, k_all_ref, v_all_ref, k_ref, v_ref):
    del k_all_ref, v_all_ref
    _store_kv(_dot(ckv_ref[0].astype(BF16), wukv_ref[...]), kr_ref[0], k_ref, v_ref)


def _cache_kv(ckv_c, krope_c_pad, w_ukv_p, k_all, v_all):
    bsz, past, _ = ckv_c.shape
    tail = (k_all.shape[1] - past) // past
    assert (tail + 1) * past == k_all.shape[1] == v_all.shape[2]
    k_spec = pl.BlockSpec((1, past, QK_PAD_W), lambda b: (b, tail, 0))
    v_spec = pl.BlockSpec((1, QK_PAD_W, past), lambda b: (b, 0, tail))
    return pl.pallas_call(
        _cache_kv_kernel,
        grid=(bsz,),
        in_specs=[pl.BlockSpec((1, past, KV_LORA), lambda b: (b, 0, 0)),
                  pl.BlockSpec((1, past, LANES), lambda b: (b, 0, 0)),
                  _const_spec(w_ukv_p.shape),
                  pl.BlockSpec(memory_space=pl.ANY), pl.BlockSpec(memory_space=pl.ANY)],
        out_specs=[k_spec, v_spec],
        out_shape=[jax.ShapeDtypeStruct(k_all.shape, BF16), jax.ShapeDtypeStruct(v_all.shape, BF16)],
        input_output_aliases={3: 0, 4: 1},
        compiler_params=_cparams(("parallel",)),
        name="cache_kv",
    )(ckv_c, krope_c_pad, w_ukv_p, k_all, v_all)


def _attn_kernel(q_ref, k_ref, v_ref, o_ref, m_scr, acc_scr):
    j = pl.program_id(2)

    @pl.when(j == 0)
    def _():
        m_scr[...] = jnp.full(m_scr.shape, -jnp.inf, F32)
        acc_scr[...] = jnp.zeros(acc_scr.shape, F32)

    def scores(hh):
        sl = slice(hh * HEAD_PAD, (hh + 1) * HEAD_PAD)
        return _dot(k_ref[0, :, sl], q_ref[0, sl, :])

    ahead = 2
    pending = [scores(hh) for hh in range(ahead)]
    for hh in range(MLA_HEADS):
        sl = slice(hh * HEAD_PAD, (hh + 1) * HEAD_PAD)
        s = pending.pop(0)
        if hh + ahead < MLA_HEADS:
            pending.append(scores(hh + ahead))
        m_prev = m_scr[hh]
        m_cur = jnp.maximum(m_prev, jnp.max(s, axis=0, keepdims=True))
        alpha = jnp.exp2(m_prev - m_cur)
        p = jnp.exp2(s - m_cur[:1])
        m_scr[hh] = m_cur
        acc_scr[hh] = alpha[:1] * acc_scr[hh] + _dot(v_ref[0, sl, :], p.astype(BF16))

    @pl.when(j == pl.num_programs(2) - 1)
    def _():
        def head_out(hh):
            acc = acc_scr[hh]
            return acc[:V_HEAD] / acc[V_HEAD:V_HEAD + 1]

        for pair in range(MLA_HEADS // 2):
            both = jnp.concatenate([head_out(2 * pair), head_out(2 * pair + 1)], axis=0)
            o_ref[0, :, pair * LANES:(pair + 1) * LANES] = both.T.astype(o_ref.dtype)


def _attention(q, k, v, bq, bk):
    bsz, _, nq = q.shape
    nk = k.shape[1]
    return pl.pallas_call(
        _attn_kernel,
        grid=(bsz, nq // bq, nk // bk),
        in_specs=[pl.BlockSpec((1, QK_PAD_W, bq), lambda b, i, j: (b, 0, i)),
                  pl.BlockSpec((1, bk, QK_PAD_W), lambda b, i, j: (b, j, 0)),
                  pl.BlockSpec((1, QK_PAD_W, bk), lambda b, i, j: (b, 0, j))],
        out_specs=pl.BlockSpec((1, bq, MLA_W), lambda b, i, j: (b, i, 0)),
        out_shape=jax.ShapeDtypeStruct((bsz, nq, MLA_W), BF16),
        scratch_shapes=[pltpu.VMEM((MLA_HEADS, 8, bq), F32),
                        pltpu.VMEM((MLA_HEADS, HEAD_PAD, bq), F32)],
        compiler_params=_cparams(("parallel", "parallel", "arbitrary")),
        name="attn",
    )(q, k, v)


HG_TILE_LEVELS = 3


def _hgrn_constants(reverse):
    c = HG_CHUNK
    t = np.arange(c)
    mats = [(t[None, :] <= t[:, None])]
    for lvl in range(1, HG_TILE_LEVELS + 1):
        g, half = 1 << lvl, 1 << (lvl - 1)
        p = t % g
        mid = t - p + half - 1
        isq = p >= half
        u = t[None, :]
        mats.append(np.where(isq[:, None], (u > mid[:, None]) & (u <= t[:, None]),
                             (u > t[:, None]) & (u <= mid[:, None])))
    mstack = np.stack(mats).astype(np.float32)
    if reverse:
        mstack = mstack[:, ::-1, ::-1]
    mstack = mstack.reshape(-1, c)
    return jnp.asarray(np.concatenate([mstack, mstack], axis=1), BF16)


def _hgrn_kernel(qf_ref, qb_ref, kf_ref, lff_ref, kb_ref, lfb_ref, vf_ref, vb_ref, s0_ref, mstack_ref,
                 of_ref, ob_ref, *rest, emit_state):
    st_scr = rest[-1]
    c = HG_CHUNK
    step = pl.program_id(1)

    @pl.when(step == 0)
    def _():
        st_scr[...] = s0_ref[0]

    dirs = ((qf_ref, kf_ref, lff_ref, vf_ref, of_ref, False), (qb_ref, kb_ref, lfb_ref, vb_ref, ob_ref, True))
    exps = []
    for d, (_, _, lf_ref, _, _, _) in enumerate(dirs):
        lf_hi, lf_lo = _split_hi_lo(lf_ref[0])
        exps.append(_dot(mstack_ref[d], jnp.concatenate([lf_hi, lf_lo], axis=0)))

    for pair in range(N_PAIRS):
        for d, (q_ref, k_ref, _, v_ref, o_ref, reverse) in enumerate(dirs):
            _hgrn_pair(q_ref, k_ref, v_ref, o_ref, exps[d], st_scr.at[d], pair, reverse)

    if emit_state:
        sfin_ref = rest[0]

        @pl.when(step == pl.num_programs(1) - 1)
        def _():
            for d in range(2):
                for pair in range(N_PAIRS):
                    s_pair = st_scr[d, pair].T
                    sfin_ref[0, d, 2 * pair] = s_pair[:HG_DK, :HG_DV]
                    sfin_ref[0, d, 2 * pair + 1] = pltpu.roll(s_pair, HG_DV, 1)[HG_DK:, :HG_DV]


def _hgrn_pair(q_ref, k_ref, v_ref, o_ref, e, st_scr, pair, reverse):
    c = HG_CHUNK
    last = 0 if reverse else c - 1
    row = lax.broadcasted_iota(jnp.int32, (c, LANES), 0)
    lane_low = lax.broadcasted_iota(jnp.int32, (c, LANES), 1) < HG_DK
    blank_t = jnp.zeros((HG_DK, c), BF16)
    xor = lax.broadcasted_iota(jnp.int32, (c, c), 0) ^ lax.broadcasted_iota(jnp.int32, (c, c), 1)
    diag_block = (lax.broadcasted_iota(jnp.int32, (LANES, LANES), 0) < HG_DV) == (
        lax.broadcasted_iota(jnp.int32, (LANES, LANES), 1) < HG_DK)
    zeros8 = jnp.zeros((8, LANES), F32)

    def q_side_block(b):
        return (b % 2 == 1) != reverse

    sl = slice(pair * LANES, (pair + 1) * LANES)
    q = q_ref[0, :, sl]
    k = k_ref[0, :, sl]
    v = v_ref[0, :, sl].astype(F32)
    cum = e[0:c, sl]
    cum_last = cum[last:last + 1]

    def blank(rows):
        return jnp.concatenate([zeros8] * (rows // 8), axis=0)

    products = []
    for lvl in range(HG_LEVELS, -1, -1):
        half = (1 << lvl) // 2
        if lvl == 0:
            lhs, zk = q.astype(BF16), k
        elif lvl > HG_TILE_LEVELS:
            ys, zs = [], []
            for b in range(0, c // half, 2):
                ref_row = (b + 1) * half if reverse else (b + 1) * half - 1
                r = cum[ref_row:ref_row + 1]
                for bb in (b, b + 1):
                    rs = slice(bb * half, (bb + 1) * half)
                    if q_side_block(bb):
                        ys.append(q[rs] * jnp.exp(cum[rs] - r))
                        zs.append(blank(half))
                    else:
                        zs.append(k[rs] * jnp.exp(r - cum[rs]))
            lhs = jnp.concatenate(ys, axis=0).astype(BF16)
            zk = jnp.concatenate(zs, axis=0)
        else:
            z = jnp.exp(e[lvl * c:(lvl + 1) * c, sl])
            q_row = ((row & half) == 0) if reverse else ((row & half) != 0)
            y = jnp.where(q_row, q, k) * z
            lhs = y.astype(BF16)
            zk = jnp.where(q_row, 0.0, y)
        zt = zk.T.astype(BF16)
        rhs_t = jnp.concatenate([jnp.concatenate([zt[:HG_DK], blank_t], axis=0),
                                 jnp.concatenate([blank_t, zt[HG_DK:]], axis=0)], axis=1)
        products.append(_dot(lhs, rhs_t))
    a_cat = None
    for lvl, p in zip(range(HG_LEVELS, -1, -1), products):
        half = (1 << lvl) // 2
        if lvl > HG_TILE_LEVELS:
            q_blocks = [bb for bb in range(c // half) if q_side_block(bb)]
            new = []
            for h_idx in range(2):
                p_h = p[:, h_idx * c:(h_idx + 1) * c]
                rows = []
                for bb in range(c // half):
                    rs = slice(bb * half, (bb + 1) * half)
                    if not q_side_block(bb):
                        rows.append(blank(half) if a_cat is None else a_cat[h_idx][rs])
                        continue
                    i = q_blocks.index(bb)
                    p_blk = p_h[i * half:(i + 1) * half]
                    if a_cat is None:
                        rows.append(p_blk)
                    else:
                        rows.append(jnp.where(xor[rs] < (1 << lvl), p_blk, a_cat[h_idx][rs]))
                new.append(jnp.concatenate(rows, axis=0))
            a_cat = tuple(new)
        else:
            same = xor < (1 << lvl)
            a_cat = tuple(jnp.where(same, p[:, h_idx * c:(h_idx + 1) * c], a_cat[h_idx]) for h_idx in range(2))
    a_cat = jnp.concatenate(a_cat, axis=1)
    v_cat = jnp.concatenate([jnp.where(lane_low, v, 0.0), jnp.where(lane_low, 0.0, v)], axis=0).astype(BF16)
    st = st_scr[pair]
    o = _dot(a_cat.astype(BF16), v_cat) + _dot_nt((q * jnp.exp(cum)).astype(BF16), st.astype(BF16))
    o_ref[0, :, sl] = o
    kd = (k * jnp.exp(cum_last - cum)).astype(BF16)
    upd = _dot_tn(v.astype(BF16), kd)
    st_scr[pair] = jnp.exp(cum_last) * st + jnp.where(diag_block, upd, 0.0)


def _hgrn(qh, kf, lff, kb, lfb, vh, s0t, emit_state):
    bsz, n, _ = qh.shape
    c = HG_CHUNK
    nc = n // c
    mstack = jnp.stack([_hgrn_constants(False), _hgrn_constants(True)])
    fwd = pl.BlockSpec((1, c, HG_KW), lambda b, i: (b, i, 0))
    bwd = pl.BlockSpec((1, c, HG_KW), lambda b, i: (b, nc - 1 - i, 0))
    st_spec = pl.BlockSpec((1, 2, N_PAIRS, LANES, LANES), lambda b, i: (b, 0, 0, 0, 0))
    out_specs = [fwd, bwd]
    out_shape = [jax.ShapeDtypeStruct((bsz, n, HG_W), F32)] * 2
    if emit_state:
        out_specs.append(pl.BlockSpec((1, 2, HG_HEADS, HG_DK, HG_DV), lambda b, i: (b, 0, 0, 0, 0)))
        out_shape.append(jax.ShapeDtypeStruct((bsz, 2, HG_HEADS, HG_DK, HG_DV), F32))
    return pl.pallas_call(
        functools.partial(_hgrn_kernel, emit_state=emit_state),
        grid=(bsz, nc),
        in_specs=[fwd, bwd, fwd, fwd, bwd, bwd, fwd, bwd, st_spec, _const_spec(mstack.shape)],
        out_specs=out_specs, out_shape=out_shape,
        scratch_shapes=[pltpu.VMEM((2, N_PAIRS, LANES, LANES), F32)],
        compiler_params=_cparams(("parallel", "arbitrary")),
        name="hgrn",
    )(qh, qh, kf, lff, kb, lfb, vh, vh, s0t, mstack)


def _state_to_pairs(s):
    b = s.shape[0]
    st = jnp.swapaxes(s, -1, -2).reshape(b, N_PAIRS, 2, HG_DV, HG_DK)
    eye = jnp.eye(2, dtype=s.dtype)
    out = jnp.einsum('bpavk,ac->bpavck', st, eye)
    return out.reshape(b, N_PAIRS, 2 * HG_DV, 2 * HG_DK)


def _merge_kernel(x_ref, ga_ref, om_ref, of_ref, ob_ref, sg_ref, gm_ref, gh_ref, ghg_ref, hmean_ref,
                  wbm_ref, wbh_ref, wout_ref, x1_ref):
    tm = x_ref.shape[1]
    halves = [slice(i * tm // 2, (i + 1) * tm // 2) for i in range(2)]
    o = [of_ref[0, rs] + ob_ref[0, rs] for rs in halves]

    def head_mean_sq(i):
        sq_hi, sq_lo = _split_hi_lo(o[i] * o[i])
        return _dot(jnp.concatenate([sq_hi, sq_lo], axis=1), hmean_ref[...])

    ms = [head_mean_sq(i) for i in range(2)]
    mla = [_dot(om_ref[0, rs], wbm_ref[...]) for rs in halves]
    hg = []
    for i, rs in enumerate(halves):
        o_hg = (o[i] * lax.rsqrt(ms[i] + EPS) * ghg_ref[...]) * sg_ref[0, rs]
        hg.append(_dot(o_hg.astype(BF16), wbh_ref[...]))
    out = []
    for i, rs in enumerate(halves):
        merged = gm_ref[0, rs] * mla[i] + gh_ref[0, rs] * hg[i]
        out.append(_dot(merged.astype(BF16), wout_ref[...]))
    for i, rs in enumerate(halves):
        x1_ref[0, rs] = x_ref[0, rs] + ga_ref[0] * out[i]


def _merge(x, mods, mod_row, o_mla, o_f, o_b, sg, gm, gh, wts, tm):
    bsz, n, _ = x.shape
    g_hg, hmean, w_br_mla, w_br_hg, w_out = wts

    def tok(width):
        return pl.BlockSpec((1, tm, width), lambda b, i: (b, i, 0))

    return pl.pallas_call(
        _merge_kernel,
        grid=(bsz, n // tm),
        in_specs=[tok(D_MODEL), pl.BlockSpec((1, 1, D_MODEL), lambda b, i: (mod_row(b), 0, 2)),
                  tok(MLA_W), tok(HG_W), tok(HG_W), tok(HG_W), tok(D_MODEL), tok(D_MODEL),
                  _const_spec((1, HG_W)), _const_spec(hmean.shape), _const_spec(w_br_mla.shape),
                  _const_spec(w_br_hg.shape), _const_spec(w_out.shape)],
        out_specs=tok(D_MODEL),
        out_shape=jax.ShapeDtypeStruct((bsz, n, D_MODEL), F32),
        compiler_params=_cparams(("parallel", "parallel")),
        name="merge",
    )(x, mods, o_mla, o_f, o_b, sg, gm, gh, g_hg, hmean, w_br_mla, w_br_hg, w_out)


def _ffn_kernel(x_ref, sh_ref, sc_ref, ga_ref, gff_ref, w1_ref, w2_ref, gfin_ref, y_ref, acc_scr, h_scr):
    j = pl.program_id(2)

    @pl.when(j == 0)
    def _():
        h = _rms(x_ref[0], gff_ref[...]) * (1.0 + sc_ref[0]) + sh_ref[0]
        h_scr[...] = h.astype(BF16)
        acc_scr[...] = jnp.zeros(acc_scr.shape, F32)

    a = jnp.maximum(_dot(h_scr[...], w1_ref[...]), 0.0)
    acc_scr[...] += _dot((a * a).astype(BF16), w2_ref[...])

    @pl.when(j == pl.num_programs(2) - 1)
    def _():
        x2 = x_ref[0] + ga_ref[0] * acc_scr[...]
        y_ref[0] = _rms(x2, gfin_ref[...])


def _ffn(x1, mods, mod_row, wts, tm, tf):
    bsz, n, _ = x1.shape
    g_ff, w_ff1, w_ff2, g_final = wts

    def mod(col):
        return pl.BlockSpec((1, 1, D_MODEL), lambda b, i, j: (mod_row(b), 0, col))

    tok = pl.BlockSpec((1, tm, D_MODEL), lambda b, i, j: (b, i, 0))
    return pl.pallas_call(
        _ffn_kernel,
        grid=(bsz, n // tm, D_FF // tf),
        in_specs=[tok, mod(3), mod(4), mod(5), _const_spec((1, D_MODEL)),
                  pl.BlockSpec((D_MODEL, tf), lambda b, i, j: (0, j)),
                  pl.BlockSpec((tf, D_MODEL), lambda b, i, j: (j, 0)),
                  _const_spec((1, D_MODEL))],
        out_specs=tok,
        out_shape=jax.ShapeDtypeStruct((bsz, n, D_MODEL), F32),
        scratch_shapes=[pltpu.VMEM((tm, D_MODEL), F32), pltpu.VMEM((tm, D_MODEL), BF16)],
        compiler_params=_cparams(("parallel", "parallel", "arbitrary")),
        name="ffn",
    )(x1, mods, mods, mods, g_ff, w_ff1, w_ff2, g_final)


def _pad_head_cols(w, widths_in, layout):
    rows = w.shape[0]
    w = w.reshape(rows, MLA_HEADS, sum(width for _, width in widths_in))
    pieces, off = {}, 0
    for name, width in widths_in:
        pieces[name] = w[:, :, off:off + width]
        off += width
    cols = [pieces[name] if name in pieces else jnp.zeros((rows, MLA_HEADS, width), w.dtype) for name, width in layout]
    return jnp.concatenate(cols, axis=-1).reshape(rows, -1)


def _prep_weights(w_in, w_uq, w_ukv):
    w_lat = w_in[:, :_LAT_W].astype(BF16)
    w_kr = jnp.pad(w_in[:, _LAT_W:_LAT_W + QK_ROPE].astype(BF16), ((0, 0), (0, LANES - QK_ROPE)))
    w_mix = w_in[:, _LAT_W + QK_ROPE:].astype(BF16)
    w_in_p = (w_lat, w_kr, w_mix)
    w_uq_p = _pad_head_cols(w_uq, (("nope", QK_NOPE), ("rope", QK_ROPE)),
                            (("rope", QK_ROPE), ("zero", HEAD_PAD - QK_ROPE - QK_NOPE), ("nope", QK_NOPE))).astype(BF16)
    kv_in = (("nope", QK_NOPE), ("v", V_HEAD))
    k_cols = _pad_head_cols(w_ukv, kv_in, (("zero", HEAD_PAD - QK_NOPE), ("nope", QK_NOPE)))
    v_cols = _pad_head_cols(w_ukv, kv_in, (("v", V_HEAD), ("zero", HEAD_PAD - V_HEAD)))
    w_ukv_p = jnp.concatenate([k_cols, v_cols], axis=1).astype(BF16)
    return w_in_p, w_uq_p, w_ukv_p


def _rope_tables(n):
    rows = n // GRID_W
    row = jnp.repeat(jnp.arange(rows, dtype=F32), GRID_W)
    col = jnp.tile(jnp.arange(GRID_W, dtype=F32), rows)
    half = QK_ROPE // 2
    inv = ROPE_BASE ** (-jnp.arange(0, half, 2, dtype=F32) / half)
    ang = jnp.concatenate([row[:, None] * inv, col[:, None] * inv], axis=-1)
    cos = jnp.repeat(jnp.cos(ang), 2, axis=-1)
    sin = jnp.repeat(jnp.sin(ang), 2, axis=-1)
    cos = jnp.concatenate([cos, jnp.ones((n, LANES - QK_ROPE), F32)], axis=-1)
    sin = jnp.concatenate([sin, jnp.zeros((n, LANES - QK_ROPE), F32)], axis=-1)
    return cos, sin


def _trunk(x, mods, mod_row, wts, rope_tabs, ctx):
    (inproj_w, merge_w, ffn_w, w_ukv_p) = wts
    bsz, n, _ = x.shape
    tm = 256
    past = 0 if ctx is None else ctx[0].shape[1]
    outs = _inproj(x, mods, mod_row, inproj_w, rope_tabs, ctx is None, tm, n + past)
    q, k, v, qh, kf, lff, kb, lfb, vh, sg, gm, gh = outs[:12]
    if ctx is None:
        ckv, krope = outs[12:]
        s0 = jnp.zeros((bsz, 2, N_PAIRS, LANES, LANES), F32)
        bk = n
    else:
        ckv_c, krope_c, state_c = ctx
        kr_pad = jnp.pad(krope_c, ((0, 0), (0, 0), (0, LANES - QK_ROPE)))
        k, v = _cache_kv(ckv_c, kr_pad, w_ukv_p, k, v)
        s0 = jnp.stack([_state_to_pairs(state_c[:, 0]), _state_to_pairs(state_c[:, 1])], axis=1)
        bk = next(b for b in (768, 512, 256, 128) if (n + past) % b == 0)
    o_mla = _attention(q, k, v, min(n, 1024), bk)
    o_f, o_b, *state = _hgrn(qh, kf, lff, kb, lfb, vh, s0, ctx is None)
    per_token = (x, o_mla, o_f, o_b, sg, gm, gh)
    if ctx is None:
        per_token = tuple(a.reshape(1, bsz * n, a.shape[-1]) for a in per_token)
    rows = per_token[0].shape[1]
    x1 = _merge(*per_token[:1], mods, mod_row, *per_token[1:], merge_w, min(rows, 512))
    y = _ffn(x1, mods, mod_row, ffn_w, min(1024, rows), 1024).reshape(bsz, n, D_MODEL)
    if ctx is None:
        return y, (ckv, krope, state[0])
    return y, None


def kernel(x_prompt, x_sample, cache_ckv, cache_krope, state_hgrn, c, c_ctx, w_ada, b_ada, g_norm_mix, g_norm_ff, w_in, g_q_norm, w_uq, g_kv_norm, w_ukv, g_hg_norm, hg_lb_logits, w_br_mla, w_br_hg, w_out, w_ff1, w_ff2, g_final):
    assert w_in.shape[0] == 1, "single-layer trunk"
    dec_b = c.shape[0]
    cc = jnp.concatenate([c, c_ctx[None, :], jnp.zeros((8 - dec_b - 1, D_MODEL), F32)], axis=0)
    mods = _ada(cc, w_ada[0], b_ada[0][None, :]).reshape(8, 1, 6 * D_MODEL)

    w_in_p, w_uq_p, w_ukv_p = _prep_weights(w_in[0], w_uq[0], w_ukv[0])
    inproj_w = (w_in_p, g_norm_mix[0][None], g_q_norm[0][None], w_uq_p, g_kv_norm[0][None], w_ukv_p, hg_lb_logits)
    head_id = np.arange(HG_W) // HG_DV
    hmean = (head_id[:, None] == head_id[None, :]).astype(np.float32) / HG_DV
    hmean = jnp.asarray(np.concatenate([hmean, hmean], axis=0), BF16)
    merge_w = (g_hg_norm[0][None], hmean, w_br_mla[0].astype(BF16), w_br_hg[0].astype(BF16), w_out[0].astype(BF16))
    ffn_w = (g_norm_ff[0][None], w_ff1[0].astype(BF16), w_ff2[0].astype(BF16), g_final[None])
    wts = (inproj_w, merge_w, ffn_w, w_ukv_p)

    y_prompt, (ckv, krope, state) = _trunk(x_prompt, mods, lambda b: dec_b, wts, None, None)
    rope_tabs = _rope_tables(x_sample.shape[1])
    y_sample, _ = _trunk(x_sample, mods, lambda b: b, wts, rope_tabs,
                         (cache_ckv[:, 0], cache_krope[:, 0], state_hgrn[:, 0]))
    return (y_prompt, y_sample, ckv[:, None], krope[:, None], state[:, None])
```

```python
import functools

import numpy as np
import jax
import jax.numpy as jnp
from jax import lax
from jax.experimental import pallas as pl
from jax.experimental.pallas import tpu as pltpu

D_MODEL = 1024
GRID_W = 64
MLA_HEADS = 8
Q_LORA = 384
KV_LORA = 256
QK_NOPE = 64
QK_ROPE = 32
V_HEAD = 64
MLA_W = MLA_HEADS * V_HEAD
MLA_SCALE = (QK_NOPE + QK_ROPE) ** -0.5
LOG2_E = 1.4426950408889634
HG_HEADS = 8
HG_DK = 64
HG_DV = 64
HG_KW = HG_HEADS * HG_DK
HG_W = HG_HEADS * HG_DV
D_FF = 4 * D_MODEL
ROPE_BASE = 10000.0
EPS = 1e-6

LANES = 128
HEAD_PAD = LANES
QK_PAD_W = MLA_HEADS * HEAD_PAD
N_PAIRS = HG_HEADS // 2
VMEM_LIMIT = 56 * 1024 * 1024

_LAT_W = Q_LORA + KV_LORA
_MIX_SIZES = (HG_KW, HG_KW, HG_KW, HG_W, HG_W, D_MODEL, D_MODEL)
_MIX_OFFS = tuple(int(o) for o in np.cumsum((0,) + _MIX_SIZES))

HG_CHUNK = 128
HG_LEVELS = 7

F32 = jnp.float32
BF16 = jnp.bfloat16


def _cparams(sem):
    return pltpu.CompilerParams(dimension_semantics=sem, vmem_limit_bytes=VMEM_LIMIT)


def _rms(x, g):
    return x * lax.rsqrt(jnp.mean(x * x, axis=-1, keepdims=True) + EPS) * g


def _dot(a, b):
    return jnp.dot(a, b, preferred_element_type=F32)


def _dot_nt(a, b):
    return lax.dot_general(a, b, (((1,), (1,)), ((), ())), preferred_element_type=F32)


def _dot_tn(a, b):
    return lax.dot_general(a, b, (((0,), (0,)), ((), ())), preferred_element_type=F32)


def _split_hi_lo(x):
    hi = x.astype(BF16)
    lo = (x - hi.astype(F32)).astype(BF16)
    return hi, lo


def _ada_kernel(c_ref, w_ref, b_ref, o_ref):
    c = c_ref[...]
    a = c * jax.nn.sigmoid(c)
    a_hi, a_lo = _split_hi_lo(a)
    w_hi, w_lo = _split_hi_lo(w_ref[...])
    o_ref[...] = _dot(a_hi, w_hi) + _dot(a_hi, w_lo) + _dot(a_lo, w_hi) + b_ref[...]


def _ada(cc, w_ada, b_ada):
    rows, tn = cc.shape[0], 1536
    n = w_ada.shape[1]
    return pl.pallas_call(
        _ada_kernel,
        grid=(n // tn,),
        in_specs=[pl.BlockSpec((rows, D_MODEL), lambda j: (0, 0)),
                  pl.BlockSpec((D_MODEL, tn), lambda j: (0, j)),
                  pl.BlockSpec((1, tn), lambda j: (0, j))],
        out_specs=pl.BlockSpec((rows, tn), lambda j: (0, j)),
        out_shape=jax.ShapeDtypeStruct((rows, n), F32),
        compiler_params=_cparams(("arbitrary",)),
        name="ada",
    )(cc, w_ada, b_ada)


def _rope_tile(blk, cos, sin, even):
    rot = jnp.where(even, -pltpu.roll(blk, LANES - 1, 1), pltpu.roll(blk, 1, 1))
    return blk * cos + rot * sin


def _run_staggered(stages):
    pending = stages[0][0]()
    for i, (_, consume) in enumerate(stages):
        current = pending
        if i + 1 < len(stages):
            pending = stages[i + 1][0]()
        consume(current)


def _store_kv(kv, kr, k_ref, v_ref):
    one_lane = ((lax.broadcasted_iota(jnp.int32, (1, QK_PAD_W), 1) & (HEAD_PAD - 1)) == V_HEAD).astype(F32)
    for hh in range(MLA_HEADS):
        sl = slice(hh * HEAD_PAD, (hh + 1) * HEAD_PAD)
        k_ref[0, :, sl] = (kv[:, sl] + kr).astype(BF16)
        vsl = slice(QK_PAD_W + hh * HEAD_PAD, QK_PAD_W + (hh + 1) * HEAD_PAD)
        v_ref[0, sl, :] = (kv[:, vsl] + one_lane[:, sl]).T.astype(BF16)


def _inproj_kernel(*refs, rope, cache_out):
    (x_ref, sh_ref, sc_ref, gmix_ref, wlat_ref, wkr_ref, wmix_ref, gq_ref, wuq_ref, gkv_ref, wukv_ref, lbl_ref) = refs[:12]
    refs = refs[12:]
    if rope:
        cos_ref, sin_ref = refs[:2]
        refs = refs[2:]
    (q_ref, k_ref, v_ref, qh_ref, kf_ref, lff_ref, kb_ref, lfb_ref, vh_ref, sg_ref, gm_ref, gh_ref) = refs[:12]
    refs = refs[12:]

    x = x_ref[0]
    h = _rms(x, gmix_ref[...]) * (1.0 + sc_ref[0]) + sh_ref[0]
    hb = h.astype(BF16)

    def proj(i):
        if i == 0:
            return _dot(hb, wlat_ref[:, :Q_LORA])
        if i == 1:
            return _dot(hb, wlat_ref[:, Q_LORA:])
        if i == 2:
            return _dot(hb, wkr_ref[...])
        return _dot(hb, wmix_ref[:, _MIX_OFFS[i - 3]:_MIX_OFFS[i - 2]])

    if rope:
        cos = cos_ref[...]
        sin = sin_ref[...]
        even = (lax.broadcasted_iota(jnp.int32, cos.shape, 1) & 1) == 0
    kept = {}

    def use_q_lat(y):
        kept["qn"] = _rms(y, gq_ref[...]).astype(BF16)

    def use_kv_lat(y):
        kept["ckv"] = _rms(y, gkv_ref[...])
        if cache_out:
            refs[0][0] = kept["ckv"]

    def use_k_rope(kr):
        if cache_out:
            refs[1][0] = kr[:, :QK_ROPE]
        kept["kr"] = _rope_tile(kr, cos, sin, even) if rope else kr

    def use_q(q):
        q = q * (MLA_SCALE * LOG2_E)
        for hh in range(MLA_HEADS):
            blk = q[:, hh * HEAD_PAD:(hh + 1) * HEAD_PAD]
            if rope:
                blk = _rope_tile(blk, cos, sin, even)
            q_ref[0, hh * HEAD_PAD:(hh + 1) * HEAD_PAD, :] = blk.T.astype(BF16)

    def use_hq(hq):
        qh_ref[0] = hq * jax.nn.sigmoid(hq) * (HG_DK ** -0.5)

    def use_forget(d, k_out, lf_out):
        def use(z):
            l0, l1 = lbl_ref[0, d:d + 1], lbl_ref[1, d:d + 1]
            lmax = jnp.maximum(l0, l1)
            e0, e1 = jnp.exp(l0 - lmax), jnp.exp(l1 - lmax)
            lbd = e0 / (e0 + e1)
            f = lbd + (1.0 - lbd) * jax.nn.sigmoid(z)
            k_out[0] = 1.0 - f
            lf_out[0] = jnp.log(f)
        return use

    def use_hi(y):
        vh_ref[0] = y.astype(vh_ref.dtype)

    def use_hg(hg):
        sg_ref[0] = (hg * jax.nn.sigmoid(hg)).astype(sg_ref.dtype)

    def use_gate(out_ref):
        def use(y):
            out_ref[0] = jax.nn.sigmoid(y).astype(out_ref.dtype)
        return use

    stages = [
        (lambda: proj(0), use_q_lat),
        (lambda: proj(1), use_kv_lat),
        (lambda: proj(2), use_k_rope),
        (lambda: _dot(kept["qn"], wuq_ref[...]), use_q),
        (lambda: _dot(kept["ckv"].astype(BF16), wukv_ref[...]), lambda kv: _store_kv(kv, kept["kr"], k_ref, v_ref)),
        (lambda: proj(3), use_hq),
        (lambda: proj(4), use_forget(0, kf_ref, lff_ref)),
        (lambda: proj(5), use_forget(1, kb_ref, lfb_ref)),
        (lambda: proj(6), use_hi),
        (lambda: proj(7), use_hg),
        (lambda: proj(8), use_gate(gm_ref)),
        (lambda: proj(9), use_gate(gh_ref)),
    ]
    _run_staggered(stages)


def _const_spec(shape):
    return pl.BlockSpec(shape, lambda *_: (0,) * len(shape), pipeline_mode=pl.Buffered(1))


def _inproj(x, mods, mod_row, wts, rope_tabs, cache_out, tm, kv_rows):
    bsz, n, _ = x.shape
    rope = rope_tabs is not None
    (w_lat, w_kr, w_mix), g_mix, g_q, w_uq_p, g_kv, w_ukv_p, lb_logits = wts

    def tok(width):
        return pl.BlockSpec((1, tm, width), lambda b, i: (b, i, 0))

    def mod(col):
        return pl.BlockSpec((1, 1, D_MODEL), lambda b, i: (mod_row(b), 0, col))

    in_specs = [tok(D_MODEL), mod(0), mod(1), _const_spec((1, D_MODEL)), _const_spec(w_lat.shape),
                _const_spec(w_kr.shape), _const_spec(w_mix.shape),
                _const_spec((1, Q_LORA)), _const_spec(w_uq_p.shape), _const_spec((1, KV_LORA)),
                _const_spec(w_ukv_p.shape), _const_spec(lb_logits.shape)]
    args = [x, mods, mods, g_mix, w_lat, w_kr, w_mix, g_q, w_uq_p, g_kv, w_ukv_p, lb_logits]
    if rope:
        in_specs += [pl.BlockSpec((tm, LANES), lambda b, i: (i, 0))] * 2
        args += list(rope_tabs)

    widths = [(QK_PAD_W, BF16)] * 3 + [(HG_KW, F32)] * 5 + [(HG_W, BF16)] * 2 + [(D_MODEL, BF16)] * 2
    if cache_out:
        widths += [(KV_LORA, F32), (QK_ROPE, F32)]
    out_specs = [tok(w) for w, _ in widths]
    out_shape = [jax.ShapeDtypeStruct((bsz, n, w), dt) for w, dt in widths]
    feat_major = pl.BlockSpec((1, QK_PAD_W, tm), lambda b, i: (b, 0, i))
    out_specs[0], out_shape[0] = feat_major, jax.ShapeDtypeStruct((bsz, QK_PAD_W, n), BF16)
    out_shape[1] = jax.ShapeDtypeStruct((bsz, kv_rows, QK_PAD_W), BF16)
    out_specs[2], out_shape[2] = feat_major, jax.ShapeDtypeStruct((bsz, QK_PAD_W, kv_rows), BF16)
    return pl.pallas_call(
        functools.partial(_inproj_kernel, rope=rope, cache_out=cache_out),
        grid=(bsz, n // tm),
        in_specs=in_specs, out_specs=out_specs, out_shape=out_shape,
        compiler_params=_cparams(("parallel", "parallel")),
        name="inproj_rope" if rope else "inproj",
    )(*args)


def _cache_kv_kernel(ckv_ref, kr_ref, wukv_ref, k_all_ref, v_all_ref, k_ref, v_ref):
    del k_all_ref, v_all_ref
    _store_kv(_dot(ckv_ref[0].astype(BF16), wukv_ref[...]), kr_ref[0], k_ref, v_ref)


def _cache_kv(ckv_c, krope_c_pad, w_ukv_p, k_all, v_all):
    bsz, past, _ = ckv_c.shape
    tail = (k_all.shape[1] - past) // past
    assert (tail + 1) * past == k_all.shape[1] == v_all.shape[2]
    k_spec = pl.BlockSpec((1, past, QK_PAD_W), lambda b: (b, tail, 0))
    v_spec = pl.BlockSpec((1, QK_PAD_W, past), lambda b: (b, 0, tail))
    return pl.pallas_call(
        _cache_kv_kernel,
        grid=(bsz,),
        in_specs=[pl.BlockSpec((1, past, KV_LORA), lambda b: (b, 0, 0)),
                  pl.BlockSpec((1, past, LANES), lambda b: (b, 0, 0)),
                  _const_spec(w_ukv_p.shape),
                  pl.BlockSpec(memory_space=pl.ANY), pl.BlockSpec(memory_space=pl.ANY)],
        out_specs=[k_spec, v_spec],
        out_shape=[jax.ShapeDtypeStruct(k_all.shape, BF16), jax.ShapeDtypeStruct(v_all.shape, BF16)],
        input_output_aliases={3: 0, 4: 1},
        compiler_params=_cparams(("parallel",)),
        name="cache_kv",
    )(ckv_c, krope_c_pad, w_ukv_p, k_all, v_all)


def _attn_kernel(q_ref, k_ref, v_ref, o_ref, m_scr, acc_scr):
    j = pl.program_id(2)

    @pl.when(j == 0)
    def _():
        m_scr[...] = jnp.full(m_scr.shape, -jnp.inf, F32)
        acc_scr[...] = jnp.zeros(acc_scr.shape, F32)

    def scores(hh):
        sl = slice(hh * HEAD_PAD, (hh + 1) * HEAD_PAD)
        return _dot(k_ref[0, :, sl], q_ref[0, sl, :])

    ahead = 2
    pending = [scores(hh) for hh in range(ahead)]
    for hh in range(MLA_HEADS):
        sl = slice(hh * HEAD_PAD, (hh + 1) * HEAD_PAD)
        s = pending.pop(0)
        if hh + ahead < MLA_HEADS:
            pending.append(scores(hh + ahead))
        m_prev = m_scr[hh]
        m_cur = jnp.maximum(m_prev, jnp.max(s, axis=0, keepdims=True))
        alpha = jnp.exp2(m_prev - m_cur)
        p = jnp.exp2(s - m_cur[:1])
        m_scr[hh] = m_cur
        acc_scr[hh] = alpha[:1] * acc_scr[hh] + _dot(v_ref[0, sl, :], p.astype(BF16))

    @pl.when(j == pl.num_programs(2) - 1)
    def _():
        def head_out(hh):
            acc = acc_scr[hh]
            return acc[:V_HEAD] / acc[V_HEAD:V_HEAD + 1]

        for pair in range(MLA_HEADS // 2):
            both = jnp.concatenate([head_out(2 * pair), head_out(2 * pair + 1)], axis=0)
            o_ref[0, :, pair * LANES:(pair + 1) * LANES] = both.T.astype(o_ref.dtype)


def _attention(q, k, v, bq, bk):
    bsz, _, nq = q.shape
    nk = k.shape[1]
    return pl.pallas_call(
        _attn_kernel,
        grid=(bsz, nq // bq, nk // bk),
        in_specs=[pl.BlockSpec((1, QK_PAD_W, bq), lambda b, i, j: (b, 0, i)),
                  pl.BlockSpec((1, bk, QK_PAD_W), lambda b, i, j: (b, j, 0)),
                  pl.BlockSpec((1, QK_PAD_W, bk), lambda b, i, j: (b, 0, j))],
        out_specs=pl.BlockSpec((1, bq, MLA_W), lambda b, i, j: (b, i, 0)),
        out_shape=jax.ShapeDtypeStruct((bsz, nq, MLA_W), BF16),
        scratch_shapes=[pltpu.VMEM((MLA_HEADS, 8, bq), F32),
                        pltpu.VMEM((MLA_HEADS, HEAD_PAD, bq), F32)],
        compiler_params=_cparams(("parallel", "parallel", "arbitrary")),
        name="attn",
    )(q, k, v)


HG_TILE_LEVELS = 3


def _hgrn_constants(reverse):
    c = HG_CHUNK
    t = np.arange(c)
    mats = [(t[None, :] <= t[:, None])]
    for lvl in range(1, HG_TILE_LEVELS + 1):
        g, half = 1 << lvl, 1 << (lvl - 1)
        p = t % g
        mid = t - p + half - 1
        isq = p >= half
        u = t[None, :]
        mats.append(np.where(isq[:, None], (u > mid[:, None]) & (u <= t[:, None]),
                             (u > t[:, None]) & (u <= mid[:, None])))
    mstack = np.stack(mats).astype(np.float32)
    if reverse:
        mstack = mstack[:, ::-1, ::-1]
    mstack = mstack.reshape(-1, c)
    return jnp.asarray(np.concatenate([mstack, mstack], axis=1), BF16)


def _hgrn_kernel(qf_ref, qb_ref, kf_ref, lff_ref, kb_ref, lfb_ref, vf_ref, vb_ref, s0_ref, mstack_ref,
                 of_ref, ob_ref, *rest, emit_state):
    st_scr = rest[-1]
    c = HG_CHUNK
    step = pl.program_id(1)

    @pl.when(step == 0)
    def _():
        st_scr[...] = s0_ref[0]

    dirs = ((qf_ref, kf_ref, lff_ref, vf_ref, of_ref, False), (qb_ref, kb_ref, lfb_ref, vb_ref, ob_ref, True))
    exps = []
    for d, (_, _, lf_ref, _, _, _) in enumerate(dirs):
        lf_hi, lf_lo = _split_hi_lo(lf_ref[0])
        exps.append(_dot(mstack_ref[d], jnp.concatenate([lf_hi, lf_lo], axis=0)))

    units = [_hgrn_pair(q_ref, k_ref, v_ref, o_ref, exps[d], st_scr.at[d], pair, reverse)
             for pair in range(N_PAIRS) for d, (q_ref, k_ref, _, v_ref, o_ref, reverse) in enumerate(dirs)]
    ahead = 2
    for unit in units[:ahead]:
        next(unit)
    for i, unit in enumerate(units):
        if i + ahead < len(units):
            next(units[i + ahead])
        next(unit, None)

    if emit_state:
        sfin_ref = rest[0]

        @pl.when(step == pl.num_programs(1) - 1)
        def _():
            for d in range(2):
                for pair in range(N_PAIRS):
                    s_pair = st_scr[d, pair].T
                    sfin_ref[0, d, 2 * pair] = s_pair[:HG_DK, :HG_DV]
                    sfin_ref[0, d, 2 * pair + 1] = pltpu.roll(s_pair, HG_DV, 1)[HG_DK:, :HG_DV]


def _hgrn_pair(q_ref, k_ref, v_ref, o_ref, e, st_scr, pair, reverse):
    c = HG_CHUNK
    last = 0 if reverse else c - 1
    row = lax.broadcasted_iota(jnp.int32, (c, LANES), 0)
    lane_low = lax.broadcasted_iota(jnp.int32, (c, LANES), 1) < HG_DK
    blank_t = jnp.zeros((HG_DK, c), BF16)
    xor = lax.broadcasted_iota(jnp.int32, (c, c), 0) ^ lax.broadcasted_iota(jnp.int32, (c, c), 1)
    diag_block = (lax.broadcasted_iota(jnp.int32, (LANES, LANES), 0) < HG_DV) == (
        lax.broadcasted_iota(jnp.int32, (LANES, LANES), 1) < HG_DK)
    zeros8 = jnp.zeros((8, LANES), F32)

    def q_side_block(b):
        return (b % 2 == 1) != reverse

    sl = slice(pair * LANES, (pair + 1) * LANES)
    q = q_ref[0, :, sl]
    k = k_ref[0, :, sl]
    v = v_ref[0, :, sl].astype(F32)
    cum = e[0:c, sl]
    cum_last = cum[last:last + 1]

    def blank(rows):
        return jnp.concatenate([zeros8] * (rows // 8), axis=0)

    products = []
    for lvl in range(HG_LEVELS, -1, -1):
        half = (1 << lvl) // 2
        if lvl == 0:
            lhs, zk = q.astype(BF16), k
        elif lvl > HG_TILE_LEVELS:
            ys, zs = [], []
            for b in range(0, c // half, 2):
                ref_row = (b + 1) * half if reverse else (b + 1) * half - 1
                r = cum[ref_row:ref_row + 1]
                for bb in (b, b + 1):
                    rs = slice(bb * half, (bb + 1) * half)
                    if q_side_block(bb):
                        ys.append(q[rs] * jnp.exp(cum[rs] - r))
                        zs.append(blank(half))
                    else:
                        zs.append(k[rs] * jnp.exp(r - cum[rs]))
            lhs = jnp.concatenate(ys, axis=0).astype(BF16)
            zk = jnp.concatenate(zs, axis=0)
        else:
            z = jnp.exp(e[lvl * c:(lvl + 1) * c, sl])
            q_row = ((row & half) == 0) if reverse else ((row & half) != 0)
            y = jnp.where(q_row, q, k) * z
            lhs = y.astype(BF16)
            zk = jnp.where(q_row, 0.0, y)
        zt = zk.T.astype(BF16)
        rhs_t = jnp.concatenate([jnp.concatenate([zt[:HG_DK], blank_t], axis=0),
                                 jnp.concatenate([blank_t, zt[HG_DK:]], axis=0)], axis=1)
        products.append(_dot(lhs, rhs_t))
    yield
    a_cat = None
    for lvl, p in zip(range(HG_LEVELS, -1, -1), products):
        half = (1 << lvl) // 2
        if lvl > HG_TILE_LEVELS:
            q_blocks = [bb for bb in range(c // half) if q_side_block(bb)]
            new = []
            for h_idx in range(2):
                p_h = p[:, h_idx * c:(h_idx + 1) * c]
                rows = []
                for bb in range(c // half):
                    rs = slice(bb * half, (bb + 1) * half)
                    if not q_side_block(bb):
                        rows.append(blank(half) if a_cat is None else a_cat[h_idx][rs])
                        continue
                    i = q_blocks.index(bb)
                    p_blk = p_h[i * half:(i + 1) * half]
                    if a_cat is None:
                        rows.append(p_blk)
                    else:
                        rows.append(jnp.where(xor[rs] < (1 << lvl), p_blk, a_cat[h_idx][rs]))
                new.append(jnp.concatenate(rows, axis=0))
            a_cat = tuple(new)
        else:
            same = xor < (1 << lvl)
            a_cat = tuple(jnp.where(same, p[:, h_idx * c:(h_idx + 1) * c], a_cat[h_idx]) for h_idx in range(2))
    a_cat = jnp.concatenate(a_cat, axis=1)
    v_cat = jnp.concatenate([jnp.where(lane_low, v, 0.0), jnp.where(lane_low, 0.0, v)], axis=0).astype(BF16)
    st = st_scr[pair]
    o = _dot(a_cat.astype(BF16), v_cat) + _dot_nt((q * jnp.exp(cum)).astype(BF16), st.astype(BF16))
    o_ref[0, :, sl] = o
    kd = (k * jnp.exp(cum_last - cum)).astype(BF16)
    upd = _dot_tn(v.astype(BF16), kd)
    st_scr[pair] = jnp.exp(cum_last) * st + jnp.where(diag_block, upd, 0.0)


def _hgrn(qh, kf, lff, kb, lfb, vh, s0t, emit_state):
    bsz, n, _ = qh.shape
    c = HG_CHUNK
    nc = n // c
    mstack = jnp.stack([_hgrn_constants(False), _hgrn_constants(True)])
    fwd = pl.BlockSpec((1, c, HG_KW), lambda b, i: (b, i, 0))
    bwd = pl.BlockSpec((1, c, HG_KW), lambda b, i: (b, nc - 1 - i, 0))
    st_spec = pl.BlockSpec((1, 2, N_PAIRS, LANES, LANES), lambda b, i: (b, 0, 0, 0, 0))
    out_specs = [fwd, bwd]
    out_shape = [jax.ShapeDtypeStruct((bsz, n, HG_W), F32)] * 2
    if emit_state:
        out_specs.append(pl.BlockSpec((1, 2, HG_HEADS, HG_DK, HG_DV), lambda b, i: (b, 0, 0, 0, 0)))
        out_shape.append(jax.ShapeDtypeStruct((bsz, 2, HG_HEADS, HG_DK, HG_DV), F32))
    return pl.pallas_call(
        functools.partial(_hgrn_kernel, emit_state=emit_state),
        grid=(bsz, nc),
        in_specs=[fwd, bwd, fwd, fwd, bwd, bwd, fwd, bwd, st_spec, _const_spec(mstack.shape)],
        out_specs=out_specs, out_shape=out_shape,
        scratch_shapes=[pltpu.VMEM((2, N_PAIRS, LANES, LANES), F32)],
        compiler_params=_cparams(("parallel", "arbitrary")),
        name="hgrn",
    )(qh, qh, kf, lff, kb, lfb, vh, vh, s0t, mstack)


def _state_to_pairs(s):
    b = s.shape[0]
    st = jnp.swapaxes(s, -1, -2).reshape(b, N_PAIRS, 2, HG_DV, HG_DK)
    eye = jnp.eye(2, dtype=s.dtype)
    out = jnp.einsum('bpavk,ac->bpavck', st, eye)
    return out.reshape(b, N_PAIRS, 2 * HG_DV, 2 * HG_DK)


def _merge_kernel(x_ref, ga_ref, om_ref, of_ref, ob_ref, sg_ref, gm_ref, gh_ref, ghg_ref, hmean_ref,
                  wbm_ref, wbh_ref, wout_ref, x1_ref):
    tm = x_ref.shape[1]
    halves = [slice(i * tm // 2, (i + 1) * tm // 2) for i in range(2)]
    o = [of_ref[0, rs] + ob_ref[0, rs] for rs in halves]

    def head_mean_sq(i):
        sq_hi, sq_lo = _split_hi_lo(o[i] * o[i])
        return _dot(jnp.concatenate([sq_hi, sq_lo], axis=1), hmean_ref[...])

    ms = [head_mean_sq(i) for i in range(2)]
    mla = [_dot(om_ref[0, rs], wbm_ref[...]) for rs in halves]
    hg = []
    for i, rs in enumerate(halves):
        o_hg = (o[i] * lax.rsqrt(ms[i] + EPS) * ghg_ref[...]) * sg_ref[0, rs]
        hg.append(_dot(o_hg.astype(BF16), wbh_ref[...]))
    out = []
    for i, rs in enumerate(halves):
        merged = gm_ref[0, rs] * mla[i] + gh_ref[0, rs] * hg[i]
        out.append(_dot(merged.astype(BF16), wout_ref[...]))
    for i, rs in enumerate(halves):
        x1_ref[0, rs] = x_ref[0, rs] + ga_ref[0] * out[i]


def _merge(x, mods, mod_row, o_mla, o_f, o_b, sg, gm, gh, wts, tm):
    bsz, n, _ = x.shape
    g_hg, hmean, w_br_mla, w_br_hg, w_out = wts

    def tok(width):
        return pl.BlockSpec((1, tm, width), lambda b, i: (b, i, 0))

    return pl.pallas_call(
        _merge_kernel,
        grid=(bsz, n // tm),
        in_specs=[tok(D_MODEL), pl.BlockSpec((1, 1, D_MODEL), lambda b, i: (mod_row(b), 0, 2)),
                  tok(MLA_W), tok(HG_W), tok(HG_W), tok(HG_W), tok(D_MODEL), tok(D_MODEL),
                  _const_spec((1, HG_W)), _const_spec(hmean.shape), _const_spec(w_br_mla.shape),
                  _const_spec(w_br_hg.shape), _const_spec(w_out.shape)],
        out_specs=tok(D_MODEL),
        out_shape=jax.ShapeDtypeStruct((bsz, n, D_MODEL), F32),
        compiler_params=_cparams(("parallel", "parallel")),
        name="merge",
    )(x, mods, o_mla, o_f, o_b, sg, gm, gh, g_hg, hmean, w_br_mla, w_br_hg, w_out)


FFN_CHUNK = 1024


def _ffn_kernel(x_ref, sh_ref, sc_ref, ga_ref, gff_ref, w1_ref, w2_ref, gfin_ref, y_ref):
    tm = x_ref.shape[1]
    halves = [slice(i * tm // 2, (i + 1) * tm // 2) for i in range(2)]
    h = [(_rms(x_ref[0, rs], gff_ref[...]) * (1.0 + sc_ref[0]) + sh_ref[0]).astype(BF16) for rs in halves]

    up = [(i, c0) for c0 in range(0, D_FF, FFN_CHUNK) for i in range(2)]
    acts = [[], []]

    def use_up(i):
        def use(a):
            a = jnp.maximum(a, 0.0)
            acts[i].append((a * a).astype(BF16))
        return use

    def use_down(rs):
        def use(f):
            y_ref[0, rs] = _rms(x_ref[0, rs] + ga_ref[0] * f, gfin_ref[...])
        return use

    stages = [(functools.partial(lambda i, c0: _dot(h[i], w1_ref[:, c0:c0 + FFN_CHUNK]), i, c0), use_up(i))
              for i, c0 in up]
    stages += [(functools.partial(lambda i: _dot(jnp.concatenate(acts[i], axis=1), w2_ref[...]), i), use_down(rs))
               for i, rs in enumerate(halves)]
    _run_staggered(stages)


def _ffn(x1, mods, mod_row, wts, tm):
    bsz, n, _ = x1.shape
    g_ff, w_ff1, w_ff2, g_final = wts

    def mod(col):
        return pl.BlockSpec((1, 1, D_MODEL), lambda b, i: (mod_row(b), 0, col))

    tok = pl.BlockSpec((1, tm, D_MODEL), lambda b, i: (b, i, 0))
    return pl.pallas_call(
        _ffn_kernel,
        grid=(bsz, n // tm),
        in_specs=[tok, mod(3), mod(4), mod(5), _const_spec((1, D_MODEL)),
                  _const_spec(w_ff1.shape), _const_spec(w_ff2.shape), _const_spec((1, D_MODEL))],
        out_specs=tok,
        out_shape=jax.ShapeDtypeStruct((bsz, n, D_MODEL), F32),
        compiler_params=_cparams(("parallel", "parallel")),
        name="ffn",
    )(x1, mods, mods, mods, g_ff, w_ff1, w_ff2, g_final)


def _pad_head_cols(w, widths_in, layout):
    rows = w.shape[0]
    w = w.reshape(rows, MLA_HEADS, sum(width for _, width in widths_in))
    pieces, off = {}, 0
    for name, width in widths_in:
        pieces[name] = w[:, :, off:off + width]
        off += width
    cols = [pieces[name] if name in pieces else jnp.zeros((rows, MLA_HEADS, width), w.dtype) for name, width in layout]
    return jnp.concatenate(cols, axis=-1).reshape(rows, -1)


def _prep_weights(w_in, w_uq, w_ukv):
    w_lat = w_in[:, :_LAT_W].astype(BF16)
    w_kr = jnp.pad(w_in[:, _LAT_W:_LAT_W + QK_ROPE].astype(BF16), ((0, 0), (0, LANES - QK_ROPE)))
    w_mix = w_in[:, _LAT_W + QK_ROPE:].astype(BF16)
    w_in_p = (w_lat, w_kr, w_mix)
    w_uq_p = _pad_head_cols(w_uq, (("nope", QK_NOPE), ("rope", QK_ROPE)),
                            (("rope", QK_ROPE), ("zero", HEAD_PAD - QK_ROPE - QK_NOPE), ("nope", QK_NOPE))).astype(BF16)
    kv_in = (("nope", QK_NOPE), ("v", V_HEAD))
    k_cols = _pad_head_cols(w_ukv, kv_in, (("zero", HEAD_PAD - QK_NOPE), ("nope", QK_NOPE)))
    v_cols = _pad_head_cols(w_ukv, kv_in, (("v", V_HEAD), ("zero", HEAD_PAD - V_HEAD)))
    w_ukv_p = jnp.concatenate([k_cols, v_cols], axis=1).astype(BF16)
    return w_in_p, w_uq_p, w_ukv_p


def _rope_tables(n):
    rows = n // GRID_W
    row = jnp.repeat(jnp.arange(rows, dtype=F32), GRID_W)
    col = jnp.tile(jnp.arange(GRID_W, dtype=F32), rows)
    half = QK_ROPE // 2
    inv = ROPE_BASE ** (-jnp.arange(0, half, 2, dtype=F32) / half)
    ang = jnp.concatenate([row[:, None] * inv, col[:, None] * inv], axis=-1)
    cos = jnp.repeat(jnp.cos(ang), 2, axis=-1)
    sin = jnp.repeat(jnp.sin(ang), 2, axis=-1)
    cos = jnp.concatenate([cos, jnp.ones((n, LANES - QK_ROPE), F32)], axis=-1)
    sin = jnp.concatenate([sin, jnp.zeros((n, LANES - QK_ROPE), F32)], axis=-1)
    return cos, sin


def _trunk(x, mods, mod_row, wts, rope_tabs, ctx):
    (inproj_w, merge_w, ffn_w, w_ukv_p) = wts
    bsz, n, _ = x.shape
    tm = 256
    past = 0 if ctx is None else ctx[0].shape[1]
    outs = _inproj(x, mods, mod_row, inproj_w, rope_tabs, ctx is None, tm, n + past)
    q, k, v, qh, kf, lff, kb, lfb, vh, sg, gm, gh = outs[:12]
    if ctx is None:
        ckv, krope = outs[12:]
        s0 = jnp.zeros((bsz, 2, N_PAIRS, LANES, LANES), F32)
        bk = n
    else:
        ckv_c, krope_c, state_c = ctx
        kr_pad = jnp.pad(krope_c, ((0, 0), (0, 0), (0, LANES - QK_ROPE)))
        k, v = _cache_kv(ckv_c, kr_pad, w_ukv_p, k, v)
        s0 = jnp.stack([_state_to_pairs(state_c[:, 0]), _state_to_pairs(state_c[:, 1])], axis=1)
        bk = next(b for b in (768, 512, 256, 128) if (n + past) % b == 0)
    o_mla = _attention(q, k, v, min(n, 1024), bk)
    o_f, o_b, *state = _hgrn(qh, kf, lff, kb, lfb, vh, s0, ctx is None)
    per_token = (x, o_mla, o_f, o_b, sg, gm, gh)
    if ctx is None:
        per_token = tuple(a.reshape(1, bsz * n, a.shape[-1]) for a in per_token)
    rows = per_token[0].shape[1]
    x1 = _merge(*per_token[:1], mods, mod_row, *per_token[1:], merge_w, min(rows, 512))
    y = _ffn(x1, mods, mod_row, ffn_w, min(512, rows)).reshape(bsz, n, D_MODEL)
    if ctx is None:
        return y, (ckv, krope, state[0])
    return y, None


def kernel(x_prompt, x_sample, cache_ckv, cache_krope, state_hgrn, c, c_ctx, w_ada, b_ada, g_norm_mix, g_norm_ff, w_in, g_q_norm, w_uq, g_kv_norm, w_ukv, g_hg_norm, hg_lb_logits, w_br_mla, w_br_hg, w_out, w_ff1, w_ff2, g_final):
    assert w_in.shape[0] == 1, "single-layer trunk"
    dec_b = c.shape[0]
    cc = jnp.concatenate([c, c_ctx[None, :], jnp.zeros((8 - dec_b - 1, D_MODEL), F32)], axis=0)
    mods = _ada(cc, w_ada[0], b_ada[0][None, :]).reshape(8, 1, 6 * D_MODEL)

    w_in_p, w_uq_p, w_ukv_p = _prep_weights(w_in[0], w_uq[0], w_ukv[0])
    inproj_w = (w_in_p, g_norm_mix[0][None], g_q_norm[0][None], w_uq_p, g_kv_norm[0][None], w_ukv_p, hg_lb_logits)
    head_id = np.arange(HG_W) // HG_DV
    hmean = (head_id[:, None] == head_id[None, :]).astype(np.float32) / HG_DV
    hmean = jnp.asarray(np.concatenate([hmean, hmean], axis=0), BF16)
    merge_w = (g_hg_norm[0][None], hmean, w_br_mla[0].astype(BF16), w_br_hg[0].astype(BF16), w_out[0].astype(BF16))
    ffn_w = (g_norm_ff[0][None], w_ff1[0].astype(BF16), w_ff2[0].astype(BF16), g_final[None])
    wts = (inproj_w, merge_w, ffn_w, w_ukv_p)

    y_prompt, (ckv, krope, state) = _trunk(x_prompt, mods, lambda b: dec_b, wts, None, None)
    rope_tabs = _rope_tables(x_sample.shape[1])
    y_sample, _ = _trunk(x_sample, mods, lambda b: b, wts, rope_tabs,
                         (cache_ckv[:, 0], cache_krope[:, 0], state_hgrn[:, 0]))
    return (y_prompt, y_sample, ckv[:, None], krope[:, None], state[:, None])
```

```python
import functools

import numpy as np
import jax
import jax.numpy as jnp
from jax import lax
from jax.experimental import pallas as pl
from jax.experimental.pallas import tpu as pltpu

D_MODEL = 1024
GRID_W = 64
MLA_HEADS = 8
Q_LORA = 384
KV_LORA = 256
QK_NOPE = 64
QK_ROPE = 32
V_HEAD = 64
MLA_W = MLA_HEADS * V_HEAD
MLA_SCALE = (QK_NOPE + QK_ROPE) ** -0.5
LOG2_E = 1.4426950408889634
HG_HEADS = 8
HG_DK = 64
HG_DV = 64
HG_KW = HG_HEADS * HG_DK
HG_W = HG_HEADS * HG_DV
D_FF = 4 * D_MODEL
ROPE_BASE = 10000.0
EPS = 1e-6

LANES = 128
HEAD_PAD = LANES
QK_PAD_W = MLA_HEADS * HEAD_PAD
N_PAIRS = HG_HEADS // 2
VMEM_LIMIT = 56 * 1024 * 1024

_LAT_W = Q_LORA + KV_LORA
_MIX_SIZES = (HG_KW, HG_KW, HG_KW, HG_W, HG_W, D_MODEL, D_MODEL)
_MIX_OFFS = tuple(int(o) for o in np.cumsum((0,) + _MIX_SIZES))

HG_CHUNK = 128
HG_LEVELS = 7

F32 = jnp.float32
BF16 = jnp.bfloat16


def _cparams(sem):
    return pltpu.CompilerParams(dimension_semantics=sem, vmem_limit_bytes=VMEM_LIMIT)


def _rms(x, g):
    return x * lax.rsqrt(jnp.mean(x * x, axis=-1, keepdims=True) + EPS) * g


def _dot(a, b):
    return jnp.dot(a, b, preferred_element_type=F32)


def _dot_nt(a, b):
    return lax.dot_general(a, b, (((1,), (1,)), ((), ())), preferred_element_type=F32)


def _dot_tn(a, b):
    return lax.dot_general(a, b, (((0,), (0,)), ((), ())), preferred_element_type=F32)


def _split_hi_lo(x):
    hi = x.astype(BF16)
    lo = (x - hi.astype(F32)).astype(BF16)
    return hi, lo


def _ada_kernel(c_ref, w_ref, b_ref, o_ref):
    c = c_ref[...]
    a = c * jax.nn.sigmoid(c)
    a_hi, a_lo = _split_hi_lo(a)
    w_hi, w_lo = _split_hi_lo(w_ref[...])
    o_ref[...] = _dot(a_hi, w_hi) + _dot(a_hi, w_lo) + _dot(a_lo, w_hi) + b_ref[...]


def _ada(cc, w_ada, b_ada):
    rows, tn = cc.shape[0], 1536
    n = w_ada.shape[1]
    return pl.pallas_call(
        _ada_kernel,
        grid=(n // tn,),
        in_specs=[pl.BlockSpec((rows, D_MODEL), lambda j: (0, 0)),
                  pl.BlockSpec((D_MODEL, tn), lambda j: (0, j)),
                  pl.BlockSpec((1, tn), lambda j: (0, j))],
        out_specs=pl.BlockSpec((rows, tn), lambda j: (0, j)),
        out_shape=jax.ShapeDtypeStruct((rows, n), F32),
        compiler_params=_cparams(("arbitrary",)),
        name="ada",
    )(cc, w_ada, b_ada)


def _rope_tile(blk, cos, sin, even):
    rot = jnp.where(even, -pltpu.roll(blk, LANES - 1, 1), pltpu.roll(blk, 1, 1))
    return blk * cos + rot * sin


def _run_staggered(stages):
    pending = stages[0][0]()
    for i, (_, consume) in enumerate(stages):
        current = pending
        if i + 1 < len(stages):
            pending = stages[i + 1][0]()
        consume(current)


def _store_kv(kv, kr, k_ref, v_ref):
    one_lane = ((lax.broadcasted_iota(jnp.int32, (1, QK_PAD_W), 1) & (HEAD_PAD - 1)) == V_HEAD).astype(F32)
    for hh in range(MLA_HEADS):
        sl = slice(hh * HEAD_PAD, (hh + 1) * HEAD_PAD)
        k_ref[0, :, sl] = (kv[:, sl] + kr).astype(BF16)
        vsl = slice(QK_PAD_W + hh * HEAD_PAD, QK_PAD_W + (hh + 1) * HEAD_PAD)
        v_ref[0, sl, :] = (kv[:, vsl] + one_lane[:, sl]).T.astype(BF16)


def _inproj_kernel(*refs, rope, cache_out, n_blocks, with_cache):
    if with_cache:
        step = pl.program_id(1)
        ckvc_ref, krc_ref = refs[12 + 2 * rope:14 + 2 * rope]
        refs = refs[:12 + 2 * rope] + refs[14 + 2 * rope:]
        k_ref, v_ref, wukv_ref = refs[12 + 2 * rope + 1], refs[12 + 2 * rope + 2], refs[10]

        @pl.when(step >= n_blocks)
        def _():
            _store_kv(_dot(ckvc_ref[0].astype(BF16), wukv_ref[...]), krc_ref[0], k_ref, v_ref)

        pl.when(step < n_blocks)(functools.partial(_inproj_tile, refs, rope, cache_out))
    else:
        _inproj_tile(refs, rope, cache_out)


def _inproj_tile(refs, rope, cache_out):
    (x_ref, sh_ref, sc_ref, gmix_ref, wlat_ref, wkr_ref, wmix_ref, gq_ref, wuq_ref, gkv_ref, wukv_ref, lbl_ref) = refs[:12]
    refs = refs[12:]
    if rope:
        cos_ref, sin_ref = refs[:2]
        refs = refs[2:]
    (q_ref, k_ref, v_ref, qh_ref, kf_ref, lff_ref, kb_ref, lfb_ref, vh_ref, sg_ref, gm_ref, gh_ref) = refs[:12]
    refs = refs[12:]

    x = x_ref[0]
    h = _rms(x, gmix_ref[...]) * (1.0 + sc_ref[0]) + sh_ref[0]
    hb = h.astype(BF16)

    def proj(i):
        if i == 0:
            return _dot(hb, wlat_ref[:, :Q_LORA])
        if i == 1:
            return _dot(hb, wlat_ref[:, Q_LORA:])
        if i == 2:
            return _dot(hb, wkr_ref[...])
        return _dot(hb, wmix_ref[:, _MIX_OFFS[i - 3]:_MIX_OFFS[i - 2]])

    if rope:
        cos = cos_ref[...]
        sin = sin_ref[...]
        even = (lax.broadcasted_iota(jnp.int32, cos.shape, 1) & 1) == 0
    kept = {}

    def use_q_lat(y):
        kept["qn"] = _rms(y, gq_ref[...]).astype(BF16)

    def use_kv_lat(y):
        kept["ckv"] = _rms(y, gkv_ref[...])
        if cache_out:
            refs[0][0] = kept["ckv"]

    def use_k_rope(kr):
        if cache_out:
            refs[1][0] = kr[:, :QK_ROPE]
        kept["kr"] = _rope_tile(kr, cos, sin, even) if rope else kr

    def use_q(q):
        q = q * (MLA_SCALE * LOG2_E)
        for hh in range(MLA_HEADS):
            blk = q[:, hh * HEAD_PAD:(hh + 1) * HEAD_PAD]
            if rope:
                blk = _rope_tile(blk, cos, sin, even)
            q_ref[0, hh * HEAD_PAD:(hh + 1) * HEAD_PAD, :] = blk.T.astype(BF16)

    def use_hq(hq):
        qh_ref[0] = hq * jax.nn.sigmoid(hq) * (HG_DK ** -0.5)

    def use_forget(d, k_out, lf_out):
        def use(z):
            l0, l1 = lbl_ref[0, d:d + 1], lbl_ref[1, d:d + 1]
            lmax = jnp.maximum(l0, l1)
            e0, e1 = jnp.exp(l0 - lmax), jnp.exp(l1 - lmax)
            lbd = e0 / (e0 + e1)
            f = lbd + (1.0 - lbd) * jax.nn.sigmoid(z)
            k_out[0] = 1.0 - f
            lf_out[0] = jnp.log2(f)
        return use

    def use_hi(y):
        vh_ref[0] = y.astype(vh_ref.dtype)

    def use_hg(hg):
        sg_ref[0] = (hg * jax.nn.sigmoid(hg)).astype(sg_ref.dtype)

    def use_gate(out_ref):
        def use(y):
            out_ref[0] = jax.nn.sigmoid(y).astype(out_ref.dtype)
        return use

    stages = [
        (lambda: proj(0), use_q_lat),
        (lambda: proj(1), use_kv_lat),
        (lambda: proj(2), use_k_rope),
        (lambda: _dot(kept["qn"], wuq_ref[...]), use_q),
        (lambda: _dot(kept["ckv"].astype(BF16), wukv_ref[...]), lambda kv: _store_kv(kv, kept["kr"], k_ref, v_ref)),
        (lambda: proj(3), use_hq),
        (lambda: proj(4), use_forget(0, kf_ref, lff_ref)),
        (lambda: proj(5), use_forget(1, kb_ref, lfb_ref)),
        (lambda: proj(6), use_hi),
        (lambda: proj(7), use_hg),
        (lambda: proj(8), use_gate(gm_ref)),
        (lambda: proj(9), use_gate(gh_ref)),
    ]
    _run_staggered(stages)


def _const_spec(shape):
    return pl.BlockSpec(shape, lambda *_: (0,) * len(shape), pipeline_mode=pl.Buffered(1))


def _inproj(x, mods, mod_row, wts, rope_tabs, cache_out, tm, cache):
    bsz, n, _ = x.shape
    rope = rope_tabs is not None
    (w_lat, w_kr, w_mix), g_mix, g_q, w_uq_p, g_kv, w_ukv_p, lb_logits = wts
    n_blocks = n // tm
    past = 0 if cache is None else cache[0].shape[1]
    assert n_blocks * tm == n and past % tm == 0
    kv_rows = n + past

    def own(i):
        return i if cache is None else jnp.minimum(i, n_blocks - 1)

    def tok(width):
        return pl.BlockSpec((1, tm, width), lambda b, i: (b, own(i), 0))

    def mod(col):
        return pl.BlockSpec((1, 1, D_MODEL), lambda b, i: (mod_row(b), 0, col))

    in_specs = [tok(D_MODEL), mod(0), mod(1), _const_spec((1, D_MODEL)), _const_spec(w_lat.shape),
                _const_spec(w_kr.shape), _const_spec(w_mix.shape),
                _const_spec((1, Q_LORA)), _const_spec(w_uq_p.shape), _const_spec((1, KV_LORA)),
                _const_spec(w_ukv_p.shape), _const_spec(lb_logits.shape)]
    args = [x, mods, mods, g_mix, w_lat, w_kr, w_mix, g_q, w_uq_p, g_kv, w_ukv_p, lb_logits]
    if rope:
        in_specs += [pl.BlockSpec((tm, LANES), lambda b, i: (own(i), 0))] * 2
        args += list(rope_tabs)
    if cache is not None:
        in_specs += [pl.BlockSpec((1, tm, w), lambda b, i: (b, jnp.maximum(i - n_blocks, 0), 0))
                     for w in (KV_LORA, LANES)]
        args += list(cache)

    widths = [(QK_PAD_W, BF16)] * 3 + [(HG_KW, F32)] * 5 + [(HG_W, BF16)] * 2 + [(D_MODEL, BF16)] * 2
    if cache_out:
        widths += [(KV_LORA, F32), (QK_ROPE, F32)]
    out_specs = [tok(w) for w, _ in widths]
    out_shape = [jax.ShapeDtypeStruct((bsz, n, w), dt) for w, dt in widths]
    out_specs[0] = pl.BlockSpec((1, QK_PAD_W, tm), lambda b, i: (b, 0, own(i)))
    out_shape[0] = jax.ShapeDtypeStruct((bsz, QK_PAD_W, n), BF16)
    out_specs[1] = pl.BlockSpec((1, tm, QK_PAD_W), lambda b, i: (b, i, 0))
    out_shape[1] = jax.ShapeDtypeStruct((bsz, kv_rows, QK_PAD_W), BF16)
    out_specs[2] = pl.BlockSpec((1, QK_PAD_W, tm), lambda b, i: (b, 0, i))
    out_shape[2] = jax.ShapeDtypeStruct((bsz, QK_PAD_W, kv_rows), BF16)
    return pl.pallas_call(
        functools.partial(_inproj_kernel, rope=rope, cache_out=cache_out, n_blocks=n_blocks,
                          with_cache=cache is not None),
        grid=(bsz, kv_rows // tm),
        in_specs=in_specs, out_specs=out_specs, out_shape=out_shape,
        compiler_params=_cparams(("parallel", "arbitrary")),
        name="inproj_rope" if rope else "inproj",
    )(*args)


def _attn_kernel(q_ref, k_ref, v_ref, o_ref, m_scr, acc_scr):
    j = pl.program_id(2)

    @pl.when(j == 0)
    def _():
        m_scr[...] = jnp.full(m_scr.shape, -jnp.inf, F32)
        acc_scr[...] = jnp.zeros(acc_scr.shape, F32)

    def scores(hh):
        sl = slice(hh * HEAD_PAD, (hh + 1) * HEAD_PAD)
        return _dot(k_ref[0, :, sl], q_ref[0, sl, :])

    ahead = 2
    pending = [scores(hh) for hh in range(ahead)]
    for hh in range(MLA_HEADS):
        sl = slice(hh * HEAD_PAD, (hh + 1) * HEAD_PAD)
        s = pending.pop(0)
        if hh + ahead < MLA_HEADS:
            pending.append(scores(hh + ahead))
        m_prev = m_scr[hh]
        m_cur = jnp.maximum(m_prev, jnp.max(s, axis=0, keepdims=True))
        alpha = jnp.exp2(m_prev - m_cur)
        p = jnp.exp2(s - m_cur[:1])
        m_scr[hh] = m_cur
        acc_scr[hh] = alpha[:1] * acc_scr[hh] + _dot(v_ref[0, sl, :], p.astype(BF16))

    @pl.when(j == pl.num_programs(2) - 1)
    def _():
        def head_out(hh):
            acc = acc_scr[hh]
            return acc[:V_HEAD] / acc[V_HEAD:V_HEAD + 1]

        for pair in range(MLA_HEADS // 2):
            both = jnp.concatenate([head_out(2 * pair), head_out(2 * pair + 1)], axis=0)
            o_ref[0, :, pair * LANES:(pair + 1) * LANES] = both.T.astype(o_ref.dtype)


def _attention(q, k, v, bq, bk):
    bsz, _, nq = q.shape
    nk = k.shape[1]
    return pl.pallas_call(
        _attn_kernel,
        grid=(bsz, nq // bq, nk // bk),
        in_specs=[pl.BlockSpec((1, QK_PAD_W, bq), lambda b, i, j: (b, 0, i)),
                  pl.BlockSpec((1, bk, QK_PAD_W), lambda b, i, j: (b, j, 0)),
                  pl.BlockSpec((1, QK_PAD_W, bk), lambda b, i, j: (b, 0, j))],
        out_specs=pl.BlockSpec((1, bq, MLA_W), lambda b, i, j: (b, i, 0)),
        out_shape=jax.ShapeDtypeStruct((bsz, nq, MLA_W), BF16),
        scratch_shapes=[pltpu.VMEM((MLA_HEADS, 8, bq), F32),
                        pltpu.VMEM((MLA_HEADS, HEAD_PAD, bq), F32)],
        compiler_params=_cparams(("parallel", "parallel", "arbitrary")),
        name="attn",
    )(q, k, v)


HG_TILE_LEVELS = 3


def _hgrn_constants(reverse):
    c = HG_CHUNK
    t = np.arange(c)
    mats = [(t[None, :] <= t[:, None])]
    for lvl in range(1, HG_TILE_LEVELS + 1):
        g, half = 1 << lvl, 1 << (lvl - 1)
        p = t % g
        mid = t - p + half - 1
        isq = p >= half
        u = t[None, :]
        mats.append(np.where(isq[:, None], (u > mid[:, None]) & (u <= t[:, None]),
                             (u > t[:, None]) & (u <= mid[:, None])))
    mstack = np.stack(mats).astype(np.float32)
    if reverse:
        mstack = mstack[:, ::-1, ::-1]
    mstack = mstack.reshape(-1, c)
    return jnp.asarray(np.concatenate([mstack, mstack], axis=1), BF16)


def _hgrn_kernel(qf_ref, qb_ref, kf_ref, lff_ref, kb_ref, lfb_ref, vf_ref, vb_ref, mstack_ref, *rest,
                 has_s0, emit_state):
    if has_s0:
        s0_ref, rest = rest[0], rest[1:]
    of_ref, ob_ref = rest[:2]
    rest = rest[2:]
    st_scr = rest[-1]
    c = HG_CHUNK
    step = pl.program_id(1)

    @pl.when(step == 0)
    def _():
        st_scr[...] = s0_ref[0] if has_s0 else jnp.zeros(st_scr.shape, F32)

    dirs = ((qf_ref, kf_ref, lff_ref, vf_ref, of_ref, False), (qb_ref, kb_ref, lfb_ref, vb_ref, ob_ref, True))
    exps = []
    for d, (_, _, lf_ref, _, _, _) in enumerate(dirs):
        lf_hi, lf_lo = _split_hi_lo(lf_ref[0])
        exps.append(_dot(mstack_ref[d], jnp.concatenate([lf_hi, lf_lo], axis=0)))

    units = [_hgrn_pair(q_ref, k_ref, v_ref, o_ref, exps[d], st_scr.at[d], pair, reverse)
             for pair in range(N_PAIRS) for d, (q_ref, k_ref, _, v_ref, o_ref, reverse) in enumerate(dirs)]
    ahead = 2
    for unit in units[:ahead]:
        next(unit)
    for i, unit in enumerate(units):
        if i + ahead < len(units):
            next(units[i + ahead])
        next(unit, None)

    if emit_state:
        sfin_ref = rest[0]

        @pl.when(step == pl.num_programs(1) - 1)
        def _():
            for d in range(2):
                for pair in range(N_PAIRS):
                    s_pair = st_scr[d, pair].T
                    sfin_ref[0, d, 2 * pair] = s_pair[:HG_DK, :HG_DV]
                    sfin_ref[0, d, 2 * pair + 1] = pltpu.roll(s_pair, HG_DV, 1)[HG_DK:, :HG_DV]


def _hgrn_pair(q_ref, k_ref, v_ref, o_ref, e, st_scr, pair, reverse):
    c = HG_CHUNK
    last = 0 if reverse else c - 1
    row = lax.broadcasted_iota(jnp.int32, (c, LANES), 0)
    lane_low = lax.broadcasted_iota(jnp.int32, (c, LANES), 1) < HG_DK
    low_b = jnp.where(lane_low, 1.0, 0.0).astype(BF16)
    high_b = jnp.where(lane_low, 0.0, 1.0).astype(BF16)
    blank_t = jnp.zeros((HG_DK, c), BF16)
    xor = lax.broadcasted_iota(jnp.int32, (c, c), 0) ^ lax.broadcasted_iota(jnp.int32, (c, c), 1)
    diag_block = (lax.broadcasted_iota(jnp.int32, (LANES, LANES), 0) < HG_DV) == (
        lax.broadcasted_iota(jnp.int32, (LANES, LANES), 1) < HG_DK)
    zeros8 = jnp.zeros((8, LANES), F32)

    def q_side_block(b):
        return (b % 2 == 1) != reverse

    sl = slice(pair * LANES, (pair + 1) * LANES)
    q = q_ref[0, :, sl]
    k = k_ref[0, :, sl]
    v = v_ref[0, :, sl]
    cum = e[0:c, sl]
    cum_last = cum[last:last + 1]

    def blank(rows):
        return jnp.concatenate([zeros8] * (rows // 8), axis=0)

    products = []
    for lvl in range(HG_LEVELS, -1, -1):
        half = (1 << lvl) // 2
        if lvl == 0:
            lhs, zk = q.astype(BF16), k
        elif lvl > HG_TILE_LEVELS:
            ys, zs = [], []
            for b in range(0, c // half, 2):
                ref_row = (b + 1) * half if reverse else (b + 1) * half - 1
                r = cum[ref_row:ref_row + 1]
                for bb in (b, b + 1):
                    rs = slice(bb * half, (bb + 1) * half)
                    if q_side_block(bb):
                        ys.append(q[rs] * jnp.exp2(cum[rs] - r))
                        zs.append(blank(half))
                    else:
                        zs.append(k[rs] * jnp.exp2(r - cum[rs]))
            lhs = jnp.concatenate(ys, axis=0).astype(BF16)
            zk = jnp.concatenate(zs, axis=0)
        else:
            z = jnp.exp2(e[lvl * c:(lvl + 1) * c, sl])
            q_row = ((row & half) == 0) if reverse else ((row & half) != 0)
            y = jnp.where(q_row, q, k) * z
            lhs = y.astype(BF16)
            zk = jnp.where(q_row, 0.0, y)
        zt = zk.T.astype(BF16)
        rhs_t = jnp.concatenate([jnp.concatenate([zt[:HG_DK], blank_t], axis=0),
                                 jnp.concatenate([blank_t, zt[HG_DK:]], axis=0)], axis=1)
        products.append(_dot(lhs, rhs_t))
    yield
    a_cat = None
    for lvl, p in zip(range(HG_LEVELS, -1, -1), products):
        half = (1 << lvl) // 2
        if lvl > HG_TILE_LEVELS:
            q_blocks = [bb for bb in range(c // half) if q_side_block(bb)]
            new = []
            for h_idx in range(2):
                p_h = p[:, h_idx * c:(h_idx + 1) * c]
                rows = []
                for bb in range(c // half):
                    rs = slice(bb * half, (bb + 1) * half)
                    if not q_side_block(bb):
                        rows.append(blank(half) if a_cat is None else a_cat[h_idx][rs])
                        continue
                    i = q_blocks.index(bb)
                    p_blk = p_h[i * half:(i + 1) * half]
                    if a_cat is None:
                        rows.append(p_blk)
                    else:
                        rows.append(jnp.where(xor[rs] < (1 << lvl), p_blk, a_cat[h_idx][rs]))
                new.append(jnp.concatenate(rows, axis=0))
            a_cat = tuple(new)
        else:
            same = xor < (1 << lvl)
            a_cat = tuple(jnp.where(same, p[:, h_idx * c:(h_idx + 1) * c], a_cat[h_idx]) for h_idx in range(2))
    a_cat = jnp.concatenate(a_cat, axis=1)
    v_cat = jnp.concatenate([v * low_b, v * high_b], axis=0)
    st = st_scr[pair]
    o = _dot(a_cat.astype(BF16), v_cat) + _dot_nt((q * jnp.exp2(cum)).astype(BF16), st.astype(BF16))
    o_ref[0, :, sl] = o
    kd = (k * jnp.exp2(cum_last - cum)).astype(BF16)
    upd = _dot_tn(v, kd)
    st_scr[pair] = jnp.exp2(cum_last) * st + jnp.where(diag_block, upd, 0.0)


def _hgrn(qh, kf, lff, kb, lfb, vh, s0t, emit_state):
    bsz, n, _ = qh.shape
    c = HG_CHUNK
    nc = n // c
    mstack = jnp.stack([_hgrn_constants(False), _hgrn_constants(True)])
    fwd = pl.BlockSpec((1, c, HG_KW), lambda b, i: (b, i, 0))
    bwd = pl.BlockSpec((1, c, HG_KW), lambda b, i: (b, nc - 1 - i, 0))
    in_specs = [fwd, bwd, fwd, fwd, bwd, bwd, fwd, bwd, _const_spec(mstack.shape)]
    args = [qh, qh, kf, lff, kb, lfb, vh, vh, mstack]
    if s0t is not None:
        in_specs.append(pl.BlockSpec((1, 2, N_PAIRS, LANES, LANES), lambda b, i: (b, 0, 0, 0, 0)))
        args.append(s0t)
    out_specs = [fwd, bwd]
    out_shape = [jax.ShapeDtypeStruct((bsz, n, HG_W), F32)] * 2
    if emit_state:
        out_specs.append(pl.BlockSpec((1, 2, HG_HEADS, HG_DK, HG_DV), lambda b, i: (b, 0, 0, 0, 0)))
        out_shape.append(jax.ShapeDtypeStruct((bsz, 2, HG_HEADS, HG_DK, HG_DV), F32))
    return pl.pallas_call(
        functools.partial(_hgrn_kernel, has_s0=s0t is not None, emit_state=emit_state),
        grid=(bsz, nc),
        in_specs=in_specs, out_specs=out_specs, out_shape=out_shape,
        scratch_shapes=[pltpu.VMEM((2, N_PAIRS, LANES, LANES), F32)],
        compiler_params=_cparams(("parallel", "arbitrary")),
        name="hgrn",
    )(*args)


def _state_to_pairs(s):
    b = s.shape[0]
    st = jnp.swapaxes(s, -1, -2).reshape(b, N_PAIRS, 2, HG_DV, HG_DK)
    eye = jnp.eye(2, dtype=s.dtype)
    out = jnp.einsum('bpavk,ac->bpavck', st, eye)
    return out.reshape(b, N_PAIRS, 2 * HG_DV, 2 * HG_DK)


def _merge_kernel(x_ref, ga_ref, om_ref, of_ref, ob_ref, sg_ref, gm_ref, gh_ref, ghg_ref, hmean_ref,
                  wbm_ref, wbh_ref, wout_ref, x1_ref):
    tm = x_ref.shape[1]
    halves = [slice(i * tm // 2, (i + 1) * tm // 2) for i in range(2)]
    o = [of_ref[0, rs] + ob_ref[0, rs] for rs in halves]

    def head_mean_sq(i):
        sq_hi, sq_lo = _split_hi_lo(o[i] * o[i])
        return _dot(jnp.concatenate([sq_hi, sq_lo], axis=1), hmean_ref[...])

    ms = [head_mean_sq(i) for i in range(2)]
    mla = [_dot(om_ref[0, rs], wbm_ref[...]) for rs in halves]
    hg = []
    for i, rs in enumerate(halves):
        o_hg = (o[i] * lax.rsqrt(ms[i] + EPS) * ghg_ref[...]) * sg_ref[0, rs]
        hg.append(_dot(o_hg.astype(BF16), wbh_ref[...]))
    out = []
    for i, rs in enumerate(halves):
        merged = gm_ref[0, rs] * mla[i] + gh_ref[0, rs] * hg[i]
        out.append(_dot(merged.astype(BF16), wout_ref[...]))
    for i, rs in enumerate(halves):
        x1_ref[0, rs] = x_ref[0, rs] + ga_ref[0] * out[i]


def _merge(x, mods, mod_row, o_mla, o_f, o_b, sg, gm, gh, wts, tm):
    bsz, n, _ = x.shape
    g_hg, hmean, w_br_mla, w_br_hg, w_out = wts

    def tok(width):
        return pl.BlockSpec((1, tm, width), lambda b, i: (b, i, 0))

    return pl.pallas_call(
        _merge_kernel,
        grid=(bsz, n // tm),
        in_specs=[tok(D_MODEL), pl.BlockSpec((1, 1, D_MODEL), lambda b, i: (mod_row(b), 0, 2)),
                  tok(MLA_W), tok(HG_W), tok(HG_W), tok(HG_W), tok(D_MODEL), tok(D_MODEL),
                  _const_spec((1, HG_W)), _const_spec(hmean.shape), _const_spec(w_br_mla.shape),
                  _const_spec(w_br_hg.shape), _const_spec(w_out.shape)],
        out_specs=tok(D_MODEL),
        out_shape=jax.ShapeDtypeStruct((bsz, n, D_MODEL), F32),
        compiler_params=_cparams(("parallel", "parallel")),
        name="merge",
    )(x, mods, o_mla, o_f, o_b, sg, gm, gh, g_hg, hmean, w_br_mla, w_br_hg, w_out)


FFN_CHUNK = 1024


def _ffn_kernel(x_ref, sh_ref, sc_ref, ga_ref, gff_ref, w1_ref, w2_ref, gfin_ref, y_ref):
    tm = x_ref.shape[1]
    halves = [slice(i * tm // 2, (i + 1) * tm // 2) for i in range(2)]
    h = [(_rms(x_ref[0, rs], gff_ref[...]) * (1.0 + sc_ref[0]) + sh_ref[0]).astype(BF16) for rs in halves]

    up = [(i, c0) for c0 in range(0, D_FF, FFN_CHUNK) for i in range(2)]
    acts = [[], []]

    def use_up(i):
        def use(a):
            a = jnp.maximum(a, 0.0)
            acts[i].append((a * a).astype(BF16))
        return use

    def use_down(rs):
        def use(f):
            y_ref[0, rs] = _rms(x_ref[0, rs] + ga_ref[0] * f, gfin_ref[...])
        return use

    stages = [(functools.partial(lambda i, c0: _dot(h[i], w1_ref[:, c0:c0 + FFN_CHUNK]), i, c0), use_up(i))
              for i, c0 in up]
    stages += [(functools.partial(lambda i: _dot(jnp.concatenate(acts[i], axis=1), w2_ref[...]), i), use_down(rs))
               for i, rs in enumerate(halves)]
    _run_staggered(stages)


def _ffn(x1, mods, mod_row, wts, tm):
    bsz, n, _ = x1.shape
    g_ff, w_ff1, w_ff2, g_final = wts

    def mod(col):
        return pl.BlockSpec((1, 1, D_MODEL), lambda b, i: (mod_row(b), 0, col))

    tok = pl.BlockSpec((1, tm, D_MODEL), lambda b, i: (b, i, 0))
    return pl.pallas_call(
        _ffn_kernel,
        grid=(bsz, n // tm),
        in_specs=[tok, mod(3), mod(4), mod(5), _const_spec((1, D_MODEL)),
                  _const_spec(w_ff1.shape), _const_spec(w_ff2.shape), _const_spec((1, D_MODEL))],
        out_specs=tok,
        out_shape=jax.ShapeDtypeStruct((bsz, n, D_MODEL), F32),
        compiler_params=_cparams(("parallel", "parallel")),
        name="ffn",
    )(x1, mods, mods, mods, g_ff, w_ff1, w_ff2, g_final)


def _pad_head_cols(w, widths_in, layout):
    rows = w.shape[0]
    w = w.reshape(rows, MLA_HEADS, sum(width for _, width in widths_in))
    pieces, off = {}, 0
    for name, width in widths_in:
        pieces[name] = w[:, :, off:off + width]
        off += width
    cols = [pieces[name] if name in pieces else jnp.zeros((rows, MLA_HEADS, width), w.dtype) for name, width in layout]
    return jnp.concatenate(cols, axis=-1).reshape(rows, -1)


def _prep_weights(w_in, w_uq, w_ukv):
    w_lat = w_in[:, :_LAT_W].astype(BF16)
    w_kr = jnp.pad(w_in[:, _LAT_W:_LAT_W + QK_ROPE].astype(BF16), ((0, 0), (0, LANES - QK_ROPE)))
    w_mix = w_in[:, _LAT_W + QK_ROPE:].astype(BF16)
    w_in_p = (w_lat, w_kr, w_mix)
    w_uq_p = _pad_head_cols(w_uq, (("nope", QK_NOPE), ("rope", QK_ROPE)),
                            (("rope", QK_ROPE), ("zero", HEAD_PAD - QK_ROPE - QK_NOPE), ("nope", QK_NOPE))).astype(BF16)
    kv_in = (("nope", QK_NOPE), ("v", V_HEAD))
    k_cols = _pad_head_cols(w_ukv, kv_in, (("zero", HEAD_PAD - QK_NOPE), ("nope", QK_NOPE)))
    v_cols = _pad_head_cols(w_ukv, kv_in, (("v", V_HEAD), ("zero", HEAD_PAD - V_HEAD)))
    w_ukv_p = jnp.concatenate([k_cols, v_cols], axis=1).astype(BF16)
    return w_in_p, w_uq_p, w_ukv_p


def _rope_tables(n):
    rows = n // GRID_W
    row = jnp.repeat(jnp.arange(rows, dtype=F32), GRID_W)
    col = jnp.tile(jnp.arange(GRID_W, dtype=F32), rows)
    half = QK_ROPE // 2
    inv = ROPE_BASE ** (-jnp.arange(0, half, 2, dtype=F32) / half)
    ang = jnp.concatenate([row[:, None] * inv, col[:, None] * inv], axis=-1)
    cos = jnp.repeat(jnp.cos(ang), 2, axis=-1)
    sin = jnp.repeat(jnp.sin(ang), 2, axis=-1)
    cos = jnp.concatenate([cos, jnp.ones((n, LANES - QK_ROPE), F32)], axis=-1)
    sin = jnp.concatenate([sin, jnp.zeros((n, LANES - QK_ROPE), F32)], axis=-1)
    return cos, sin


def _trunk(x, mods, mod_row, wts, rope_tabs, ctx):
    (inproj_w, merge_w, ffn_w, w_ukv_p) = wts
    bsz, n, _ = x.shape
    tm = 256
    if ctx is None:
        cache, past = None, 0
        s0 = None
    else:
        ckv_c, krope_c, state_c = ctx
        cache, past = (ckv_c, jnp.pad(krope_c, ((0, 0), (0, 0), (0, LANES - QK_ROPE)))), ckv_c.shape[1]
        s0 = jnp.stack([_state_to_pairs(state_c[:, 0]), _state_to_pairs(state_c[:, 1])], axis=1)
    outs = _inproj(x, mods, mod_row, inproj_w, rope_tabs, ctx is None, tm, cache)
    q, k, v, qh, kf, lff, kb, lfb, vh, sg, gm, gh = outs[:12]
    if ctx is None:
        ckv, krope = outs[12:]
    bk = next(b for b in (768, 512, 256, 128) if (n + past) % b == 0)
    o_mla = _attention(q, k, v, min(n, 1024), bk)
    o_f, o_b, *state = _hgrn(qh, kf, lff, kb, lfb, vh, s0, ctx is None)
    per_token = (x, o_mla, o_f, o_b, sg, gm, gh)
    if ctx is None:
        per_token = tuple(a.reshape(1, bsz * n, a.shape[-1]) for a in per_token)
    rows = per_token[0].shape[1]
    x1 = _merge(*per_token[:1], mods, mod_row, *per_token[1:], merge_w, min(rows, 512))
    y = _ffn(x1, mods, mod_row, ffn_w, min(512, rows)).reshape(bsz, n, D_MODEL)
    if ctx is None:
        return y, (ckv, krope, state[0])
    return y, None


def kernel(x_prompt, x_sample, cache_ckv, cache_krope, state_hgrn, c, c_ctx, w_ada, b_ada, g_norm_mix, g_norm_ff, w_in, g_q_norm, w_uq, g_kv_norm, w_ukv, g_hg_norm, hg_lb_logits, w_br_mla, w_br_hg, w_out, w_ff1, w_ff2, g_final):
    assert w_in.shape[0] == 1, "single-layer trunk"
    dec_b = c.shape[0]
    cc = jnp.concatenate([c, c_ctx[None, :], jnp.zeros((8 - dec_b - 1, D_MODEL), F32)], axis=0)
    mods = _ada(cc, w_ada[0], b_ada[0][None, :]).reshape(8, 1, 6 * D_MODEL)

    w_in_p, w_uq_p, w_ukv_p = _prep_weights(w_in[0], w_uq[0], w_ukv[0])
    inproj_w = (w_in_p, g_norm_mix[0][None], g_q_norm[0][None], w_uq_p, g_kv_norm[0][None], w_ukv_p, hg_lb_logits)
    head_id = np.arange(HG_W) // HG_DV
    hmean = (head_id[:, None] == head_id[None, :]).astype(np.float32) / HG_DV
    hmean = jnp.asarray(np.concatenate([hmean, hmean], axis=0), BF16)
    merge_w = (g_hg_norm[0][None], hmean, w_br_mla[0].astype(BF16), w_br_hg[0].astype(BF16), w_out[0].astype(BF16))
    ffn_w = (g_norm_ff[0][None], w_ff1[0].astype(BF16), w_ff2[0].astype(BF16), g_final[None])
    wts = (inproj_w, merge_w, ffn_w, w_ukv_p)

    y_prompt, (ckv, krope, state) = _trunk(x_prompt, mods, lambda b: dec_b, wts, None, None)
    rope_tabs = _rope_tables(x_sample.shape[1])
    y_sample, _ = _trunk(x_sample, mods, lambda b: b, wts, rope_tabs,
                         (cache_ckv[:, 0], cache_krope[:, 0], state_hgrn[:, 0]))
    return (y_prompt, y_sample, ckv[:, None], krope[:, None], state[:, None])
```

```python
import functools

import numpy as np
import jax
import jax.numpy as jnp
from jax import lax
from jax.experimental import pallas as pl
from jax.experimental.pallas import tpu as pltpu

D_MODEL = 1024
GRID_W = 64
MLA_HEADS = 8
Q_LORA = 384
KV_LORA = 256
QK_NOPE = 64
QK_ROPE = 32
V_HEAD = 64
MLA_W = MLA_HEADS * V_HEAD
MLA_SCALE = (QK_NOPE + QK_ROPE) ** -0.5
LOG2_E = 1.4426950408889634
HG_HEADS = 8
HG_DK = 64
HG_DV = 64
HG_KW = HG_HEADS * HG_DK
HG_W = HG_HEADS * HG_DV
D_FF = 4 * D_MODEL
ROPE_BASE = 10000.0
EPS = 1e-6

LANES = 128
HEAD_PAD = LANES
QK_PAD_W = MLA_HEADS * HEAD_PAD
N_PAIRS = HG_HEADS // 2
VMEM_LIMIT = 56 * 1024 * 1024

_LAT_W = Q_LORA + KV_LORA
_MIX_SIZES = (HG_KW, HG_KW, HG_KW, HG_W, HG_W, D_MODEL, D_MODEL)
_MIX_OFFS = tuple(int(o) for o in np.cumsum((0,) + _MIX_SIZES))

HG_CHUNK = 128
HG_LEVELS = 7

F32 = jnp.float32
BF16 = jnp.bfloat16


def _cparams(sem):
    return pltpu.CompilerParams(dimension_semantics=sem, vmem_limit_bytes=VMEM_LIMIT)


def _rms(x, g):
    return x * lax.rsqrt(jnp.mean(x * x, axis=-1, keepdims=True) + EPS) * g


def _dot(a, b):
    return jnp.dot(a, b, preferred_element_type=F32)


def _dot_nt(a, b):
    return lax.dot_general(a, b, (((1,), (1,)), ((), ())), preferred_element_type=F32)


def _dot_tn(a, b):
    return lax.dot_general(a, b, (((0,), (0,)), ((), ())), preferred_element_type=F32)


def _split_hi_lo(x):
    hi = x.astype(BF16)
    lo = (x - hi.astype(F32)).astype(BF16)
    return hi, lo


def _ada_kernel(c_ref, w_ref, b_ref, o_ref):
    c = c_ref[...]
    a = c * jax.nn.sigmoid(c)
    a_hi, a_lo = _split_hi_lo(a)
    w_hi, w_lo = _split_hi_lo(w_ref[...])
    o_ref[...] = _dot(a_hi, w_hi) + _dot(a_hi, w_lo) + _dot(a_lo, w_hi) + b_ref[...]


def _ada(cc, w_ada, b_ada):
    rows, tn = cc.shape[0], 1536
    n = w_ada.shape[1]
    return pl.pallas_call(
        _ada_kernel,
        grid=(n // tn,),
        in_specs=[pl.BlockSpec((rows, D_MODEL), lambda j: (0, 0)),
                  pl.BlockSpec((D_MODEL, tn), lambda j: (0, j)),
                  pl.BlockSpec((1, tn), lambda j: (0, j))],
        out_specs=pl.BlockSpec((rows, tn), lambda j: (0, j)),
        out_shape=jax.ShapeDtypeStruct((rows, n), F32),
        compiler_params=_cparams(("arbitrary",)),
        name="ada",
    )(cc, w_ada, b_ada)


def _rope_tile(blk, cos, sin, even):
    rot = jnp.where(even, -pltpu.roll(blk, LANES - 1, 1), pltpu.roll(blk, 1, 1))
    return blk * cos + rot * sin


def _run_staggered(stages):
    pending = stages[0][0]()
    for i, (_, consume) in enumerate(stages):
        current = pending
        if i + 1 < len(stages):
            pending = stages[i + 1][0]()
        consume(current)


def _store_kv(kv, kr, k_ref, v_ref, rs=slice(None)):
    one_lane = ((lax.broadcasted_iota(jnp.int32, (1, QK_PAD_W), 1) & (HEAD_PAD - 1)) == V_HEAD).astype(F32)
    for hh in range(MLA_HEADS):
        sl = slice(hh * HEAD_PAD, (hh + 1) * HEAD_PAD)
        k_ref[0, rs, sl] = (kv[:, sl] + kr).astype(BF16)
        vsl = slice(QK_PAD_W + hh * HEAD_PAD, QK_PAD_W + (hh + 1) * HEAD_PAD)
        v_ref[0, sl, rs] = (kv[:, vsl] + one_lane[:, sl]).T.astype(BF16)


N_INPROJ_IN = 11


def _inproj_kernel(*refs, rope, cache_out, n_blocks, with_cache):
    if with_cache:
        step = pl.program_id(1)
        n_in = N_INPROJ_IN + 2 * rope
        ckvc_ref, krc_ref = refs[n_in:n_in + 2]
        refs = refs[:n_in] + refs[n_in + 2:]
        k_ref, v_ref, wukv_ref = refs[n_in + 1], refs[n_in + 2], refs[9]

        @pl.when(step >= n_blocks)
        def _():
            _store_kv(_dot(ckvc_ref[0].astype(BF16), wukv_ref[...]), krc_ref[0], k_ref, v_ref)

        pl.when(step < n_blocks)(functools.partial(_inproj_tile, refs, rope, cache_out))
    else:
        _inproj_tile(refs, rope, cache_out)


def _inproj_tile(refs, rope, cache_out):
    (x_ref, sh_ref, sc_ref, gmix_ref, wlat_ref, wmix_ref, gq_ref, wuq_ref, gkv_ref, wukv_ref, lbl_ref) = refs[:N_INPROJ_IN]
    refs = refs[N_INPROJ_IN:]
    if rope:
        cos_ref, sin_ref = refs[:2]
        refs = refs[2:]
    (q_ref, k_ref, v_ref, qh_ref, kf_ref, lff_ref, kb_ref, lfb_ref, vh_ref, sg_ref, gm_ref, gh_ref) = refs[:12]
    refs = refs[12:]
    tm = x_ref.shape[1]
    n_parts = 2 if tm >= 512 else 1
    parts = [slice(i * tm // n_parts, (i + 1) * tm // n_parts) for i in range(n_parts)]

    def stages_for(rs):
        hb = (_rms(x_ref[0, rs], gmix_ref[...]) * (1.0 + sc_ref[0]) + sh_ref[0]).astype(BF16)
        if rope:
            cos, sin = cos_ref[rs], sin_ref[rs]
            even = (lax.broadcasted_iota(jnp.int32, cos.shape, 1) & 1) == 0
        kept = {}

        def use_latents(y):
            kept["qn"] = _rms(y[:, :Q_LORA], gq_ref[...]).astype(BF16)
            kept["ckv"] = _rms(y[:, Q_LORA:_LAT_W], gkv_ref[...])
            kr = y[:, _LAT_W:]
            if cache_out:
                refs[0][0, rs] = kept["ckv"]
                refs[1][0, rs] = kr[:, :QK_ROPE]
            kept["kr"] = _rope_tile(kr, cos, sin, even) if rope else kr

        def use_q(q):
            q = q * (MLA_SCALE * LOG2_E)
            for hh in range(MLA_HEADS):
                sl = slice(hh * HEAD_PAD, (hh + 1) * HEAD_PAD)
                blk = q[:, sl]
                if rope:
                    blk = _rope_tile(blk, cos, sin, even)
                q_ref[0, sl, rs] = blk.T.astype(BF16)

        def use_hq(hq):
            qh_ref[0, rs] = hq * jax.nn.sigmoid(hq) * (HG_DK ** -0.5)

        def use_forget(d, k_out, lf_out):
            def use(z):
                l0, l1 = lbl_ref[0, d:d + 1], lbl_ref[1, d:d + 1]
                lmax = jnp.maximum(l0, l1)
                e0, e1 = jnp.exp(l0 - lmax), jnp.exp(l1 - lmax)
                lbd = e0 / (e0 + e1)
                f = lbd + (1.0 - lbd) * jax.nn.sigmoid(z)
                k_out[0, rs] = 1.0 - f
                lf_out[0, rs] = jnp.log2(f)
            return use

        def use_hi(y):
            vh_ref[0, rs] = y.astype(vh_ref.dtype)

        def use_hg(hg):
            sg_ref[0, rs] = (hg * jax.nn.sigmoid(hg)).astype(sg_ref.dtype)

        def use_gate(out_ref):
            def use(y):
                out_ref[0, rs] = jax.nn.sigmoid(y).astype(out_ref.dtype)
            return use

        def mix(i):
            return lambda: _dot(hb, wmix_ref[:, _MIX_OFFS[i]:_MIX_OFFS[i + 1]])

        return [
            (lambda: _dot(hb, wlat_ref[...]), use_latents),
            (mix(0), use_hq),
            (lambda: _dot(kept["qn"], wuq_ref[...]), use_q),
            (lambda: _dot(kept["ckv"].astype(BF16), wukv_ref[...]),
             lambda kv: _store_kv(kv, kept["kr"], k_ref, v_ref, rs)),
            (mix(1), use_forget(0, kf_ref, lff_ref)),
            (mix(2), use_forget(1, kb_ref, lfb_ref)),
            (mix(3), use_hi),
            (mix(4), use_hg),
            (mix(5), use_gate(gm_ref)),
            (mix(6), use_gate(gh_ref)),
        ]

    per_part = [stages_for(rs) for rs in parts]
    _run_staggered([stage for group in zip(*per_part) for stage in group])


def _const_spec(shape):
    return pl.BlockSpec(shape, lambda *_: (0,) * len(shape), pipeline_mode=pl.Buffered(1))


def _inproj(x, mods, mod_row, wts, rope_tabs, cache_out, tm, cache):
    bsz, n, _ = x.shape
    rope = rope_tabs is not None
    (w_lat, w_mix), g_mix, g_q, w_uq_p, g_kv, w_ukv_p, lb_logits = wts
    n_blocks = n // tm
    past = 0 if cache is None else cache[0].shape[1]
    assert n_blocks * tm == n and past % tm == 0
    kv_rows = n + past

    def own(i):
        return i if cache is None else jnp.minimum(i, n_blocks - 1)

    def tok(width):
        return pl.BlockSpec((1, tm, width), lambda b, i: (b, own(i), 0))

    def mod(col):
        return pl.BlockSpec((1, 1, D_MODEL), lambda b, i: (mod_row(b), 0, col))

    in_specs = [tok(D_MODEL), mod(0), mod(1), _const_spec((1, D_MODEL)), _const_spec(w_lat.shape),
                _const_spec(w_mix.shape),
                _const_spec((1, Q_LORA)), _const_spec(w_uq_p.shape), _const_spec((1, KV_LORA)),
                _const_spec(w_ukv_p.shape), _const_spec(lb_logits.shape)]
    args = [x, mods, mods, g_mix, w_lat, w_mix, g_q, w_uq_p, g_kv, w_ukv_p, lb_logits]
    assert len(args) == N_INPROJ_IN
    if rope:
        in_specs += [pl.BlockSpec((tm, LANES), lambda b, i: (own(i), 0))] * 2
        args += list(rope_tabs)
    if cache is not None:
        in_specs += [pl.BlockSpec((1, tm, w), lambda b, i: (b, jnp.maximum(i - n_blocks, 0), 0))
                     for w in (KV_LORA, LANES)]
        args += list(cache)

    widths = [(QK_PAD_W, BF16)] * 3 + [(HG_KW, F32)] * 5 + [(HG_W, BF16)] * 2 + [(D_MODEL, BF16)] * 2
    if cache_out:
        widths += [(KV_LORA, F32), (QK_ROPE, F32)]
    out_specs = [tok(w) for w, _ in widths]
    out_shape = [jax.ShapeDtypeStruct((bsz, n, w), dt) for w, dt in widths]
    out_specs[0] = pl.BlockSpec((1, QK_PAD_W, tm), lambda b, i: (b, 0, own(i)))
    out_shape[0] = jax.ShapeDtypeStruct((bsz, QK_PAD_W, n), BF16)
    out_specs[1] = pl.BlockSpec((1, tm, QK_PAD_W), lambda b, i: (b, i, 0))
    out_shape[1] = jax.ShapeDtypeStruct((bsz, kv_rows, QK_PAD_W), BF16)
    out_specs[2] = pl.BlockSpec((1, QK_PAD_W, tm), lambda b, i: (b, 0, i))
    out_shape[2] = jax.ShapeDtypeStruct((bsz, QK_PAD_W, kv_rows), BF16)
    return pl.pallas_call(
        functools.partial(_inproj_kernel, rope=rope, cache_out=cache_out, n_blocks=n_blocks,
                          with_cache=cache is not None),
        grid=(bsz, kv_rows // tm),
        in_specs=in_specs, out_specs=out_specs, out_shape=out_shape,
        compiler_params=_cparams(("parallel", "arbitrary")),
        name="inproj_rope" if rope else "inproj",
    )(*args)


def _attn_kernel(q_ref, k_ref, v_ref, o_ref, m_scr, acc_scr):
    j = pl.program_id(2)

    @pl.when(j == 0)
    def _():
        m_scr[...] = jnp.full(m_scr.shape, -jnp.inf, F32)
        acc_scr[...] = jnp.zeros(acc_scr.shape, F32)

    def scores(hh):
        sl = slice(hh * HEAD_PAD, (hh + 1) * HEAD_PAD)
        return _dot(k_ref[0, :, sl], q_ref[0, sl, :])

    ahead = 2
    pending = [scores(hh) for hh in range(ahead)]
    for hh in range(MLA_HEADS):
        sl = slice(hh * HEAD_PAD, (hh + 1) * HEAD_PAD)
        s = pending.pop(0)
        if hh + ahead < MLA_HEADS:
            pending.append(scores(hh + ahead))
        m_prev = m_scr[hh]
        m_cur = jnp.maximum(m_prev, jnp.max(s, axis=0, keepdims=True))
        alpha = jnp.exp2(m_prev - m_cur)
        p = jnp.exp2(s - m_cur[:1])
        m_scr[hh] = m_cur
        acc_scr[hh] = alpha[:1] * acc_scr[hh] + _dot(v_ref[0, sl, :], p.astype(BF16))

    @pl.when(j == pl.num_programs(2) - 1)
    def _():
        def head_out(hh):
            acc = acc_scr[hh]
            return acc[:V_HEAD] / acc[V_HEAD:V_HEAD + 1]

        for pair in range(MLA_HEADS // 2):
            both = jnp.concatenate([head_out(2 * pair), head_out(2 * pair + 1)], axis=0)
            o_ref[0, :, pair * LANES:(pair + 1) * LANES] = both.T.astype(o_ref.dtype)


def _attention(q, k, v, bq, bk):
    bsz, _, nq = q.shape
    nk = k.shape[1]
    return pl.pallas_call(
        _attn_kernel,
        grid=(bsz, nq // bq, nk // bk),
        in_specs=[pl.BlockSpec((1, QK_PAD_W, bq), lambda b, i, j: (b, 0, i)),
                  pl.BlockSpec((1, bk, QK_PAD_W), lambda b, i, j: (b, j, 0)),
                  pl.BlockSpec((1, QK_PAD_W, bk), lambda b, i, j: (b, 0, j))],
        out_specs=pl.BlockSpec((1, bq, MLA_W), lambda b, i, j: (b, i, 0)),
        out_shape=jax.ShapeDtypeStruct((bsz, nq, MLA_W), BF16),
        scratch_shapes=[pltpu.VMEM((MLA_HEADS, 8, bq), F32),
                        pltpu.VMEM((MLA_HEADS, HEAD_PAD, bq), F32)],
        compiler_params=_cparams(("parallel", "parallel", "arbitrary")),
        name="attn",
    )(q, k, v)


HG_TILE_LEVELS = 3


def _hgrn_constants(reverse):
    c = HG_CHUNK
    t = np.arange(c)
    mats = [(t[None, :] <= t[:, None])]
    for lvl in range(1, HG_TILE_LEVELS + 1):
        g, half = 1 << lvl, 1 << (lvl - 1)
        p = t % g
        mid = t - p + half - 1
        isq = p >= half
        u = t[None, :]
        mats.append(np.where(isq[:, None], (u > mid[:, None]) & (u <= t[:, None]),
                             (u > t[:, None]) & (u <= mid[:, None])))
    mstack = np.stack(mats).astype(np.float32)
    if reverse:
        mstack = mstack[:, ::-1, ::-1]
    mstack = mstack.reshape(-1, c)
    return jnp.asarray(np.concatenate([mstack, mstack], axis=1), BF16)


def _hgrn_kernel(qf_ref, qb_ref, kf_ref, lff_ref, kb_ref, lfb_ref, vf_ref, vb_ref, mstack_ref, *rest,
                 has_s0, emit_state):
    if has_s0:
        s0_ref, rest = rest[0], rest[1:]
    of_ref, ob_ref = rest[:2]
    rest = rest[2:]
    st_scr = rest[-1]
    c = HG_CHUNK
    step = pl.program_id(1)

    @pl.when(step == 0)
    def _():
        st_scr[...] = s0_ref[0] if has_s0 else jnp.zeros(st_scr.shape, F32)

    dirs = ((qf_ref, kf_ref, lff_ref, vf_ref, of_ref, False), (qb_ref, kb_ref, lfb_ref, vb_ref, ob_ref, True))
    exps = []
    for d, (_, _, lf_ref, _, _, _) in enumerate(dirs):
        lf_hi, lf_lo = _split_hi_lo(lf_ref[0])
        exps.append(_dot(mstack_ref[d], jnp.concatenate([lf_hi, lf_lo], axis=0)))

    units = [_hgrn_pair(q_ref, k_ref, v_ref, o_ref, exps[d], st_scr.at[d], pair, reverse)
             for pair in range(N_PAIRS) for d, (q_ref, k_ref, _, v_ref, o_ref, reverse) in enumerate(dirs)]
    ahead = 2
    for unit in units[:ahead]:
        next(unit)
    for i, unit in enumerate(units):
        if i + ahead < len(units):
            next(units[i + ahead])
        next(unit, None)

    if emit_state:
        sfin_ref = rest[0]

        @pl.when(step == pl.num_programs(1) - 1)
        def _():
            for d in range(2):
                for pair in range(N_PAIRS):
                    s_pair = st_scr[d, pair].T
                    sfin_ref[0, d, 2 * pair] = s_pair[:HG_DK, :HG_DV]
                    sfin_ref[0, d, 2 * pair + 1] = pltpu.roll(s_pair, HG_DV, 1)[HG_DK:, :HG_DV]


def _hgrn_pair(q_ref, k_ref, v_ref, o_ref, e, st_scr, pair, reverse):
    c = HG_CHUNK
    last = 0 if reverse else c - 1
    row = lax.broadcasted_iota(jnp.int32, (c, LANES), 0)
    lane_low = lax.broadcasted_iota(jnp.int32, (c, LANES), 1) < HG_DK
    low_b = jnp.where(lane_low, 1.0, 0.0).astype(BF16)
    high_b = jnp.where(lane_low, 0.0, 1.0).astype(BF16)
    blank_t = jnp.zeros((HG_DK, c), BF16)
    xor = lax.broadcasted_iota(jnp.int32, (c, c), 0) ^ lax.broadcasted_iota(jnp.int32, (c, c), 1)
    diag_block = (lax.broadcasted_iota(jnp.int32, (LANES, LANES), 0) < HG_DV) == (
        lax.broadcasted_iota(jnp.int32, (LANES, LANES), 1) < HG_DK)
    zeros8 = jnp.zeros((8, LANES), F32)

    def q_side_block(b):
        return (b % 2 == 1) != reverse

    sl = slice(pair * LANES, (pair + 1) * LANES)
    q = q_ref[0, :, sl]
    k = k_ref[0, :, sl]
    v = v_ref[0, :, sl]
    cum = e[0:c, sl]
    cum_last = cum[last:last + 1]

    def blank(rows):
        return jnp.concatenate([zeros8] * (rows // 8), axis=0)

    products = []
    for lvl in range(HG_LEVELS, -1, -1):
        half = (1 << lvl) // 2
        if lvl == 0:
            lhs, zk = q.astype(BF16), k
        elif lvl > HG_TILE_LEVELS:
            ys, zs = [], []
            for b in range(0, c // half, 2):
                ref_row = (b + 1) * half if reverse else (b + 1) * half - 1
                r = cum[ref_row:ref_row + 1]
                for bb in (b, b + 1):
                    rs = slice(bb * half, (bb + 1) * half)
                    if q_side_block(bb):
                        ys.append(q[rs] * jnp.exp2(cum[rs] - r))
                        zs.append(blank(half))
                    else:
                        zs.append(k[rs] * jnp.exp2(r - cum[rs]))
            lhs = jnp.concatenate(ys, axis=0).astype(BF16)
            zk = jnp.concatenate(zs, axis=0)
        else:
            z = jnp.exp2(e[lvl * c:(lvl + 1) * c, sl])
            q_row = ((row & half) == 0) if reverse else ((row & half) != 0)
            y = jnp.where(q_row, q, k) * z
            lhs = y.astype(BF16)
            zk = jnp.where(q_row, 0.0, y)
        zt = zk.T.astype(BF16)
        rhs_t = jnp.concatenate([jnp.concatenate([zt[:HG_DK], blank_t], axis=0),
                                 jnp.concatenate([blank_t, zt[HG_DK:]], axis=0)], axis=1)
        products.append(_dot(lhs, rhs_t))
    yield
    a_cat = None
    for lvl, p in zip(range(HG_LEVELS, -1, -1), products):
        half = (1 << lvl) // 2
        if lvl > HG_TILE_LEVELS:
            q_blocks = [bb for bb in range(c // half) if q_side_block(bb)]
            new = []
            for h_idx in range(2):
                p_h = p[:, h_idx * c:(h_idx + 1) * c]
                rows = []
                for bb in range(c // half):
                    rs = slice(bb * half, (bb + 1) * half)
                    if not q_side_block(bb):
                        rows.append(blank(half) if a_cat is None else a_cat[h_idx][rs])
                        continue
                    i = q_blocks.index(bb)
                    p_blk = p_h[i * half:(i + 1) * half]
                    if a_cat is None:
                        rows.append(p_blk)
                    else:
                        rows.append(jnp.where(xor[rs] < (1 << lvl), p_blk, a_cat[h_idx][rs]))
                new.append(jnp.concatenate(rows, axis=0))
            a_cat = tuple(new)
        else:
            same = xor < (1 << lvl)
            a_cat = tuple(jnp.where(same, p[:, h_idx * c:(h_idx + 1) * c], a_cat[h_idx]) for h_idx in range(2))
    a_cat = jnp.concatenate(a_cat, axis=1)
    v_cat = jnp.concatenate([v * low_b, v * high_b], axis=0)
    st = st_scr[pair]
    o = _dot(a_cat.astype(BF16), v_cat) + _dot_nt((q * jnp.exp2(cum)).astype(BF16), st.astype(BF16))
    o_ref[0, :, sl] = o
    kd = (k * jnp.exp2(cum_last - cum)).astype(BF16)
    upd = _dot_tn(v, kd)
    st_scr[pair] = jnp.exp2(cum_last) * st + jnp.where(diag_block, upd, 0.0)


def _hgrn(qh, kf, lff, kb, lfb, vh, s0t, emit_state):
    bsz, n, _ = qh.shape
    c = HG_CHUNK
    nc = n // c
    mstack = jnp.stack([_hgrn_constants(False), _hgrn_constants(True)])
    fwd = pl.BlockSpec((1, c, HG_KW), lambda b, i: (b, i, 0))
    bwd = pl.BlockSpec((1, c, HG_KW), lambda b, i: (b, nc - 1 - i, 0))
    in_specs = [fwd, bwd, fwd, fwd, bwd, bwd, fwd, bwd, _const_spec(mstack.shape)]
    args = [qh, qh, kf, lff, kb, lfb, vh, vh, mstack]
    if s0t is not None:
        in_specs.append(pl.BlockSpec((1, 2, N_PAIRS, LANES, LANES), lambda b, i: (b, 0, 0, 0, 0)))
        args.append(s0t)
    out_specs = [fwd, bwd]
    out_shape = [jax.ShapeDtypeStruct((bsz, n, HG_W), F32)] * 2
    if emit_state:
        out_specs.append(pl.BlockSpec((1, 2, HG_HEADS, HG_DK, HG_DV), lambda b, i: (b, 0, 0, 0, 0)))
        out_shape.append(jax.ShapeDtypeStruct((bsz, 2, HG_HEADS, HG_DK, HG_DV), F32))
    return pl.pallas_call(
        functools.partial(_hgrn_kernel, has_s0=s0t is not None, emit_state=emit_state),
        grid=(bsz, nc),
        in_specs=in_specs, out_specs=out_specs, out_shape=out_shape,
        scratch_shapes=[pltpu.VMEM((2, N_PAIRS, LANES, LANES), F32)],
        compiler_params=_cparams(("parallel", "arbitrary")),
        name="hgrn",
    )(*args)


def _state_to_pairs(s):
    b = s.shape[0]
    st = jnp.swapaxes(s, -1, -2).reshape(b, N_PAIRS, 2, HG_DV, HG_DK)
    eye = jnp.eye(2, dtype=s.dtype)
    out = jnp.einsum('bpavk,ac->bpavck', st, eye)
    return out.reshape(b, N_PAIRS, 2 * HG_DV, 2 * HG_DK)


def _merge_kernel(x_ref, ga_ref, om_ref, of_ref, ob_ref, sg_ref, gm_ref, gh_ref, ghg_ref, hmean_ref,
                  wbm_ref, wbh_ref, wout_ref, x1_ref):
    tm = x_ref.shape[1]
    halves = [slice(i * tm // 2, (i + 1) * tm // 2) for i in range(2)]
    o = [of_ref[0, rs] + ob_ref[0, rs] for rs in halves]

    def head_mean_sq(i):
        sq_hi, sq_lo = _split_hi_lo(o[i] * o[i])
        return _dot(jnp.concatenate([sq_hi, sq_lo], axis=1), hmean_ref[...])

    ms = [head_mean_sq(i) for i in range(2)]
    mla = [_dot(om_ref[0, rs], wbm_ref[...]) for rs in halves]
    hg = []
    for i, rs in enumerate(halves):
        o_hg = (o[i] * lax.rsqrt(ms[i] + EPS) * ghg_ref[...]) * sg_ref[0, rs]
        hg.append(_dot(o_hg.astype(BF16), wbh_ref[...]))
    out = []
    for i, rs in enumerate(halves):
        merged = gm_ref[0, rs] * mla[i] + gh_ref[0, rs] * hg[i]
        out.append(_dot(merged.astype(BF16), wout_ref[...]))
    for i, rs in enumerate(halves):
        x1_ref[0, rs] = x_ref[0, rs] + ga_ref[0] * out[i]


def _merge(x, mods, mod_row, o_mla, o_f, o_b, sg, gm, gh, wts, tm):
    bsz, n, _ = x.shape
    g_hg, hmean, w_br_mla, w_br_hg, w_out = wts

    def tok(width):
        return pl.BlockSpec((1, tm, width), lambda b, i: (b, i, 0))

    return pl.pallas_call(
        _merge_kernel,
        grid=(bsz, n // tm),
        in_specs=[tok(D_MODEL), pl.BlockSpec((1, 1, D_MODEL), lambda b, i: (mod_row(b), 0, 2)),
                  tok(MLA_W), tok(HG_W), tok(HG_W), tok(HG_W), tok(D_MODEL), tok(D_MODEL),
                  _const_spec((1, HG_W)), _const_spec(hmean.shape), _const_spec(w_br_mla.shape),
                  _const_spec(w_br_hg.shape), _const_spec(w_out.shape)],
        out_specs=tok(D_MODEL),
        out_shape=jax.ShapeDtypeStruct((bsz, n, D_MODEL), F32),
        compiler_params=_cparams(("parallel", "parallel")),
        name="merge",
    )(x, mods, o_mla, o_f, o_b, sg, gm, gh, g_hg, hmean, w_br_mla, w_br_hg, w_out)


FFN_CHUNK = 1024


def _ffn_kernel(x_ref, sh_ref, sc_ref, ga_ref, gff_ref, w1_ref, w2_ref, gfin_ref, y_ref):
    tm = x_ref.shape[1]
    halves = [slice(i * tm // 2, (i + 1) * tm // 2) for i in range(2)]
    h = [(_rms(x_ref[0, rs], gff_ref[...]) * (1.0 + sc_ref[0]) + sh_ref[0]).astype(BF16) for rs in halves]

    up = [(i, c0) for c0 in range(0, D_FF, FFN_CHUNK) for i in range(2)]
    acts = [[], []]

    def use_up(i):
        def use(a):
            a = jnp.maximum(a, 0.0)
            acts[i].append((a * a).astype(BF16))
        return use

    def use_down(rs):
        def use(f):
            y_ref[0, rs] = _rms(x_ref[0, rs] + ga_ref[0] * f, gfin_ref[...])
        return use

    stages = [(functools.partial(lambda i, c0: _dot(h[i], w1_ref[:, c0:c0 + FFN_CHUNK]), i, c0), use_up(i))
              for i, c0 in up]
    stages += [(functools.partial(lambda i: _dot(jnp.concatenate(acts[i], axis=1), w2_ref[...]), i), use_down(rs))
               for i, rs in enumerate(halves)]
    _run_staggered(stages)


def _ffn(x1, mods, mod_row, wts, tm):
    bsz, n, _ = x1.shape
    g_ff, w_ff1, w_ff2, g_final = wts

    def mod(col):
        return pl.BlockSpec((1, 1, D_MODEL), lambda b, i: (mod_row(b), 0, col))

    tok = pl.BlockSpec((1, tm, D_MODEL), lambda b, i: (b, i, 0))
    return pl.pallas_call(
        _ffn_kernel,
        grid=(bsz, n // tm),
        in_specs=[tok, mod(3), mod(4), mod(5), _const_spec((1, D_MODEL)),
                  _const_spec(w_ff1.shape), _const_spec(w_ff2.shape), _const_spec((1, D_MODEL))],
        out_specs=tok,
        out_shape=jax.ShapeDtypeStruct((bsz, n, D_MODEL), F32),
        compiler_params=_cparams(("parallel", "parallel")),
        name="ffn",
    )(x1, mods, mods, mods, g_ff, w_ff1, w_ff2, g_final)


def _pad_head_cols(w, widths_in, layout):
    rows = w.shape[0]
    w = w.reshape(rows, MLA_HEADS, sum(width for _, width in widths_in))
    pieces, off = {}, 0
    for name, width in widths_in:
        pieces[name] = w[:, :, off:off + width]
        off += width
    cols = [pieces[name] if name in pieces else jnp.zeros((rows, MLA_HEADS, width), w.dtype) for name, width in layout]
    return jnp.concatenate(cols, axis=-1).reshape(rows, -1)


def _split_w_in_kernel(w_ref, lat_ref, mix_ref):
    w = w_ref[...]
    lat = w[:, :_LAT_W + LANES]
    keep = lax.broadcasted_iota(jnp.int32, lat.shape, 1) < _LAT_W + QK_ROPE
    lat_ref[...] = jnp.where(keep, lat, 0.0).astype(BF16)
    mix_ref[...] = w[:, _LAT_W + QK_ROPE:].astype(BF16)


def _split_w_in(w_in):
    rows, cols = w_in.shape
    tr = 128
    n_mix = cols - _LAT_W - QK_ROPE
    return pl.pallas_call(
        _split_w_in_kernel,
        grid=(rows // tr,),
        in_specs=[pl.BlockSpec((tr, cols), lambda i: (i, 0))],
        out_specs=[pl.BlockSpec((tr, _LAT_W + LANES), lambda i: (i, 0)), pl.BlockSpec((tr, n_mix), lambda i: (i, 0))],
        out_shape=[jax.ShapeDtypeStruct((rows, _LAT_W + LANES), BF16), jax.ShapeDtypeStruct((rows, n_mix), BF16)],
        compiler_params=_cparams(("parallel",)),
        name="split_w_in",
    )(w_in)


def _prep_weights(w_in, w_uq, w_ukv):
    w_in_p = tuple(_split_w_in(w_in))
    w_uq_p = _pad_head_cols(w_uq, (("nope", QK_NOPE), ("rope", QK_ROPE)),
                            (("rope", QK_ROPE), ("zero", HEAD_PAD - QK_ROPE - QK_NOPE), ("nope", QK_NOPE))).astype(BF16)
    kv_in = (("nope", QK_NOPE), ("v", V_HEAD))
    k_cols = _pad_head_cols(w_ukv, kv_in, (("zero", HEAD_PAD - QK_NOPE), ("nope", QK_NOPE)))
    v_cols = _pad_head_cols(w_ukv, kv_in, (("v", V_HEAD), ("zero", HEAD_PAD - V_HEAD)))
    w_ukv_p = jnp.concatenate([k_cols, v_cols], axis=1).astype(BF16)
    return w_in_p, w_uq_p, w_ukv_p


def _rope_tables(n):
    rows = n // GRID_W
    row = jnp.repeat(jnp.arange(rows, dtype=F32), GRID_W)
    col = jnp.tile(jnp.arange(GRID_W, dtype=F32), rows)
    half = QK_ROPE // 2
    inv = ROPE_BASE ** (-jnp.arange(0, half, 2, dtype=F32) / half)
    ang = jnp.concatenate([row[:, None] * inv, col[:, None] * inv], axis=-1)
    cos = jnp.repeat(jnp.cos(ang), 2, axis=-1)
    sin = jnp.repeat(jnp.sin(ang), 2, axis=-1)
    cos = jnp.concatenate([cos, jnp.ones((n, LANES - QK_ROPE), F32)], axis=-1)
    sin = jnp.concatenate([sin, jnp.zeros((n, LANES - QK_ROPE), F32)], axis=-1)
    return cos, sin


def _trunk(x, mods, mod_row, wts, rope_tabs, ctx):
    (inproj_w, merge_w, ffn_w, w_ukv_p) = wts
    bsz, n, _ = x.shape
    tm = min(n, 512)
    if ctx is None:
        cache, past = None, 0
        s0 = None
    else:
        ckv_c, krope_c, state_c = ctx
        cache, past = (ckv_c, jnp.pad(krope_c, ((0, 0), (0, 0), (0, LANES - QK_ROPE)))), ckv_c.shape[1]
        s0 = jnp.stack([_state_to_pairs(state_c[:, 0]), _state_to_pairs(state_c[:, 1])], axis=1)
    outs = _inproj(x, mods, mod_row, inproj_w, rope_tabs, ctx is None, tm, cache)
    q, k, v, qh, kf, lff, kb, lfb, vh, sg, gm, gh = outs[:12]
    if ctx is None:
        ckv, krope = outs[12:]
    bk = next(b for b in (768, 512, 256, 128) if (n + past) % b == 0)
    o_mla = _attention(q, k, v, min(n, 1024), bk)
    o_f, o_b, *state = _hgrn(qh, kf, lff, kb, lfb, vh, s0, ctx is None)
    per_token = (x, o_mla, o_f, o_b, sg, gm, gh)
    if ctx is None:
        per_token = tuple(a.reshape(1, bsz * n, a.shape[-1]) for a in per_token)
    rows = per_token[0].shape[1]
    x1 = _merge(*per_token[:1], mods, mod_row, *per_token[1:], merge_w, min(rows, 512))
    y = _ffn(x1, mods, mod_row, ffn_w, min(512, rows)).reshape(bsz, n, D_MODEL)
    if ctx is None:
        return y, (ckv, krope, state[0])
    return y, None


def kernel(x_prompt, x_sample, cache_ckv, cache_krope, state_hgrn, c, c_ctx, w_ada, b_ada, g_norm_mix, g_norm_ff, w_in, g_q_norm, w_uq, g_kv_norm, w_ukv, g_hg_norm, hg_lb_logits, w_br_mla, w_br_hg, w_out, w_ff1, w_ff2, g_final):
    assert w_in.shape[0] == 1, "single-layer trunk"
    dec_b = c.shape[0]
    cc = jnp.concatenate([c, c_ctx[None, :], jnp.zeros((8 - dec_b - 1, D_MODEL), F32)], axis=0)
    mods = _ada(cc, w_ada[0], b_ada[0][None, :]).reshape(8, 1, 6 * D_MODEL)

    w_in_p, w_uq_p, w_ukv_p = _prep_weights(w_in[0], w_uq[0], w_ukv[0])
    inproj_w = (w_in_p, g_norm_mix[0][None], g_q_norm[0][None], w_uq_p, g_kv_norm[0][None], w_ukv_p, hg_lb_logits)
    head_id = np.arange(HG_W) // HG_DV
    hmean = (head_id[:, None] == head_id[None, :]).astype(np.float32) / HG_DV
    hmean = jnp.asarray(np.concatenate([hmean, hmean], axis=0), BF16)
    merge_w = (g_hg_norm[0][None], hmean, w_br_mla[0].astype(BF16), w_br_hg[0].astype(BF16), w_out[0].astype(BF16))
    ffn_w = (g_norm_ff[0][None], w_ff1[0].astype(BF16), w_ff2[0].astype(BF16), g_final[None])
    wts = (inproj_w, merge_w, ffn_w, w_ukv_p)

    y_prompt, (ckv, krope, state) = _trunk(x_prompt, mods, lambda b: dec_b, wts, None, None)
    rope_tabs = _rope_tables(x_sample.shape[1])
    y_sample, _ = _trunk(x_sample, mods, lambda b: b, wts, rope_tabs,
                         (cache_ckv[:, 0], cache_krope[:, 0], state_hgrn[:, 0]))
    return (y_prompt, y_sample, ckv[:, None], krope[:, None], state[:, None])
```

```python
import functools

import numpy as np
import jax
import jax.numpy as jnp
from jax import lax
from jax.experimental import pallas as pl
from jax.experimental.pallas import tpu as pltpu

D_MODEL = 1024
GRID_W = 64
MLA_HEADS = 8
Q_LORA = 384
KV_LORA = 256
QK_NOPE = 64
QK_ROPE = 32
V_HEAD = 64
MLA_W = MLA_HEADS * V_HEAD
MLA_SCALE = (QK_NOPE + QK_ROPE) ** -0.5
LOG2_E = 1.4426950408889634
HG_HEADS = 8
HG_DK = 64
HG_DV = 64
HG_KW = HG_HEADS * HG_DK
HG_W = HG_HEADS * HG_DV
D_FF = 4 * D_MODEL
ROPE_BASE = 10000.0
EPS = 1e-6

LANES = 128
HEAD_PAD = LANES
QK_PAD_W = MLA_HEADS * HEAD_PAD
N_PAIRS = HG_HEADS // 2
VMEM_LIMIT = 56 * 1024 * 1024

_LAT_W = Q_LORA + KV_LORA
_MIX_SIZES = (HG_KW, HG_KW, HG_KW, HG_W, HG_W)
_MIX_OFFS = tuple(int(o) for o in np.cumsum((0,) + _MIX_SIZES))

HG_CHUNK = 128
HG_LEVELS = 7

F32 = jnp.float32
BF16 = jnp.bfloat16


def _cparams(sem):
    return pltpu.CompilerParams(dimension_semantics=sem, vmem_limit_bytes=VMEM_LIMIT)


def _rms(x, g):
    return x * lax.rsqrt(jnp.mean(x * x, axis=-1, keepdims=True) + EPS) * g


def _dot(a, b):
    return jnp.dot(a, b, preferred_element_type=F32)


def _dot_nt(a, b):
    return lax.dot_general(a, b, (((1,), (1,)), ((), ())), preferred_element_type=F32)


def _dot_tn(a, b):
    return lax.dot_general(a, b, (((0,), (0,)), ((), ())), preferred_element_type=F32)


def _split_hi_lo(x):
    hi = x.astype(BF16)
    lo = (x - hi.astype(F32)).astype(BF16)
    return hi, lo


def _ada_kernel(c_ref, w_ref, b_ref, o_ref):
    c = c_ref[...]
    a = c * jax.nn.sigmoid(c)
    a_hi, a_lo = _split_hi_lo(a)
    w_hi, w_lo = _split_hi_lo(w_ref[...])
    o_ref[...] = _dot(a_hi, w_hi) + _dot(a_hi, w_lo) + _dot(a_lo, w_hi) + b_ref[...]


def _ada(cc, w_ada, b_ada):
    rows, tn = cc.shape[0], 1536
    n = w_ada.shape[1]
    return pl.pallas_call(
        _ada_kernel,
        grid=(n // tn,),
        in_specs=[pl.BlockSpec((rows, D_MODEL), lambda j: (0, 0)),
                  pl.BlockSpec((D_MODEL, tn), lambda j: (0, j)),
                  pl.BlockSpec((1, tn), lambda j: (0, j))],
        out_specs=pl.BlockSpec((rows, tn), lambda j: (0, j)),
        out_shape=jax.ShapeDtypeStruct((rows, n), F32),
        compiler_params=_cparams(("arbitrary",)),
        name="ada",
    )(cc, w_ada, b_ada)


def _rope_tile(blk, cos, sin, even):
    rot = jnp.where(even, -pltpu.roll(blk, LANES - 1, 1), pltpu.roll(blk, 1, 1))
    return blk * cos + rot * sin


def _run_staggered(stages):
    pending = stages[0][0]()
    for i, (_, consume) in enumerate(stages):
        current = pending
        if i + 1 < len(stages):
            pending = stages[i + 1][0]()
        consume(current)


def _store_kv(kv, kr, k_ref, v_ref, rs=slice(None)):
    one_lane = ((lax.broadcasted_iota(jnp.int32, (1, QK_PAD_W), 1) & (HEAD_PAD - 1)) == V_HEAD).astype(F32)
    for hh in range(MLA_HEADS):
        sl = slice(hh * HEAD_PAD, (hh + 1) * HEAD_PAD)
        k_ref[0, rs, sl] = (kv[:, sl] + kr).astype(BF16)
        vsl = slice(QK_PAD_W + hh * HEAD_PAD, QK_PAD_W + (hh + 1) * HEAD_PAD)
        v_ref[0, sl, rs] = (kv[:, vsl] + one_lane[:, sl]).T.astype(BF16)


N_INPROJ_IN = 11
N_INPROJ_OUT = 10


def _inproj_kernel(*refs, rope, cache_out, n_blocks, with_cache):
    if with_cache:
        step = pl.program_id(1)
        n_in = N_INPROJ_IN + 2 * rope
        ckvc_ref, krc_ref = refs[n_in:n_in + 2]
        refs = refs[:n_in] + refs[n_in + 2:]
        k_ref, v_ref, wukv_ref = refs[n_in + 1], refs[n_in + 2], refs[9]

        @pl.when(step >= n_blocks)
        def _():
            _store_kv(_dot(ckvc_ref[0].astype(BF16), wukv_ref[...]), krc_ref[0], k_ref, v_ref)

        pl.when(step < n_blocks)(functools.partial(_inproj_tile, refs, rope, cache_out))
    else:
        _inproj_tile(refs, rope, cache_out)


def _inproj_tile(refs, rope, cache_out):
    (x_ref, sh_ref, sc_ref, gmix_ref, wlat_ref, wmix_ref, gq_ref, wuq_ref, gkv_ref, wukv_ref, lbl_ref) = refs[:N_INPROJ_IN]
    refs = refs[N_INPROJ_IN:]
    if rope:
        cos_ref, sin_ref = refs[:2]
        refs = refs[2:]
    (q_ref, k_ref, v_ref, qh_ref, kf_ref, lff_ref, kb_ref, lfb_ref, vh_ref, sg_ref) = refs[:N_INPROJ_OUT]
    refs = refs[N_INPROJ_OUT:]
    tm = x_ref.shape[1]
    n_parts = 2 if tm >= 512 else 1
    parts = [slice(i * tm // n_parts, (i + 1) * tm // n_parts) for i in range(n_parts)]

    def stages_for(rs):
        hb = (_rms(x_ref[0, rs], gmix_ref[...]) * (1.0 + sc_ref[0]) + sh_ref[0]).astype(BF16)
        if rope:
            cos, sin = cos_ref[rs], sin_ref[rs]
            even = (lax.broadcasted_iota(jnp.int32, cos.shape, 1) & 1) == 0
        kept = {}

        def use_latents(y):
            kept["qn"] = _rms(y[:, :Q_LORA], gq_ref[...]).astype(BF16)
            kept["ckv"] = _rms(y[:, Q_LORA:_LAT_W], gkv_ref[...])
            kr = y[:, _LAT_W:]
            if cache_out:
                refs[0][0, rs] = kept["ckv"]
                refs[1][0, rs] = kr[:, :QK_ROPE]
            kept["kr"] = _rope_tile(kr, cos, sin, even) if rope else kr

        def use_q(q):
            q = q * (MLA_SCALE * LOG2_E)
            for hh in range(MLA_HEADS):
                sl = slice(hh * HEAD_PAD, (hh + 1) * HEAD_PAD)
                blk = q[:, sl]
                if rope:
                    blk = _rope_tile(blk, cos, sin, even)
                q_ref[0, sl, rs] = blk.T.astype(BF16)

        def use_hq(hq):
            qh_ref[0, rs] = hq * jax.nn.sigmoid(hq) * (HG_DK ** -0.5)

        def use_forget(d, k_out, lf_out):
            def use(z):
                l0, l1 = lbl_ref[0, d:d + 1], lbl_ref[1, d:d + 1]
                lmax = jnp.maximum(l0, l1)
                e0, e1 = jnp.exp(l0 - lmax), jnp.exp(l1 - lmax)
                lbd = e0 / (e0 + e1)
                f = lbd + (1.0 - lbd) * jax.nn.sigmoid(z)
                k_out[0, rs] = 1.0 - f
                lf_out[0, rs] = jnp.log2(f)
            return use

        def use_hi(y):
            vh_ref[0, rs] = y.astype(vh_ref.dtype)

        def use_hg(hg):
            sg_ref[0, rs] = (hg * jax.nn.sigmoid(hg)).astype(sg_ref.dtype)

        def mix(i):
            return lambda: _dot(hb, wmix_ref[:, _MIX_OFFS[i]:_MIX_OFFS[i + 1]])

        return [
            (lambda: _dot(hb, wlat_ref[...]), use_latents),
            (mix(0), use_hq),
            (lambda: _dot(kept["qn"], wuq_ref[...]), use_q),
            (lambda: _dot(kept["ckv"].astype(BF16), wukv_ref[...]),
             lambda kv: _store_kv(kv, kept["kr"], k_ref, v_ref, rs)),
            (mix(1), use_forget(0, kf_ref, lff_ref)),
            (mix(2), use_forget(1, kb_ref, lfb_ref)),
            (mix(3), use_hi),
            (mix(4), use_hg),
        ]

    per_part = [stages_for(rs) for rs in parts]
    _run_staggered([stage for group in zip(*per_part) for stage in group])


def _const_spec(shape):
    return pl.BlockSpec(shape, lambda *_: (0,) * len(shape), pipeline_mode=pl.Buffered(1))


def _inproj(x, mods, mod_row, wts, rope_tabs, cache_out, tm, cache):
    bsz, n, _ = x.shape
    rope = rope_tabs is not None
    (w_lat, w_mix), g_mix, g_q, w_uq_p, g_kv, w_ukv_p, lb_logits = wts
    n_blocks = n // tm
    past = 0 if cache is None else cache[0].shape[1]
    assert n_blocks * tm == n and past % tm == 0
    kv_rows = n + past

    def own(i):
        return i if cache is None else jnp.minimum(i, n_blocks - 1)

    def tok(width):
        return pl.BlockSpec((1, tm, width), lambda b, i: (b, own(i), 0))

    def mod(col):
        return pl.BlockSpec((1, 1, D_MODEL), lambda b, i: (mod_row(b), 0, col))

    in_specs = [tok(D_MODEL), mod(0), mod(1), _const_spec((1, D_MODEL)), _const_spec(w_lat.shape),
                _const_spec(w_mix.shape),
                _const_spec((1, Q_LORA)), _const_spec(w_uq_p.shape), _const_spec((1, KV_LORA)),
                _const_spec(w_ukv_p.shape), _const_spec(lb_logits.shape)]
    args = [x, mods, mods, g_mix, w_lat, w_mix, g_q, w_uq_p, g_kv, w_ukv_p, lb_logits]
    assert len(args) == N_INPROJ_IN
    if rope:
        in_specs += [pl.BlockSpec((tm, LANES), lambda b, i: (own(i), 0))] * 2
        args += list(rope_tabs)
    if cache is not None:
        in_specs += [pl.BlockSpec((1, tm, w), lambda b, i: (b, jnp.maximum(i - n_blocks, 0), 0))
                     for w in (KV_LORA, LANES)]
        args += list(cache)

    widths = [(QK_PAD_W, BF16)] * 3 + [(HG_KW, F32)] * 5 + [(HG_W, BF16)] * 2
    assert len(widths) == N_INPROJ_OUT
    if cache_out:
        widths += [(KV_LORA, F32), (QK_ROPE, F32)]
    out_specs = [tok(w) for w, _ in widths]
    out_shape = [jax.ShapeDtypeStruct((bsz, n, w), dt) for w, dt in widths]
    out_specs[0] = pl.BlockSpec((1, QK_PAD_W, tm), lambda b, i: (b, 0, own(i)))
    out_shape[0] = jax.ShapeDtypeStruct((bsz, QK_PAD_W, n), BF16)
    out_specs[1] = pl.BlockSpec((1, tm, QK_PAD_W), lambda b, i: (b, i, 0))
    out_shape[1] = jax.ShapeDtypeStruct((bsz, kv_rows, QK_PAD_W), BF16)
    out_specs[2] = pl.BlockSpec((1, QK_PAD_W, tm), lambda b, i: (b, 0, i))
    out_shape[2] = jax.ShapeDtypeStruct((bsz, QK_PAD_W, kv_rows), BF16)
    return pl.pallas_call(
        functools.partial(_inproj_kernel, rope=rope, cache_out=cache_out, n_blocks=n_blocks,
                          with_cache=cache is not None),
        grid=(bsz, kv_rows // tm),
        in_specs=in_specs, out_specs=out_specs, out_shape=out_shape,
        compiler_params=_cparams(("parallel", "arbitrary")),
        name="inproj_rope" if rope else "inproj",
    )(*args)


def _attn_kernel(q_ref, k_ref, v_ref, o_ref, m_scr, acc_scr):
    j = pl.program_id(2)

    @pl.when(j == 0)
    def _():
        m_scr[...] = jnp.full(m_scr.shape, -jnp.inf, F32)
        acc_scr[...] = jnp.zeros(acc_scr.shape, F32)

    def scores(hh):
        sl = slice(hh * HEAD_PAD, (hh + 1) * HEAD_PAD)
        return _dot(k_ref[0, :, sl], q_ref[0, sl, :])

    ahead = 2
    pending = [scores(hh) for hh in range(ahead)]
    for hh in range(MLA_HEADS):
        sl = slice(hh * HEAD_PAD, (hh + 1) * HEAD_PAD)
        s = pending.pop(0)
        if hh + ahead < MLA_HEADS:
            pending.append(scores(hh + ahead))
        m_prev = m_scr[hh]
        m_cur = jnp.maximum(m_prev, jnp.max(s, axis=0, keepdims=True))
        alpha = jnp.exp2(m_prev - m_cur)
        p = jnp.exp2(s - m_cur[:1])
        m_scr[hh] = m_cur
        acc_scr[hh] = alpha[:1] * acc_scr[hh] + _dot(v_ref[0, sl, :], p.astype(BF16))

    @pl.when(j == pl.num_programs(2) - 1)
    def _():
        def head_out(hh):
            acc = acc_scr[hh]
            return acc[:V_HEAD] / acc[V_HEAD:V_HEAD + 1]

        for pair in range(MLA_HEADS // 2):
            both = jnp.concatenate([head_out(2 * pair), head_out(2 * pair + 1)], axis=0)
            o_ref[0, :, pair * LANES:(pair + 1) * LANES] = both.T.astype(o_ref.dtype)


def _attention(q, k, v, bq, bk):
    bsz, _, nq = q.shape
    nk = k.shape[1]
    return pl.pallas_call(
        _attn_kernel,
        grid=(bsz, nq // bq, nk // bk),
        in_specs=[pl.BlockSpec((1, QK_PAD_W, bq), lambda b, i, j: (b, 0, i)),
                  pl.BlockSpec((1, bk, QK_PAD_W), lambda b, i, j: (b, j, 0)),
                  pl.BlockSpec((1, QK_PAD_W, bk), lambda b, i, j: (b, 0, j))],
        out_specs=pl.BlockSpec((1, bq, MLA_W), lambda b, i, j: (b, i, 0)),
        out_shape=jax.ShapeDtypeStruct((bsz, nq, MLA_W), BF16),
        scratch_shapes=[pltpu.VMEM((MLA_HEADS, 8, bq), F32),
                        pltpu.VMEM((MLA_HEADS, HEAD_PAD, bq), F32)],
        compiler_params=_cparams(("parallel", "parallel", "arbitrary")),
        name="attn",
    )(q, k, v)


HG_TILE_LEVELS = 3


def _hgrn_constants(reverse):
    c = HG_CHUNK
    t = np.arange(c)
    mats = [(t[None, :] <= t[:, None])]
    for lvl in range(1, HG_TILE_LEVELS + 1):
        g, half = 1 << lvl, 1 << (lvl - 1)
        p = t % g
        mid = t - p + half - 1
        isq = p >= half
        u = t[None, :]
        mats.append(np.where(isq[:, None], (u > mid[:, None]) & (u <= t[:, None]),
                             (u > t[:, None]) & (u <= mid[:, None])))
    mstack = np.stack(mats).astype(np.float32)
    if reverse:
        mstack = mstack[:, ::-1, ::-1]
    mstack = mstack.reshape(-1, c)
    return jnp.asarray(np.concatenate([mstack, mstack], axis=1), BF16)


def _hgrn_kernel(qf_ref, qb_ref, kf_ref, lff_ref, kb_ref, lfb_ref, vf_ref, vb_ref, mstack_ref, *rest,
                 has_s0, emit_state):
    if has_s0:
        s0_ref, rest = rest[0], rest[1:]
    of_ref, ob_ref = rest[:2]
    rest = rest[2:]
    st_scr = rest[-1]
    c = HG_CHUNK
    step = pl.program_id(1)

    @pl.when(step == 0)
    def _():
        st_scr[...] = s0_ref[0] if has_s0 else jnp.zeros(st_scr.shape, F32)

    dirs = ((qf_ref, kf_ref, lff_ref, vf_ref, of_ref, False), (qb_ref, kb_ref, lfb_ref, vb_ref, ob_ref, True))
    exps = []
    for d, (_, _, lf_ref, _, _, _) in enumerate(dirs):
        lf_hi, lf_lo = _split_hi_lo(lf_ref[0])
        exps.append(_dot(mstack_ref[d], jnp.concatenate([lf_hi, lf_lo], axis=0)))

    units = [_hgrn_pair(q_ref, k_ref, v_ref, o_ref, exps[d], st_scr.at[d], pair, reverse)
             for pair in range(N_PAIRS) for d, (q_ref, k_ref, _, v_ref, o_ref, reverse) in enumerate(dirs)]
    ahead = 2
    for unit in units[:ahead]:
        next(unit)
    for i, unit in enumerate(units):
        if i + ahead < len(units):
            next(units[i + ahead])
        next(unit, None)

    if emit_state:
        sfin_ref = rest[0]

        @pl.when(step == pl.num_programs(1) - 1)
        def _():
            for d in range(2):
                for pair in range(N_PAIRS):
                    s_pair = st_scr[d, pair].T
                    sfin_ref[0, d, 2 * pair] = s_pair[:HG_DK, :HG_DV]
                    sfin_ref[0, d, 2 * pair + 1] = pltpu.roll(s_pair, HG_DV, 1)[HG_DK:, :HG_DV]


def _hgrn_pair(q_ref, k_ref, v_ref, o_ref, e, st_scr, pair, reverse):
    c = HG_CHUNK
    last = 0 if reverse else c - 1
    row = lax.broadcasted_iota(jnp.int32, (c, LANES), 0)
    lane_low = lax.broadcasted_iota(jnp.int32, (c, LANES), 1) < HG_DK
    low_b = jnp.where(lane_low, 1.0, 0.0).astype(BF16)
    high_b = jnp.where(lane_low, 0.0, 1.0).astype(BF16)
    blank_t = jnp.zeros((HG_DK, c), BF16)
    xor = lax.broadcasted_iota(jnp.int32, (c, c), 0) ^ lax.broadcasted_iota(jnp.int32, (c, c), 1)
    diag_block = (lax.broadcasted_iota(jnp.int32, (LANES, LANES), 0) < HG_DV) == (
        lax.broadcasted_iota(jnp.int32, (LANES, LANES), 1) < HG_DK)
    zeros8 = jnp.zeros((8, LANES), F32)

    def q_side_block(b):
        return (b % 2 == 1) != reverse

    sl = slice(pair * LANES, (pair + 1) * LANES)
    q = q_ref[0, :, sl]
    k = k_ref[0, :, sl]
    v = v_ref[0, :, sl]
    cum = e[0:c, sl]
    cum_last = cum[last:last + 1]

    def blank(rows):
        return jnp.concatenate([zeros8] * (rows // 8), axis=0)

    products = []
    for lvl in range(HG_LEVELS, -1, -1):
        half = (1 << lvl) // 2
        if lvl == 0:
            lhs, zk = q.astype(BF16), k
        elif lvl > HG_TILE_LEVELS:
            ys, zs = [], []
            for b in range(0, c // half, 2):
                ref_row = (b + 1) * half if reverse else (b + 1) * half - 1
                r = cum[ref_row:ref_row + 1]
                for bb in (b, b + 1):
                    rs = slice(bb * half, (bb + 1) * half)
                    if q_side_block(bb):
                        ys.append(q[rs] * jnp.exp2(cum[rs] - r))
                        zs.append(blank(half))
                    else:
                        zs.append(k[rs] * jnp.exp2(r - cum[rs]))
            lhs = jnp.concatenate(ys, axis=0).astype(BF16)
            zk = jnp.concatenate(zs, axis=0)
        else:
            z = jnp.exp2(e[lvl * c:(lvl + 1) * c, sl])
            q_row = ((row & half) == 0) if reverse else ((row & half) != 0)
            y = jnp.where(q_row, q, k) * z
            lhs = y.astype(BF16)
            zk = jnp.where(q_row, 0.0, y)
        zt = zk.T.astype(BF16)
        rhs_t = jnp.concatenate([jnp.concatenate([zt[:HG_DK], blank_t], axis=0),
                                 jnp.concatenate([blank_t, zt[HG_DK:]], axis=0)], axis=1)
        products.append(_dot(lhs, rhs_t))
    yield
    a_cat = None
    for lvl, p in zip(range(HG_LEVELS, -1, -1), products):
        half = (1 << lvl) // 2
        if lvl > HG_TILE_LEVELS:
            q_blocks = [bb for bb in range(c // half) if q_side_block(bb)]
            new = []
            for h_idx in range(2):
                p_h = p[:, h_idx * c:(h_idx + 1) * c]
                rows = []
                for bb in range(c // half):
                    rs = slice(bb * half, (bb + 1) * half)
                    if not q_side_block(bb):
                        rows.append(blank(half) if a_cat is None else a_cat[h_idx][rs])
                        continue
                    i = q_blocks.index(bb)
                    p_blk = p_h[i * half:(i + 1) * half]
                    if a_cat is None:
                        rows.append(p_blk)
                    else:
                        rows.append(jnp.where(xor[rs] < (1 << lvl), p_blk, a_cat[h_idx][rs]))
                new.append(jnp.concatenate(rows, axis=0))
            a_cat = tuple(new)
        else:
            same = xor < (1 << lvl)
            a_cat = tuple(jnp.where(same, p[:, h_idx * c:(h_idx + 1) * c], a_cat[h_idx]) for h_idx in range(2))
    a_cat = jnp.concatenate(a_cat, axis=1)
    v_cat = jnp.concatenate([v * low_b, v * high_b], axis=0)
    st = st_scr[pair]
    o = _dot(a_cat.astype(BF16), v_cat) + _dot_nt((q * jnp.exp2(cum)).astype(BF16), st.astype(BF16))
    o_ref[0, :, sl] = o
    kd = (k * jnp.exp2(cum_last - cum)).astype(BF16)
    upd = _dot_tn(v, kd)
    st_scr[pair] = jnp.exp2(cum_last) * st + jnp.where(diag_block, upd, 0.0)


def _hgrn(qh, kf, lff, kb, lfb, vh, s0t, emit_state):
    bsz, n, _ = qh.shape
    c = HG_CHUNK
    nc = n // c
    mstack = jnp.stack([_hgrn_constants(False), _hgrn_constants(True)])
    fwd = pl.BlockSpec((1, c, HG_KW), lambda b, i: (b, i, 0))
    bwd = pl.BlockSpec((1, c, HG_KW), lambda b, i: (b, nc - 1 - i, 0))
    in_specs = [fwd, bwd, fwd, fwd, bwd, bwd, fwd, bwd, _const_spec(mstack.shape)]
    args = [qh, qh, kf, lff, kb, lfb, vh, vh, mstack]
    if s0t is not None:
        in_specs.append(pl.BlockSpec((1, 2, N_PAIRS, LANES, LANES), lambda b, i: (b, 0, 0, 0, 0)))
        args.append(s0t)
    out_specs = [fwd, bwd]
    out_shape = [jax.ShapeDtypeStruct((bsz, n, HG_W), F32)] * 2
    if emit_state:
        out_specs.append(pl.BlockSpec((1, 2, HG_HEADS, HG_DK, HG_DV), lambda b, i: (b, 0, 0, 0, 0)))
        out_shape.append(jax.ShapeDtypeStruct((bsz, 2, HG_HEADS, HG_DK, HG_DV), F32))
    return pl.pallas_call(
        functools.partial(_hgrn_kernel, has_s0=s0t is not None, emit_state=emit_state),
        grid=(bsz, nc),
        in_specs=in_specs, out_specs=out_specs, out_shape=out_shape,
        scratch_shapes=[pltpu.VMEM((2, N_PAIRS, LANES, LANES), F32)],
        compiler_params=_cparams(("parallel", "arbitrary")),
        name="hgrn",
    )(*args)


def _state_to_pairs(s):
    b = s.shape[0]
    st = jnp.swapaxes(s, -1, -2).reshape(b, N_PAIRS, 2, HG_DV, HG_DK)
    eye = jnp.eye(2, dtype=s.dtype)
    out = jnp.einsum('bpavk,ac->bpavck', st, eye)
    return out.reshape(b, N_PAIRS, 2 * HG_DV, 2 * HG_DK)


def _merge_kernel(x_ref, sh_ref, sc_ref, ga_ref, om_ref, of_ref, ob_ref, sg_ref, gmix_ref, wgate_ref, ghg_ref,
                  hmean_ref, wbm_ref, wbh_ref, wout_ref, x1_ref):
    tm = x_ref.shape[1]
    halves = [slice(i * tm // 2, (i + 1) * tm // 2) for i in range(2)]
    o = [of_ref[0, rs] + ob_ref[0, rs] for rs in halves]

    def head_mean_sq(i):
        sq_hi, sq_lo = _split_hi_lo(o[i] * o[i])
        return _dot(jnp.concatenate([sq_hi, sq_lo], axis=1), hmean_ref[...])

    ms = [head_mean_sq(i) for i in range(2)]
    mla = [_dot(om_ref[0, rs], wbm_ref[...]) for rs in halves]
    hb = [(_rms(x_ref[0, rs], gmix_ref[...]) * (1.0 + sc_ref[0]) + sh_ref[0]).astype(BF16) for rs in halves]
    gates = [_dot(hb[i], wgate_ref[...]) for i in range(2)]
    hg = []
    for i, rs in enumerate(halves):
        o_hg = (o[i] * lax.rsqrt(ms[i] + EPS) * ghg_ref[...]) * sg_ref[0, rs]
        hg.append(_dot(o_hg.astype(BF16), wbh_ref[...]))
    out = []
    for i, rs in enumerate(halves):
        merged = (jax.nn.sigmoid(gates[i][:, :D_MODEL]) * mla[i]
                  + jax.nn.sigmoid(gates[i][:, D_MODEL:]) * hg[i])
        out.append(_dot(merged.astype(BF16), wout_ref[...]))
    for i, rs in enumerate(halves):
        x1_ref[0, rs] = x_ref[0, rs] + ga_ref[0] * out[i]


def _merge(x, mods, mod_row, o_mla, o_f, o_b, sg, wts, tm):
    bsz, n, _ = x.shape
    g_mix, w_gate, g_hg, hmean, w_br_mla, w_br_hg, w_out = wts

    def tok(width):
        return pl.BlockSpec((1, tm, width), lambda b, i: (b, i, 0))

    def mod(col):
        return pl.BlockSpec((1, 1, D_MODEL), lambda b, i: (mod_row(b), 0, col))

    return pl.pallas_call(
        _merge_kernel,
        grid=(bsz, n // tm),
        in_specs=[tok(D_MODEL), mod(0), mod(1), mod(2),
                  tok(MLA_W), tok(HG_W), tok(HG_W), tok(HG_W),
                  _const_spec((1, D_MODEL)), _const_spec(w_gate.shape),
                  _const_spec((1, HG_W)), _const_spec(hmean.shape), _const_spec(w_br_mla.shape),
                  _const_spec(w_br_hg.shape), _const_spec(w_out.shape)],
        out_specs=tok(D_MODEL),
        out_shape=jax.ShapeDtypeStruct((bsz, n, D_MODEL), F32),
        compiler_params=_cparams(("parallel", "parallel")),
        name="merge",
    )(x, mods, mods, mods, o_mla, o_f, o_b, sg, g_mix, w_gate, g_hg, hmean, w_br_mla, w_br_hg, w_out)


FFN_CHUNK = 1024


def _ffn_kernel(x_ref, sh_ref, sc_ref, ga_ref, gff_ref, w1_ref, w2_ref, gfin_ref, y_ref):
    tm = x_ref.shape[1]
    halves = [slice(i * tm // 2, (i + 1) * tm // 2) for i in range(2)]
    h = [(_rms(x_ref[0, rs], gff_ref[...]) * (1.0 + sc_ref[0]) + sh_ref[0]).astype(BF16) for rs in halves]

    up = [(i, c0) for c0 in range(0, D_FF, FFN_CHUNK) for i in range(2)]
    acts = [[], []]

    def use_up(i):
        def use(a):
            a = jnp.maximum(a, 0.0)
            acts[i].append((a * a).astype(BF16))
        return use

    def use_down(rs):
        def use(f):
            y_ref[0, rs] = _rms(x_ref[0, rs] + ga_ref[0] * f, gfin_ref[...])
        return use

    stages = [(functools.partial(lambda i, c0: _dot(h[i], w1_ref[:, c0:c0 + FFN_CHUNK]), i, c0), use_up(i))
              for i, c0 in up]
    stages += [(functools.partial(lambda i: _dot(jnp.concatenate(acts[i], axis=1), w2_ref[...]), i), use_down(rs))
               for i, rs in enumerate(halves)]
    _run_staggered(stages)


def _ffn(x1, mods, mod_row, wts, tm):
    bsz, n, _ = x1.shape
    g_ff, w_ff1, w_ff2, g_final = wts

    def mod(col):
        return pl.BlockSpec((1, 1, D_MODEL), lambda b, i: (mod_row(b), 0, col))

    tok = pl.BlockSpec((1, tm, D_MODEL), lambda b, i: (b, i, 0))
    return pl.pallas_call(
        _ffn_kernel,
        grid=(bsz, n // tm),
        in_specs=[tok, mod(3), mod(4), mod(5), _const_spec((1, D_MODEL)),
                  _const_spec(w_ff1.shape), _const_spec(w_ff2.shape), _const_spec((1, D_MODEL))],
        out_specs=tok,
        out_shape=jax.ShapeDtypeStruct((bsz, n, D_MODEL), F32),
        compiler_params=_cparams(("parallel", "parallel")),
        name="ffn",
    )(x1, mods, mods, mods, g_ff, w_ff1, w_ff2, g_final)


def _pad_head_cols(w, widths_in, layout):
    rows = w.shape[0]
    w = w.reshape(rows, MLA_HEADS, sum(width for _, width in widths_in))
    pieces, off = {}, 0
    for name, width in widths_in:
        pieces[name] = w[:, :, off:off + width]
        off += width
    cols = [pieces[name] if name in pieces else jnp.zeros((rows, MLA_HEADS, width), w.dtype) for name, width in layout]
    return jnp.concatenate(cols, axis=-1).reshape(rows, -1)


def _prep_weights(w_in, w_uq, w_ukv):
    w_lat = jnp.pad(w_in[:, :_LAT_W + QK_ROPE].astype(BF16), ((0, 0), (0, LANES - QK_ROPE)))
    mix0 = _LAT_W + QK_ROPE
    w_mix = w_in[:, mix0:mix0 + _MIX_OFFS[-1]].astype(BF16)
    w_gate = w_in[:, mix0 + _MIX_OFFS[-1]:].astype(BF16)
    w_in_p = (w_lat, w_mix, w_gate)
    w_uq_p = _pad_head_cols(w_uq, (("nope", QK_NOPE), ("rope", QK_ROPE)),
                            (("rope", QK_ROPE), ("zero", HEAD_PAD - QK_ROPE - QK_NOPE), ("nope", QK_NOPE))).astype(BF16)
    kv_in = (("nope", QK_NOPE), ("v", V_HEAD))
    k_cols = _pad_head_cols(w_ukv, kv_in, (("zero", HEAD_PAD - QK_NOPE), ("nope", QK_NOPE)))
    v_cols = _pad_head_cols(w_ukv, kv_in, (("v", V_HEAD), ("zero", HEAD_PAD - V_HEAD)))
    w_ukv_p = jnp.concatenate([k_cols, v_cols], axis=1).astype(BF16)
    return w_in_p, w_uq_p, w_ukv_p


def _rope_tables(n):
    rows = n // GRID_W
    row = jnp.repeat(jnp.arange(rows, dtype=F32), GRID_W)
    col = jnp.tile(jnp.arange(GRID_W, dtype=F32), rows)
    half = QK_ROPE // 2
    inv = ROPE_BASE ** (-jnp.arange(0, half, 2, dtype=F32) / half)
    ang = jnp.concatenate([row[:, None] * inv, col[:, None] * inv], axis=-1)
    cos = jnp.repeat(jnp.cos(ang), 2, axis=-1)
    sin = jnp.repeat(jnp.sin(ang), 2, axis=-1)
    cos = jnp.concatenate([cos, jnp.ones((n, LANES - QK_ROPE), F32)], axis=-1)
    sin = jnp.concatenate([sin, jnp.zeros((n, LANES - QK_ROPE), F32)], axis=-1)
    return cos, sin


def _trunk(x, mods, mod_row, wts, rope_tabs, ctx):
    inproj_w, merge_w, ffn_w = wts
    bsz, n, _ = x.shape
    tm = min(n, 512)
    if ctx is None:
        cache, past = None, 0
        s0 = None
    else:
        ckv_c, krope_c, state_c = ctx
        cache, past = (ckv_c, jnp.pad(krope_c, ((0, 0), (0, 0), (0, LANES - QK_ROPE)))), ckv_c.shape[1]
        s0 = jnp.stack([_state_to_pairs(state_c[:, 0]), _state_to_pairs(state_c[:, 1])], axis=1)
    outs = _inproj(x, mods, mod_row, inproj_w, rope_tabs, ctx is None, tm, cache)
    q, k, v, qh, kf, lff, kb, lfb, vh, sg = outs[:N_INPROJ_OUT]
    if ctx is None:
        ckv, krope = outs[N_INPROJ_OUT:]
    bk = next(b for b in (768, 512, 256, 128) if (n + past) % b == 0)
    o_mla = _attention(q, k, v, min(n, 1024), bk)
    o_f, o_b, *state = _hgrn(qh, kf, lff, kb, lfb, vh, s0, ctx is None)
    per_token = (x, o_mla, o_f, o_b, sg)
    if ctx is None:
        per_token = tuple(a.reshape(1, bsz * n, a.shape[-1]) for a in per_token)
    rows = per_token[0].shape[1]
    x1 = _merge(*per_token[:1], mods, mod_row, *per_token[1:], merge_w, min(rows, 512))
    y = _ffn(x1, mods, mod_row, ffn_w, min(512, rows)).reshape(bsz, n, D_MODEL)
    if ctx is None:
        return y, (ckv, krope, state[0])
    return y, None


def kernel(x_prompt, x_sample, cache_ckv, cache_krope, state_hgrn, c, c_ctx, w_ada, b_ada, g_norm_mix, g_norm_ff, w_in, g_q_norm, w_uq, g_kv_norm, w_ukv, g_hg_norm, hg_lb_logits, w_br_mla, w_br_hg, w_out, w_ff1, w_ff2, g_final):
    assert w_in.shape[0] == 1, "single-layer trunk"
    dec_b = c.shape[0]
    cc = jnp.concatenate([c, c_ctx[None, :], jnp.zeros((8 - dec_b - 1, D_MODEL), F32)], axis=0)
    mods = _ada(cc, w_ada[0], b_ada[0][None, :]).reshape(8, 1, 6 * D_MODEL)

    (w_lat, w_mix, w_gate), w_uq_p, w_ukv_p = _prep_weights(w_in[0], w_uq[0], w_ukv[0])
    g_mix = g_norm_mix[0][None]
    inproj_w = ((w_lat, w_mix), g_mix, g_q_norm[0][None], w_uq_p, g_kv_norm[0][None], w_ukv_p, hg_lb_logits)
    head_id = np.arange(HG_W) // HG_DV
    hmean = (head_id[:, None] == head_id[None, :]).astype(np.float32) / HG_DV
    hmean = jnp.asarray(np.concatenate([hmean, hmean], axis=0), BF16)
    merge_w = (g_mix, w_gate, g_hg_norm[0][None], hmean, w_br_mla[0].astype(BF16), w_br_hg[0].astype(BF16),
               w_out[0].astype(BF16))
    ffn_w = (g_norm_ff[0][None], w_ff1[0].astype(BF16), w_ff2[0].astype(BF16), g_final[None])
    wts = (inproj_w, merge_w, ffn_w)

    y_prompt, (ckv, krope, state) = _trunk(x_prompt, mods, lambda b: dec_b, wts, None, None)
    rope_tabs = _rope_tables(x_sample.shape[1])
    y_sample, _ = _trunk(x_sample, mods, lambda b: b, wts, rope_tabs,
                         (cache_ckv[:, 0], cache_krope[:, 0], state_hgrn[:, 0]))
    return (y_prompt, y_sample, ckv[:, None], krope[:, None], state[:, None])
```

```python
import functools

import numpy as np
import jax
import jax.numpy as jnp
from jax import lax
from jax.experimental import pallas as pl
from jax.experimental.pallas import tpu as pltpu

D_MODEL = 1024
GRID_W = 64
MLA_HEADS = 8
Q_LORA = 384
KV_LORA = 256
QK_NOPE = 64
QK_ROPE = 32
V_HEAD = 64
MLA_W = MLA_HEADS * V_HEAD
MLA_SCALE = (QK_NOPE + QK_ROPE) ** -0.5
LOG2_E = 1.4426950408889634
HG_HEADS = 8
HG_DK = 64
HG_DV = 64
HG_KW = HG_HEADS * HG_DK
HG_W = HG_HEADS * HG_DV
D_FF = 4 * D_MODEL
ROPE_BASE = 10000.0
EPS = 1e-6

LANES = 128
SUBLANES = 8
HEAD_PAD = LANES
QK_PAD_W = MLA_HEADS * HEAD_PAD
N_PAIRS = HG_HEADS // 2
VMEM_LIMIT = 56 * 1024 * 1024

_LAT_W = Q_LORA + KV_LORA
_MIX_SIZES = (HG_KW, HG_KW, HG_KW, HG_W, HG_W)
_MIX_OFFS = tuple(int(o) for o in np.cumsum((0,) + _MIX_SIZES))

HG_CHUNK = 128
HG_LEVELS = 7

TOKEN_TILE = 512
ATTN_Q_TILE = 1024
ATTN_KV_TILES = (768, 512, 256, 128)
ADA_COL_TILE = 1536

F32 = jnp.float32
BF16 = jnp.bfloat16


def _cparams(sem):
    return pltpu.CompilerParams(dimension_semantics=sem, vmem_limit_bytes=VMEM_LIMIT)


def _rms(x, g):
    return x * lax.rsqrt(jnp.mean(x * x, axis=-1, keepdims=True) + EPS) * g


def _dot(a, b):
    return jnp.dot(a, b, preferred_element_type=F32)


def _dot_nt(a, b):
    return lax.dot_general(a, b, (((1,), (1,)), ((), ())), preferred_element_type=F32)


def _dot_tn(a, b):
    return lax.dot_general(a, b, (((0,), (0,)), ((), ())), preferred_element_type=F32)


def _split_hi_lo(x):
    hi = x.astype(BF16)
    lo = (x - hi.astype(F32)).astype(BF16)
    return hi, lo


def _ada_kernel(c_ref, w_ref, b_ref, o_ref):
    c = c_ref[...]
    a = c * jax.nn.sigmoid(c)
    a_hi, a_lo = _split_hi_lo(a)
    w_hi, w_lo = _split_hi_lo(w_ref[...])
    o_ref[...] = _dot(a_hi, w_hi) + _dot(a_hi, w_lo) + _dot(a_lo, w_hi) + b_ref[...]


def _ada(cc, w_ada, b_ada):
    rows, tn = cc.shape[0], ADA_COL_TILE
    n = w_ada.shape[1]
    return pl.pallas_call(
        _ada_kernel,
        grid=(n // tn,),
        in_specs=[pl.BlockSpec((rows, D_MODEL), lambda j: (0, 0)),
                  pl.BlockSpec((D_MODEL, tn), lambda j: (0, j)),
                  pl.BlockSpec((1, tn), lambda j: (0, j))],
        out_specs=pl.BlockSpec((rows, tn), lambda j: (0, j)),
        out_shape=jax.ShapeDtypeStruct((rows, n), F32),
        compiler_params=_cparams(("arbitrary",)),
        name="ada",
    )(cc, w_ada, b_ada)


def _rope_tile(blk, cos, sin, even):
    rot = jnp.where(even, -pltpu.roll(blk, LANES - 1, 1), pltpu.roll(blk, 1, 1))
    return blk * cos + rot * sin


def _run_staggered(stages):
    pending = stages[0][0]()
    for i, (_, consume) in enumerate(stages):
        current = pending
        if i + 1 < len(stages):
            pending = stages[i + 1][0]()
        consume(current)


def _store_kv(kv, kr, k_ref, v_ref, rs=slice(None)):
    one_lane = ((lax.broadcasted_iota(jnp.int32, (1, QK_PAD_W), 1) & (HEAD_PAD - 1)) == V_HEAD).astype(F32)
    for hh in range(MLA_HEADS):
        sl = slice(hh * HEAD_PAD, (hh + 1) * HEAD_PAD)
        k_ref[0, rs, sl] = (kv[:, sl] + kr).astype(BF16)
        vsl = slice(QK_PAD_W + hh * HEAD_PAD, QK_PAD_W + (hh + 1) * HEAD_PAD)
        v_ref[0, sl, rs] = (kv[:, vsl] + one_lane[:, sl]).T.astype(BF16)


N_INPROJ_IN = 11
N_INPROJ_OUT = 10


def _inproj_kernel(*refs, rope, cache_out, n_blocks, with_cache):
    if with_cache:
        step = pl.program_id(1)
        n_in = N_INPROJ_IN + 2 * rope
        ckvc_ref, krc_ref = refs[n_in:n_in + 2]
        refs = refs[:n_in] + refs[n_in + 2:]
        k_ref, v_ref, wukv_ref = refs[n_in + 1], refs[n_in + 2], refs[9]

        @pl.when(step >= n_blocks)
        def _():
            _store_kv(_dot(ckvc_ref[0].astype(BF16), wukv_ref[...]), krc_ref[0], k_ref, v_ref)

        pl.when(step < n_blocks)(functools.partial(_inproj_tile, refs, rope, cache_out))
    else:
        _inproj_tile(refs, rope, cache_out)


def _inproj_tile(refs, rope, cache_out):
    (x_ref, sh_ref, sc_ref, gmix_ref, wlat_ref, wmix_ref, gq_ref, wuq_ref, gkv_ref, wukv_ref, lbl_ref) = refs[:N_INPROJ_IN]
    refs = refs[N_INPROJ_IN:]
    if rope:
        cos_ref, sin_ref = refs[:2]
        refs = refs[2:]
    (q_ref, k_ref, v_ref, qh_ref, kf_ref, lff_ref, kb_ref, lfb_ref, vh_ref, sg_ref) = refs[:N_INPROJ_OUT]
    refs = refs[N_INPROJ_OUT:]
    tm = x_ref.shape[1]
    n_parts = 2 if tm == TOKEN_TILE else 1
    parts = [slice(i * tm // n_parts, (i + 1) * tm // n_parts) for i in range(n_parts)]

    def stages_for(rs):
        hb = (_rms(x_ref[0, rs], gmix_ref[...]) * (1.0 + sc_ref[0]) + sh_ref[0]).astype(BF16)
        if rope:
            cos, sin = cos_ref[rs], sin_ref[rs]
            even = (lax.broadcasted_iota(jnp.int32, cos.shape, 1) & 1) == 0
        kept = {}

        def use_latents(y):
            kept["qn"] = _rms(y[:, :Q_LORA], gq_ref[...]).astype(BF16)
            kept["ckv"] = _rms(y[:, Q_LORA:_LAT_W], gkv_ref[...])
            kr = y[:, _LAT_W:]
            if cache_out:
                refs[0][0, rs] = kept["ckv"]
                refs[1][0, rs] = kr[:, :QK_ROPE]
            kept["kr"] = _rope_tile(kr, cos, sin, even) if rope else kr

        def use_q(q):
            q = q * (MLA_SCALE * LOG2_E)
            for hh in range(MLA_HEADS):
                sl = slice(hh * HEAD_PAD, (hh + 1) * HEAD_PAD)
                blk = q[:, sl]
                if rope:
                    blk = _rope_tile(blk, cos, sin, even)
                q_ref[0, sl, rs] = blk.T.astype(BF16)

        def use_hq(hq):
            qh_ref[0, rs] = hq * jax.nn.sigmoid(hq) * (HG_DK ** -0.5)

        def use_forget(d, k_out, lf_out):
            def use(z):
                l0, l1 = lbl_ref[0, d:d + 1], lbl_ref[1, d:d + 1]
                lmax = jnp.maximum(l0, l1)
                e0, e1 = jnp.exp(l0 - lmax), jnp.exp(l1 - lmax)
                lbd = e0 / (e0 + e1)
                f = lbd + (1.0 - lbd) * jax.nn.sigmoid(z)
                k_out[0, rs] = 1.0 - f
                lf_out[0, rs] = jnp.log2(f)
            return use

        def use_hi(y):
            vh_ref[0, rs] = y.astype(vh_ref.dtype)

        def use_hg(hg):
            sg_ref[0, rs] = (hg * jax.nn.sigmoid(hg)).astype(sg_ref.dtype)

        def mix(i):
            return lambda: _dot(hb, wmix_ref[:, _MIX_OFFS[i]:_MIX_OFFS[i + 1]])

        return [
            (lambda: _dot(hb, wlat_ref[...]), use_latents),
            (mix(0), use_hq),
            (lambda: _dot(kept["qn"], wuq_ref[...]), use_q),
            (lambda: _dot(kept["ckv"].astype(BF16), wukv_ref[...]),
             lambda kv: _store_kv(kv, kept["kr"], k_ref, v_ref, rs)),
            (mix(1), use_forget(0, kf_ref, lff_ref)),
            (mix(2), use_forget(1, kb_ref, lfb_ref)),
            (mix(3), use_hi),
            (mix(4), use_hg),
        ]

    per_part = [stages_for(rs) for rs in parts]
    _run_staggered([stage for group in zip(*per_part) for stage in group])


def _const_spec(shape):
    return pl.BlockSpec(shape, lambda *_: (0,) * len(shape), pipeline_mode=pl.Buffered(1))


def _inproj(x, mods, mod_row, wts, rope_tabs, cache_out, tm, cache):
    bsz, n, _ = x.shape
    rope = rope_tabs is not None
    (w_lat, w_mix), g_mix, g_q, w_uq_p, g_kv, w_ukv_p, lb_logits = wts
    n_blocks = n // tm
    past = 0 if cache is None else cache[0].shape[1]
    assert n_blocks * tm == n and past % tm == 0
    kv_rows = n + past

    def own(i):
        return i if cache is None else jnp.minimum(i, n_blocks - 1)

    def tok(width):
        return pl.BlockSpec((1, tm, width), lambda b, i: (b, own(i), 0))

    def mod(col):
        return pl.BlockSpec((1, 1, D_MODEL), lambda b, i: (mod_row(b), 0, col))

    in_specs = [tok(D_MODEL), mod(0), mod(1), _const_spec((1, D_MODEL)), _const_spec(w_lat.shape),
                _const_spec(w_mix.shape),
                _const_spec((1, Q_LORA)), _const_spec(w_uq_p.shape), _const_spec((1, KV_LORA)),
                _const_spec(w_ukv_p.shape), _const_spec(lb_logits.shape)]
    args = [x, mods, mods, g_mix, w_lat, w_mix, g_q, w_uq_p, g_kv, w_ukv_p, lb_logits]
    assert len(args) == N_INPROJ_IN
    if rope:
        in_specs += [pl.BlockSpec((tm, LANES), lambda b, i: (own(i), 0))] * 2
        args += list(rope_tabs)
    if cache is not None:
        in_specs += [pl.BlockSpec((1, tm, w), lambda b, i: (b, jnp.maximum(i - n_blocks, 0), 0))
                     for w in (KV_LORA, LANES)]
        args += list(cache)

    widths = [(QK_PAD_W, BF16)] * 3 + [(HG_KW, F32)] * 5 + [(HG_W, BF16)] * 2
    assert len(widths) == N_INPROJ_OUT
    if cache_out:
        widths += [(KV_LORA, F32), (QK_ROPE, F32)]
    out_specs = [tok(w) for w, _ in widths]
    out_shape = [jax.ShapeDtypeStruct((bsz, n, w), dt) for w, dt in widths]
    out_specs[0] = pl.BlockSpec((1, QK_PAD_W, tm), lambda b, i: (b, 0, own(i)))
    out_shape[0] = jax.ShapeDtypeStruct((bsz, QK_PAD_W, n), BF16)
    out_specs[1] = pl.BlockSpec((1, tm, QK_PAD_W), lambda b, i: (b, i, 0))
    out_shape[1] = jax.ShapeDtypeStruct((bsz, kv_rows, QK_PAD_W), BF16)
    out_specs[2] = pl.BlockSpec((1, QK_PAD_W, tm), lambda b, i: (b, 0, i))
    out_shape[2] = jax.ShapeDtypeStruct((bsz, QK_PAD_W, kv_rows), BF16)
    return pl.pallas_call(
        functools.partial(_inproj_kernel, rope=rope, cache_out=cache_out, n_blocks=n_blocks,
                          with_cache=cache is not None),
        grid=(bsz, kv_rows // tm),
        in_specs=in_specs, out_specs=out_specs, out_shape=out_shape,
        compiler_params=_cparams(("parallel", "arbitrary")),
        name="inproj_rope" if rope else "inproj",
    )(*args)


def _attn_kernel(q_ref, k_ref, v_ref, o_ref, m_scr, acc_scr):
    j = pl.program_id(2)

    @pl.when(j == 0)
    def _():
        m_scr[...] = jnp.full(m_scr.shape, -jnp.inf, F32)
        acc_scr[...] = jnp.zeros(acc_scr.shape, F32)

    def scores(hh):
        sl = slice(hh * HEAD_PAD, (hh + 1) * HEAD_PAD)
        return _dot(k_ref[0, :, sl], q_ref[0, sl, :])

    ahead = 2
    pending = [scores(hh) for hh in range(ahead)]
    for hh in range(MLA_HEADS):
        sl = slice(hh * HEAD_PAD, (hh + 1) * HEAD_PAD)
        s = pending.pop(0)
        if hh + ahead < MLA_HEADS:
            pending.append(scores(hh + ahead))
        bq = s.shape[1]
        for cs in (slice(0, bq // 2), slice(bq // 2, bq)):
            s_c = s[:, cs]
            m_prev = m_scr[hh, :, cs]
            m_cur = jnp.maximum(m_prev, jnp.max(s_c, axis=0, keepdims=True))
            alpha = jnp.exp2(m_prev - m_cur)
            p = jnp.exp2(s_c - m_cur[:1])
            m_scr[hh, :, cs] = m_cur
            acc_scr[hh, :, cs] = alpha[:1] * acc_scr[hh, :, cs] + _dot(v_ref[0, sl, :], p.astype(BF16))

    @pl.when(j == pl.num_programs(2) - 1)
    def _():
        def head_out(hh):
            acc = acc_scr[hh]
            return acc[:V_HEAD] / acc[V_HEAD:V_HEAD + 1]

        for pair in range(MLA_HEADS // 2):
            both = jnp.concatenate([head_out(2 * pair), head_out(2 * pair + 1)], axis=0)
            o_ref[0, :, pair * LANES:(pair + 1) * LANES] = both.T.astype(o_ref.dtype)


def _attention(q, k, v, bq, bk):
    bsz, _, nq = q.shape
    nk = k.shape[1]
    return pl.pallas_call(
        _attn_kernel,
        grid=(bsz, nq // bq, nk // bk),
        in_specs=[pl.BlockSpec((1, QK_PAD_W, bq), lambda b, i, j: (b, 0, i)),
                  pl.BlockSpec((1, bk, QK_PAD_W), lambda b, i, j: (b, j, 0)),
                  pl.BlockSpec((1, QK_PAD_W, bk), lambda b, i, j: (b, 0, j))],
        out_specs=pl.BlockSpec((1, bq, MLA_W), lambda b, i, j: (b, i, 0)),
        out_shape=jax.ShapeDtypeStruct((bsz, nq, MLA_W), BF16),
        scratch_shapes=[pltpu.VMEM((MLA_HEADS, SUBLANES, bq), F32),
                        pltpu.VMEM((MLA_HEADS, HEAD_PAD, bq), F32)],
        compiler_params=_cparams(("parallel", "parallel", "arbitrary")),
        name="attn",
    )(q, k, v)


HG_TILE_LEVELS = 3


def _hgrn_constants(reverse):
    c = HG_CHUNK
    t = np.arange(c)
    mats = [(t[None, :] <= t[:, None])]
    for lvl in range(1, HG_TILE_LEVELS + 1):
        g, half = 1 << lvl, 1 << (lvl - 1)
        p = t % g
        mid = t - p + half - 1
        isq = p >= half
        u = t[None, :]
        mats.append(np.where(isq[:, None], (u > mid[:, None]) & (u <= t[:, None]),
                             (u > t[:, None]) & (u <= mid[:, None])))
    mstack = np.stack(mats).astype(np.float32)
    if reverse:
        mstack = mstack[:, ::-1, ::-1]
    mstack = mstack.reshape(-1, c)
    return jnp.asarray(np.concatenate([mstack, mstack], axis=1), BF16)


def _hgrn_kernel(qf_ref, qb_ref, kf_ref, lff_ref, kb_ref, lfb_ref, vf_ref, vb_ref, mstack_ref, *rest,
                 has_s0, emit_state):
    if has_s0:
        s0_ref, rest = rest[0], rest[1:]
    of_ref, ob_ref = rest[:2]
    rest = rest[2:]
    st_scr = rest[-1]
    c = HG_CHUNK
    step = pl.program_id(1)

    @pl.when(step == 0)
    def _():
        st_scr[...] = s0_ref[0] if has_s0 else jnp.zeros(st_scr.shape, F32)

    dirs = ((qf_ref, kf_ref, lff_ref, vf_ref, of_ref, False), (qb_ref, kb_ref, lfb_ref, vb_ref, ob_ref, True))
    exps = []
    for d, (_, _, lf_ref, _, _, _) in enumerate(dirs):
        lf_hi, lf_lo = _split_hi_lo(lf_ref[0])
        exps.append(_dot(mstack_ref[d], jnp.concatenate([lf_hi, lf_lo], axis=0)))

    units = [_hgrn_pair(q_ref, k_ref, v_ref, o_ref, exps[d], st_scr.at[d], pair, reverse)
             for pair in range(N_PAIRS) for d, (q_ref, k_ref, _, v_ref, o_ref, reverse) in enumerate(dirs)]
    ahead = 2
    for unit in units[:ahead]:
        next(unit)
    for i, unit in enumerate(units):
        if i + ahead < len(units):
            next(units[i + ahead])
        next(unit, None)

    if emit_state:
        sfin_ref = rest[0]

        @pl.when(step == pl.num_programs(1) - 1)
        def _():
            for d in range(2):
                for pair in range(N_PAIRS):
                    s_pair = st_scr[d, pair].T
                    sfin_ref[0, d, 2 * pair] = s_pair[:HG_DK, :HG_DV]
                    sfin_ref[0, d, 2 * pair + 1] = pltpu.roll(s_pair, HG_DV, 1)[HG_DK:, :HG_DV]


def _hgrn_pair(q_ref, k_ref, v_ref, o_ref, e, st_scr, pair, reverse):
    c = HG_CHUNK
    last = 0 if reverse else c - 1
    row = lax.broadcasted_iota(jnp.int32, (c, LANES), 0)
    lane_low = lax.broadcasted_iota(jnp.int32, (c, LANES), 1) < HG_DK
    low_b = jnp.where(lane_low, 1.0, 0.0).astype(BF16)
    high_b = jnp.where(lane_low, 0.0, 1.0).astype(BF16)
    blank_t = jnp.zeros((HG_DK, c), BF16)
    xor = lax.broadcasted_iota(jnp.int32, (c, c), 0) ^ lax.broadcasted_iota(jnp.int32, (c, c), 1)
    diag_block = (lax.broadcasted_iota(jnp.int32, (LANES, LANES), 0) < HG_DV) == (
        lax.broadcasted_iota(jnp.int32, (LANES, LANES), 1) < HG_DK)
    zeros8 = jnp.zeros((SUBLANES, LANES), F32)

    def q_side_block(b):
        return (b % 2 == 1) != reverse

    sl = slice(pair * LANES, (pair + 1) * LANES)
    q = q_ref[0, :, sl]
    k = k_ref[0, :, sl]
    v = v_ref[0, :, sl]
    cum = e[0:c, sl]
    cum_last = cum[last:last + 1]

    def blank(rows):
        return jnp.concatenate([zeros8] * (rows // SUBLANES), axis=0)

    products = []
    for lvl in range(HG_LEVELS, -1, -1):
        half = (1 << lvl) // 2
        if lvl == 0:
            lhs, zk = q.astype(BF16), k
        elif lvl > HG_TILE_LEVELS:
            ys, zs = [], []
            for b in range(0, c // half, 2):
                ref_row = (b + 1) * half if reverse else (b + 1) * half - 1
                r = cum[ref_row:ref_row + 1]
                for bb in (b, b + 1):
                    rs = slice(bb * half, (bb + 1) * half)
                    if q_side_block(bb):
                        ys.append(q[rs] * jnp.exp2(cum[rs] - r))
                        zs.append(blank(half))
                    else:
                        zs.append(k[rs] * jnp.exp2(r - cum[rs]))
            lhs = jnp.concatenate(ys, axis=0).astype(BF16)
            zk = jnp.concatenate(zs, axis=0)
        else:
            z = jnp.exp2(e[lvl * c:(lvl + 1) * c, sl])
            q_row = ((row & half) == 0) if reverse else ((row & half) != 0)
            y = jnp.where(q_row, q, k) * z
            lhs = y.astype(BF16)
            zk = jnp.where(q_row, 0.0, y)
        zt = zk.T.astype(BF16)
        rhs_t = jnp.concatenate([jnp.concatenate([zt[:HG_DK], blank_t], axis=0),
                                 jnp.concatenate([blank_t, zt[HG_DK:]], axis=0)], axis=1)
        products.append(_dot(lhs, rhs_t))
    yield
    a_cat = None
    for lvl, p in zip(range(HG_LEVELS, -1, -1), products):
        half = (1 << lvl) // 2
        if lvl > HG_TILE_LEVELS:
            q_blocks = [bb for bb in range(c // half) if q_side_block(bb)]
            new = []
            for h_idx in range(2):
                p_h = p[:, h_idx * c:(h_idx + 1) * c]
                rows = []
                for bb in range(c // half):
                    rs = slice(bb * half, (bb + 1) * half)
                    if not q_side_block(bb):
                        rows.append(blank(half) if a_cat is None else a_cat[h_idx][rs])
                        continue
                    i = q_blocks.index(bb)
                    p_blk = p_h[i * half:(i + 1) * half]
                    if a_cat is None:
                        rows.append(p_blk)
                    else:
                        rows.append(jnp.where(xor[rs] < (1 << lvl), p_blk, a_cat[h_idx][rs]))
                new.append(jnp.concatenate(rows, axis=0))
            a_cat = tuple(new)
        else:
            same = xor < (1 << lvl)
            a_cat = tuple(jnp.where(same, p[:, h_idx * c:(h_idx + 1) * c], a_cat[h_idx]) for h_idx in range(2))
    a_cat = jnp.concatenate(a_cat, axis=1)
    v_cat = jnp.concatenate([v * low_b, v * high_b], axis=0)
    st = st_scr[pair]
    o = _dot(a_cat.astype(BF16), v_cat) + _dot_nt((q * jnp.exp2(cum)).astype(BF16), st.astype(BF16))
    o_ref[0, :, sl] = o
    kd = (k * jnp.exp2(cum_last - cum)).astype(BF16)
    upd = _dot_tn(v, kd)
    st_scr[pair] = jnp.exp2(cum_last) * st + jnp.where(diag_block, upd, 0.0)


def _hgrn(qh, kf, lff, kb, lfb, vh, s0t, emit_state):
    bsz, n, _ = qh.shape
    c = HG_CHUNK
    nc = n // c
    mstack = jnp.stack([_hgrn_constants(False), _hgrn_constants(True)])
    fwd = pl.BlockSpec((1, c, HG_KW), lambda b, i: (b, i, 0))
    bwd = pl.BlockSpec((1, c, HG_KW), lambda b, i: (b, nc - 1 - i, 0))
    in_specs = [fwd, bwd, fwd, fwd, bwd, bwd, fwd, bwd, _const_spec(mstack.shape)]
    args = [qh, qh, kf, lff, kb, lfb, vh, vh, mstack]
    if s0t is not None:
        in_specs.append(pl.BlockSpec((1, 2, N_PAIRS, LANES, LANES), lambda b, i: (b, 0, 0, 0, 0)))
        args.append(s0t)
    out_specs = [fwd, bwd]
    out_shape = [jax.ShapeDtypeStruct((bsz, n, HG_W), F32)] * 2
    if emit_state:
        out_specs.append(pl.BlockSpec((1, 2, HG_HEADS, HG_DK, HG_DV), lambda b, i: (b, 0, 0, 0, 0)))
        out_shape.append(jax.ShapeDtypeStruct((bsz, 2, HG_HEADS, HG_DK, HG_DV), F32))
    return pl.pallas_call(
        functools.partial(_hgrn_kernel, has_s0=s0t is not None, emit_state=emit_state),
        grid=(bsz, nc),
        in_specs=in_specs, out_specs=out_specs, out_shape=out_shape,
        scratch_shapes=[pltpu.VMEM((2, N_PAIRS, LANES, LANES), F32)],
        compiler_params=_cparams(("parallel", "arbitrary")),
        name="hgrn",
    )(*args)


def _state_to_pairs(s):
    b = s.shape[0]
    st = jnp.swapaxes(s, -1, -2).reshape(b, N_PAIRS, 2, HG_DV, HG_DK)
    eye = jnp.eye(2, dtype=s.dtype)
    out = jnp.einsum('bpavk,ac->bpavck', st, eye)
    return out.reshape(b, N_PAIRS, 2 * HG_DV, 2 * HG_DK)


def _merge_kernel(x_ref, sh_ref, sc_ref, ga_ref, om_ref, of_ref, ob_ref, sg_ref, gmix_ref, wgate_ref, ghg_ref,
                  hmean_ref, wbm_ref, wbh_ref, wout_ref, x1_ref):
    tm = x_ref.shape[1]
    halves = [slice(i * tm // 2, (i + 1) * tm // 2) for i in range(2)]
    o = [of_ref[0, rs] + ob_ref[0, rs] for rs in halves]

    def head_mean_sq(i):
        return _dot((o[i] * o[i]).astype(BF16), hmean_ref[...])

    ms = [head_mean_sq(i) for i in range(2)]
    mla = [_dot(om_ref[0, rs], wbm_ref[...]) for rs in halves]
    hb = [(_rms(x_ref[0, rs], gmix_ref[...]) * (1.0 + sc_ref[0]) + sh_ref[0]).astype(BF16) for rs in halves]
    gates = [_dot(hb[i], wgate_ref[...]) for i in range(2)]
    hg = []
    for i, rs in enumerate(halves):
        o_hg = (o[i] * lax.rsqrt(ms[i] + EPS) * ghg_ref[...]) * sg_ref[0, rs]
        hg.append(_dot(o_hg.astype(BF16), wbh_ref[...]))
    out = []
    for i, rs in enumerate(halves):
        merged = (jax.nn.sigmoid(gates[i][:, :D_MODEL]) * mla[i]
                  + jax.nn.sigmoid(gates[i][:, D_MODEL:]) * hg[i])
        out.append(_dot(merged.astype(BF16), wout_ref[...]))
    for i, rs in enumerate(halves):
        x1_ref[0, rs] = x_ref[0, rs] + ga_ref[0] * out[i]


def _merge(x, mods, mod_row, o_mla, o_f, o_b, sg, wts, tm):
    bsz, n, _ = x.shape
    g_mix, w_gate, g_hg, hmean, w_br_mla, w_br_hg, w_out = wts

    def tok(width):
        return pl.BlockSpec((1, tm, width), lambda b, i: (b, i, 0))

    def mod(col):
        return pl.BlockSpec((1, 1, D_MODEL), lambda b, i: (mod_row(b), 0, col))

    return pl.pallas_call(
        _merge_kernel,
        grid=(bsz, n // tm),
        in_specs=[tok(D_MODEL), mod(0), mod(1), mod(2),
                  tok(MLA_W), tok(HG_W), tok(HG_W), tok(HG_W),
                  _const_spec((1, D_MODEL)), _const_spec(w_gate.shape),
                  _const_spec((1, HG_W)), _const_spec(hmean.shape), _const_spec(w_br_mla.shape),
                  _const_spec(w_br_hg.shape), _const_spec(w_out.shape)],
        out_specs=tok(D_MODEL),
        out_shape=jax.ShapeDtypeStruct((bsz, n, D_MODEL), F32),
        compiler_params=_cparams(("parallel", "parallel")),
        name="merge",
    )(x, mods, mods, mods, o_mla, o_f, o_b, sg, g_mix, w_gate, g_hg, hmean, w_br_mla, w_br_hg, w_out)


FFN_CHUNK = 1024


def _ffn_kernel(x_ref, sh_ref, sc_ref, ga_ref, gff_ref, w1_ref, w2_ref, gfin_ref, y_ref):
    tm = x_ref.shape[1]
    halves = [slice(i * tm // 2, (i + 1) * tm // 2) for i in range(2)]
    h = [(_rms(x_ref[0, rs], gff_ref[...]) * (1.0 + sc_ref[0]) + sh_ref[0]).astype(BF16) for rs in halves]

    up = [(i, c0) for c0 in range(0, D_FF, FFN_CHUNK) for i in range(2)]
    acts = [[], []]

    def use_up(i):
        def use(a):
            a = jnp.maximum(a, 0.0)
            acts[i].append((a * a).astype(BF16))
        return use

    def use_down(rs):
        def use(f):
            y_ref[0, rs] = _rms(x_ref[0, rs] + ga_ref[0] * f, gfin_ref[...])
        return use

    stages = [(functools.partial(lambda i, c0: _dot(h[i], w1_ref[:, c0:c0 + FFN_CHUNK]), i, c0), use_up(i))
              for i, c0 in up]
    stages += [(functools.partial(lambda i: _dot(jnp.concatenate(acts[i], axis=1), w2_ref[...]), i), use_down(rs))
               for i, rs in enumerate(halves)]
    _run_staggered(stages)


def _ffn(x1, mods, mod_row, wts, tm):
    bsz, n, _ = x1.shape
    g_ff, w_ff1, w_ff2, g_final = wts

    def mod(col):
        return pl.BlockSpec((1, 1, D_MODEL), lambda b, i: (mod_row(b), 0, col))

    tok = pl.BlockSpec((1, tm, D_MODEL), lambda b, i: (b, i, 0))
    return pl.pallas_call(
        _ffn_kernel,
        grid=(bsz, n // tm),
        in_specs=[tok, mod(3), mod(4), mod(5), _const_spec((1, D_MODEL)),
                  _const_spec(w_ff1.shape), _const_spec(w_ff2.shape), _const_spec((1, D_MODEL))],
        out_specs=tok,
        out_shape=jax.ShapeDtypeStruct((bsz, n, D_MODEL), F32),
        compiler_params=_cparams(("parallel", "parallel")),
        name="ffn",
    )(x1, mods, mods, mods, g_ff, w_ff1, w_ff2, g_final)


def _pad_head_cols(w, widths_in, layout):
    rows = w.shape[0]
    w = w.reshape(rows, MLA_HEADS, sum(width for _, width in widths_in))
    pieces, off = {}, 0
    for name, width in widths_in:
        pieces[name] = w[:, :, off:off + width]
        off += width
    cols = [pieces[name] if name in pieces else jnp.zeros((rows, MLA_HEADS, width), w.dtype) for name, width in layout]
    return jnp.concatenate(cols, axis=-1).reshape(rows, -1)


def _prep_weights(w_in, w_uq, w_ukv):
    w_lat = jnp.pad(w_in[:, :_LAT_W + QK_ROPE].astype(BF16), ((0, 0), (0, LANES - QK_ROPE)))
    mix0 = _LAT_W + QK_ROPE
    w_mix = w_in[:, mix0:mix0 + _MIX_OFFS[-1]].astype(BF16)
    w_gate = w_in[:, mix0 + _MIX_OFFS[-1]:].astype(BF16)
    w_in_p = (w_lat, w_mix, w_gate)
    w_uq_p = _pad_head_cols(w_uq, (("nope", QK_NOPE), ("rope", QK_ROPE)),
                            (("rope", QK_ROPE), ("zero", HEAD_PAD - QK_ROPE - QK_NOPE), ("nope", QK_NOPE))).astype(BF16)
    kv_in = (("nope", QK_NOPE), ("v", V_HEAD))
    k_cols = _pad_head_cols(w_ukv, kv_in, (("zero", HEAD_PAD - QK_NOPE), ("nope", QK_NOPE)))
    v_cols = _pad_head_cols(w_ukv, kv_in, (("v", V_HEAD), ("zero", HEAD_PAD - V_HEAD)))
    w_ukv_p = jnp.concatenate([k_cols, v_cols], axis=1).astype(BF16)
    return w_in_p, w_uq_p, w_ukv_p


def _rope_tables(n):
    rows = n // GRID_W
    row = jnp.repeat(jnp.arange(rows, dtype=F32), GRID_W)
    col = jnp.tile(jnp.arange(GRID_W, dtype=F32), rows)
    half = QK_ROPE // 2
    inv = ROPE_BASE ** (-jnp.arange(0, half, 2, dtype=F32) / half)
    ang = jnp.concatenate([row[:, None] * inv, col[:, None] * inv], axis=-1)
    cos = jnp.repeat(jnp.cos(ang), 2, axis=-1)
    sin = jnp.repeat(jnp.sin(ang), 2, axis=-1)
    cos = jnp.concatenate([cos, jnp.ones((n, LANES - QK_ROPE), F32)], axis=-1)
    sin = jnp.concatenate([sin, jnp.zeros((n, LANES - QK_ROPE), F32)], axis=-1)
    return cos, sin


def _trunk(x, mods, mod_row, wts, rope_tabs, ctx):
    inproj_w, merge_w, ffn_w = wts
    bsz, n, _ = x.shape
    if ctx is None:
        cache, past = None, 0
        s0 = None
    else:
        ckv_c, krope_c, state_c = ctx
        cache, past = (ckv_c, jnp.pad(krope_c, ((0, 0), (0, 0), (0, LANES - QK_ROPE)))), ckv_c.shape[1]
        s0 = jnp.stack([_state_to_pairs(state_c[:, 0]), _state_to_pairs(state_c[:, 1])], axis=1)
    outs = _inproj(x, mods, mod_row, inproj_w, rope_tabs, ctx is None, min(n, TOKEN_TILE), cache)
    q, k, v, qh, kf, lff, kb, lfb, vh, sg = outs[:N_INPROJ_OUT]
    if ctx is None:
        ckv, krope = outs[N_INPROJ_OUT:]
    bk = next(b for b in ATTN_KV_TILES if (n + past) % b == 0)
    o_mla = _attention(q, k, v, min(n, ATTN_Q_TILE), bk)
    o_f, o_b, *state = _hgrn(qh, kf, lff, kb, lfb, vh, s0, ctx is None)
    per_token = (x, o_mla, o_f, o_b, sg)
    if ctx is None:
        per_token = tuple(a.reshape(1, bsz * n, a.shape[-1]) for a in per_token)
    tm = min(per_token[0].shape[1], TOKEN_TILE)
    x1 = _merge(*per_token[:1], mods, mod_row, *per_token[1:], merge_w, tm)
    y = _ffn(x1, mods, mod_row, ffn_w, tm).reshape(bsz, n, D_MODEL)
    if ctx is None:
        return y, (ckv, krope, state[0])
    return y, None


def kernel(x_prompt, x_sample, cache_ckv, cache_krope, state_hgrn, c, c_ctx, w_ada, b_ada, g_norm_mix, g_norm_ff, w_in, g_q_norm, w_uq, g_kv_norm, w_ukv, g_hg_norm, hg_lb_logits, w_br_mla, w_br_hg, w_out, w_ff1, w_ff2, g_final):
    assert w_in.shape[0] == 1, "single-layer trunk"
    dec_b = c.shape[0]
    assert dec_b + 1 <= SUBLANES
    cc = jnp.concatenate([c, c_ctx[None, :], jnp.zeros((SUBLANES - dec_b - 1, D_MODEL), F32)], axis=0)
    mods = _ada(cc, w_ada[0], b_ada[0][None, :]).reshape(SUBLANES, 1, 6 * D_MODEL)

    (w_lat, w_mix, w_gate), w_uq_p, w_ukv_p = _prep_weights(w_in[0], w_uq[0], w_ukv[0])
    g_mix = g_norm_mix[0][None]
    inproj_w = ((w_lat, w_mix), g_mix, g_q_norm[0][None], w_uq_p, g_kv_norm[0][None], w_ukv_p, hg_lb_logits)
    head_id = np.arange(HG_W) // HG_DV
    hmean = jnp.asarray((head_id[:, None] == head_id[None, :]).astype(np.float32) / HG_DV, BF16)
    merge_w = (g_mix, w_gate, g_hg_norm[0][None], hmean, w_br_mla[0].astype(BF16), w_br_hg[0].astype(BF16),
               w_out[0].astype(BF16))
    ffn_w = (g_norm_ff[0][None], w_ff1[0].astype(BF16), w_ff2[0].astype(BF16), g_final[None])
    wts = (inproj_w, merge_w, ffn_w)

    y_prompt, (ckv, krope, state) = _trunk(x_prompt, mods, lambda b: dec_b, wts, None, None)
    rope_tabs = _rope_tables(x_sample.shape[1])
    y_sample, _ = _trunk(x_sample, mods, lambda b: b, wts, rope_tabs,
                         (cache_ckv[:, 0], cache_krope[:, 0], state_hgrn[:, 0]))
    return (y_prompt, y_sample, ckv[:, None], krope[:, None], state[:, None])
```

```python
import functools

import numpy as np
import jax
import jax.numpy as jnp
from jax import lax
from jax.experimental import pallas as pl
from jax.experimental.pallas import tpu as pltpu

D_MODEL = 1024
GRID_W = 64
MLA_HEADS = 8
Q_LORA = 384
KV_LORA = 256
QK_NOPE = 64
QK_ROPE = 32
V_HEAD = 64
MLA_W = MLA_HEADS * V_HEAD
MLA_SCALE = (QK_NOPE + QK_ROPE) ** -0.5
LOG2_E = 1.4426950408889634
HG_HEADS = 8
HG_DK = 64
HG_DV = 64
HG_KW = HG_HEADS * HG_DK
HG_W = HG_HEADS * HG_DV
D_FF = 4 * D_MODEL
ROPE_BASE = 10000.0
EPS = 1e-6

LANES = 128
SUBLANES = 8
HEAD_PAD = LANES
ROPE_PAD = (QK_NOPE, HEAD_PAD - QK_NOPE - QK_ROPE)
QK_PAD_W = MLA_HEADS * HEAD_PAD
N_PAIRS = HG_HEADS // 2
VMEM_LIMIT = 56 * 1024 * 1024

_LAT_W = Q_LORA + KV_LORA
_MIX_SIZES = (HG_KW, HG_KW, HG_KW, HG_W, HG_W)
_MIX_OFFS = tuple(int(o) for o in np.cumsum((0,) + _MIX_SIZES))

HG_CHUNK = 128
HG_LEVELS = 7

TOKEN_TILE = 512
ATTN_Q_TILE = 1024
ATTN_Q_SPLIT = 4
ATTN_KV_TILES = (768, 512, 256, 128)
ADA_COL_TILE = 1536

F32 = jnp.float32
BF16 = jnp.bfloat16


def _cparams(sem):
    return pltpu.CompilerParams(dimension_semantics=sem, vmem_limit_bytes=VMEM_LIMIT)


def _rms(x, g):
    return x * lax.rsqrt(jnp.mean(x * x, axis=-1, keepdims=True) + EPS) * g


def _dot(a, b):
    return jnp.dot(a, b, preferred_element_type=F32)


def _dot_nt(a, b):
    return lax.dot_general(a, b, (((1,), (1,)), ((), ())), preferred_element_type=F32)


def _dot_tn(a, b):
    return lax.dot_general(a, b, (((0,), (0,)), ((), ())), preferred_element_type=F32)


def _split_hi_lo(x):
    hi = x.astype(BF16)
    lo = (x - hi.astype(F32)).astype(BF16)
    return hi, lo


def _ada_kernel(c_ref, w_ref, b_ref, o_ref):
    c = c_ref[...]
    a = c * jax.nn.sigmoid(c)
    a_hi, a_lo = _split_hi_lo(a)
    w_hi, w_lo = _split_hi_lo(w_ref[...])
    o_ref[...] = _dot(a_hi, w_hi) + _dot(a_hi, w_lo) + _dot(a_lo, w_hi) + b_ref[...]


def _ada(cc, w_ada, b_ada):
    rows, tn = cc.shape[0], ADA_COL_TILE
    n = w_ada.shape[1]
    return pl.pallas_call(
        _ada_kernel,
        grid=(n // tn,),
        in_specs=[pl.BlockSpec((rows, D_MODEL), lambda j: (0, 0)),
                  pl.BlockSpec((D_MODEL, tn), lambda j: (0, j)),
                  pl.BlockSpec((1, tn), lambda j: (0, j))],
        out_specs=pl.BlockSpec((rows, tn), lambda j: (0, j)),
        out_shape=jax.ShapeDtypeStruct((rows, n), F32),
        compiler_params=_cparams(("arbitrary",)),
        name="ada",
    )(cc, w_ada, b_ada)


def _rope_tile(blk, cos, sin, even):
    rot = jnp.where(even, -pltpu.roll(blk, LANES - 1, 1), pltpu.roll(blk, 1, 1))
    return blk * cos + rot * sin


def _run_staggered(stages):
    pending = stages[0][0]()
    for i, (_, consume) in enumerate(stages):
        current = pending
        if i + 1 < len(stages):
            pending = stages[i + 1][0]()
        consume(current)


def _store_kv(kv, kr, k_ref, v_ref, rs=slice(None)):
    rows = kv.shape[0]
    nope_lane = lax.broadcasted_iota(jnp.int32, (rows, HEAD_PAD), 1) < QK_NOPE
    ones_row = jnp.where(lax.broadcasted_iota(jnp.int32, (SUBLANES, rows), 0) == 0, 1.0, 0.0)
    tail = jnp.concatenate([ones_row, jnp.zeros((HEAD_PAD - V_HEAD - SUBLANES, rows), F32)], axis=0)
    for hh in range(MLA_HEADS):
        sl = slice(hh * HEAD_PAD, (hh + 1) * HEAD_PAD)
        blk = kv[:, sl]
        k_ref[0, rs, sl] = jnp.where(nope_lane, blk, kr).astype(BF16)
        v_t = blk.T[QK_NOPE:]
        v_ref[0, sl, rs] = jnp.concatenate([v_t, tail], axis=0).astype(BF16)


N_INPROJ_IN = 11
N_INPROJ_OUT = 10


def _inproj_kernel(*refs, rope, cache_out, n_blocks, with_cache):
    if with_cache:
        step = pl.program_id(1)
        n_in = N_INPROJ_IN + 2 * rope
        ckvc_ref, krc_ref = refs[n_in:n_in + 2]
        refs = refs[:n_in] + refs[n_in + 2:]
        k_ref, v_ref, wukv_ref = refs[n_in + 1], refs[n_in + 2], refs[9]

        @pl.when(step >= n_blocks)
        def _():
            _store_kv(_dot(ckvc_ref[0].astype(BF16), wukv_ref[...]), krc_ref[0], k_ref, v_ref)

        pl.when(step < n_blocks)(functools.partial(_inproj_tile, refs, rope, cache_out))
    else:
        _inproj_tile(refs, rope, cache_out)


def _inproj_tile(refs, rope, cache_out):
    (x_ref, sh_ref, sc_ref, gmix_ref, wlat_ref, wmix_ref, gq_ref, wuq_ref, gkv_ref, wukv_ref, lbl_ref) = refs[:N_INPROJ_IN]
    refs = refs[N_INPROJ_IN:]
    if rope:
        cos_ref, sin_ref = refs[:2]
        refs = refs[2:]
    (q_ref, k_ref, v_ref, qh_ref, kf_ref, lff_ref, kb_ref, lfb_ref, vh_ref, sg_ref) = refs[:N_INPROJ_OUT]
    refs = refs[N_INPROJ_OUT:]
    tm = x_ref.shape[1]
    n_parts = 2 if tm == TOKEN_TILE else 1
    parts = [slice(i * tm // n_parts, (i + 1) * tm // n_parts) for i in range(n_parts)]

    def stages_for(rs):
        hb = (_rms(x_ref[0, rs], gmix_ref[...]) * (1.0 + sc_ref[0]) + sh_ref[0]).astype(BF16)
        if rope:
            cos, sin = cos_ref[rs], sin_ref[rs]
            even = (lax.broadcasted_iota(jnp.int32, cos.shape, 1) & 1) == 0
        kept = {}

        def use_latents(y):
            kept["qn"] = _rms(y[:, :Q_LORA], gq_ref[...]).astype(BF16)
            kept["ckv"] = _rms(y[:, Q_LORA:_LAT_W], gkv_ref[...])
            kr = y[:, _LAT_W:]
            if cache_out:
                refs[0][0, rs] = kept["ckv"]
                refs[1][0, rs] = kr[:, ROPE_PAD[0]:ROPE_PAD[0] + QK_ROPE]
            kept["kr"] = _rope_tile(kr, cos, sin, even) if rope else kr

        def use_q(q):
            q = q * (MLA_SCALE * LOG2_E)
            for hh in range(MLA_HEADS):
                sl = slice(hh * HEAD_PAD, (hh + 1) * HEAD_PAD)
                blk = q[:, sl]
                if rope:
                    blk = _rope_tile(blk, cos, sin, even)
                q_ref[0, sl, rs] = blk.T.astype(BF16)

        def use_hq(hq):
            qh_ref[0, rs] = hq * jax.nn.sigmoid(hq) * (HG_DK ** -0.5)

        def use_forget(d, k_out, lf_out):
            def use(z):
                l0, l1 = lbl_ref[0, d:d + 1], lbl_ref[1, d:d + 1]
                lmax = jnp.maximum(l0, l1)
                e0, e1 = jnp.exp(l0 - lmax), jnp.exp(l1 - lmax)
                lbd = e0 / (e0 + e1)
                f = lbd + (1.0 - lbd) * jax.nn.sigmoid(z)
                k_out[0, rs] = 1.0 - f
                lf_out[0, rs] = jnp.log2(f)
            return use

        def use_hi(y):
            vh_ref[0, rs] = y.astype(vh_ref.dtype)

        def use_hg(hg):
            sg_ref[0, rs] = (hg * jax.nn.sigmoid(hg)).astype(sg_ref.dtype)

        def mix(i):
            return lambda: _dot(hb, wmix_ref[:, _MIX_OFFS[i]:_MIX_OFFS[i + 1]])

        return [
            (lambda: _dot(hb, wlat_ref[...]), use_latents),
            (mix(0), use_hq),
            (lambda: _dot(kept["qn"], wuq_ref[...]), use_q),
            (lambda: _dot(kept["ckv"].astype(BF16), wukv_ref[...]),
             lambda kv: _store_kv(kv, kept["kr"], k_ref, v_ref, rs)),
            (mix(1), use_forget(0, kf_ref, lff_ref)),
            (mix(2), use_forget(1, kb_ref, lfb_ref)),
            (mix(3), use_hi),
            (mix(4), use_hg),
        ]

    per_part = [stages_for(rs) for rs in parts]
    _run_staggered([stage for group in zip(*per_part) for stage in group])


def _const_spec(shape):
    return pl.BlockSpec(shape, lambda *_: (0,) * len(shape), pipeline_mode=pl.Buffered(1))


def _inproj(x, mods, mod_row, wts, rope_tabs, cache_out, tm, cache):
    bsz, n, _ = x.shape
    rope = rope_tabs is not None
    (w_lat, w_mix), g_mix, g_q, w_uq_p, g_kv, w_ukv_p, lb_logits = wts
    n_blocks = n // tm
    past = 0 if cache is None else cache[0].shape[1]
    assert n_blocks * tm == n and past % tm == 0
    kv_rows = n + past

    def own(i):
        return i if cache is None else jnp.minimum(i, n_blocks - 1)

    def tok(width):
        return pl.BlockSpec((1, tm, width), lambda b, i: (b, own(i), 0))

    def mod(col):
        return pl.BlockSpec((1, 1, D_MODEL), lambda b, i: (mod_row(b), 0, col))

    in_specs = [tok(D_MODEL), mod(0), mod(1), _const_spec((1, D_MODEL)), _const_spec(w_lat.shape),
                _const_spec(w_mix.shape),
                _const_spec((1, Q_LORA)), _const_spec(w_uq_p.shape), _const_spec((1, KV_LORA)),
                _const_spec(w_ukv_p.shape), _const_spec(lb_logits.shape)]
    args = [x, mods, mods, g_mix, w_lat, w_mix, g_q, w_uq_p, g_kv, w_ukv_p, lb_logits]
    assert len(args) == N_INPROJ_IN
    if rope:
        in_specs += [pl.BlockSpec((tm, LANES), lambda b, i: (own(i), 0))] * 2
        args += list(rope_tabs)
    if cache is not None:
        in_specs += [pl.BlockSpec((1, tm, w), lambda b, i: (b, jnp.maximum(i - n_blocks, 0), 0))
                     for w in (KV_LORA, LANES)]
        args += list(cache)

    widths = [(QK_PAD_W, BF16)] * 3 + [(HG_KW, F32)] * 5 + [(HG_W, BF16)] * 2
    assert len(widths) == N_INPROJ_OUT
    if cache_out:
        widths += [(KV_LORA, F32), (QK_ROPE, F32)]
    out_specs = [tok(w) for w, _ in widths]
    out_shape = [jax.ShapeDtypeStruct((bsz, n, w), dt) for w, dt in widths]
    out_specs[0] = pl.BlockSpec((1, QK_PAD_W, tm), lambda b, i: (b, 0, own(i)))
    out_shape[0] = jax.ShapeDtypeStruct((bsz, QK_PAD_W, n), BF16)
    out_specs[1] = pl.BlockSpec((1, tm, QK_PAD_W), lambda b, i: (b, i, 0))
    out_shape[1] = jax.ShapeDtypeStruct((bsz, kv_rows, QK_PAD_W), BF16)
    out_specs[2] = pl.BlockSpec((1, QK_PAD_W, tm), lambda b, i: (b, 0, i))
    out_shape[2] = jax.ShapeDtypeStruct((bsz, QK_PAD_W, kv_rows), BF16)
    return pl.pallas_call(
        functools.partial(_inproj_kernel, rope=rope, cache_out=cache_out, n_blocks=n_blocks,
                          with_cache=cache is not None),
        grid=(bsz, kv_rows // tm),
        in_specs=in_specs, out_specs=out_specs, out_shape=out_shape,
        compiler_params=_cparams(("parallel", "arbitrary")),
        name="inproj_rope" if rope else "inproj",
    )(*args)


def _attn_kernel(q_ref, k_ref, v_ref, o_ref, m_scr, acc_scr):
    j = pl.program_id(2)

    @pl.when(j == 0)
    def _():
        m_scr[...] = jnp.full(m_scr.shape, -jnp.inf, F32)
        acc_scr[...] = jnp.zeros(acc_scr.shape, F32)

    def scores(hh):
        sl = slice(hh * HEAD_PAD, (hh + 1) * HEAD_PAD)
        return _dot(k_ref[0, :, sl], q_ref[0, sl, :])

    ahead = 2
    pending = [scores(hh) for hh in range(ahead)]
    for hh in range(MLA_HEADS):
        sl = slice(hh * HEAD_PAD, (hh + 1) * HEAD_PAD)
        s = pending.pop(0)
        if hh + ahead < MLA_HEADS:
            pending.append(scores(hh + ahead))
        bq = s.shape[1]
        n_split = min(ATTN_Q_SPLIT, bq // LANES)
        for cs in (slice(c * bq // n_split, (c + 1) * bq // n_split) for c in range(n_split)):
            s_c = s[:, cs]
            m_prev = m_scr[hh, :, cs]
            m_cur = jnp.maximum(m_prev, jnp.max(s_c, axis=0, keepdims=True))
            alpha = jnp.exp2(m_prev - m_cur)
            p = jnp.exp2(s_c - m_cur[:1])
            m_scr[hh, :, cs] = m_cur
            acc_scr[hh, :, cs] = alpha[:1] * acc_scr[hh, :, cs] + _dot(v_ref[0, sl, :], p.astype(BF16))

    @pl.when(j == pl.num_programs(2) - 1)
    def _():
        def head_out(hh):
            acc = acc_scr[hh]
            return acc[:V_HEAD] / acc[V_HEAD:V_HEAD + 1]

        for pair in range(MLA_HEADS // 2):
            both = jnp.concatenate([head_out(2 * pair), head_out(2 * pair + 1)], axis=0)
            o_ref[0, :, pair * LANES:(pair + 1) * LANES] = both.T.astype(o_ref.dtype)


def _attention(q, k, v, bq, bk):
    bsz, _, nq = q.shape
    nk = k.shape[1]
    return pl.pallas_call(
        _attn_kernel,
        grid=(bsz, nq // bq, nk // bk),
        in_specs=[pl.BlockSpec((1, QK_PAD_W, bq), lambda b, i, j: (b, 0, i)),
                  pl.BlockSpec((1, bk, QK_PAD_W), lambda b, i, j: (b, j, 0)),
                  pl.BlockSpec((1, QK_PAD_W, bk), lambda b, i, j: (b, 0, j))],
        out_specs=pl.BlockSpec((1, bq, MLA_W), lambda b, i, j: (b, i, 0)),
        out_shape=jax.ShapeDtypeStruct((bsz, nq, MLA_W), BF16),
        scratch_shapes=[pltpu.VMEM((MLA_HEADS, SUBLANES, bq), F32),
                        pltpu.VMEM((MLA_HEADS, HEAD_PAD, bq), F32)],
        compiler_params=_cparams(("parallel", "parallel", "arbitrary")),
        name="attn",
    )(q, k, v)


HG_TILE_LEVELS = 3


def _hgrn_constants(reverse):
    c = HG_CHUNK
    t = np.arange(c)
    mats = [(t[None, :] <= t[:, None])]
    for lvl in range(1, HG_TILE_LEVELS + 1):
        g, half = 1 << lvl, 1 << (lvl - 1)
        p = t % g
        mid = t - p + half - 1
        isq = p >= half
        u = t[None, :]
        mats.append(np.where(isq[:, None], (u > mid[:, None]) & (u <= t[:, None]),
                             (u > t[:, None]) & (u <= mid[:, None])))
    mstack = np.stack(mats).astype(np.float32)
    if reverse:
        mstack = mstack[:, ::-1, ::-1]
    mstack = mstack.reshape(-1, c)
    return jnp.asarray(np.concatenate([mstack, mstack], axis=1), BF16)


def _hgrn_kernel(qf_ref, qb_ref, kf_ref, lff_ref, kb_ref, lfb_ref, vf_ref, vb_ref, mstack_ref, *rest,
                 has_s0, emit_state):
    if has_s0:
        s0_ref, rest = rest[0], rest[1:]
    of_ref, ob_ref = rest[:2]
    rest = rest[2:]
    st_scr = rest[-1]
    c = HG_CHUNK
    step = pl.program_id(1)

    @pl.when(step == 0)
    def _():
        st_scr[...] = s0_ref[0] if has_s0 else jnp.zeros(st_scr.shape, F32)

    dirs = ((qf_ref, kf_ref, lff_ref, vf_ref, of_ref, False), (qb_ref, kb_ref, lfb_ref, vb_ref, ob_ref, True))
    exps = []
    for d, (_, _, lf_ref, _, _, _) in enumerate(dirs):
        lf_hi, lf_lo = _split_hi_lo(lf_ref[0])
        exps.append(_dot(mstack_ref[d], jnp.concatenate([lf_hi, lf_lo], axis=0)))

    units = [_hgrn_pair(q_ref, k_ref, v_ref, o_ref, exps[d], st_scr.at[d], pair, reverse)
             for pair in range(N_PAIRS) for d, (q_ref, k_ref, _, v_ref, o_ref, reverse) in enumerate(dirs)]
    ahead = 3
    for unit in units[:ahead]:
        next(unit)
    for i, unit in enumerate(units):
        if i + ahead < len(units):
            next(units[i + ahead])
        next(unit, None)

    if emit_state:
        sfin_ref = rest[0]

        @pl.when(step == pl.num_programs(1) - 1)
        def _():
            for d in range(2):
                for pair in range(N_PAIRS):
                    s_pair = st_scr[d, pair].T
                    sfin_ref[0, d, 2 * pair] = s_pair[:HG_DK, :HG_DV]
                    sfin_ref[0, d, 2 * pair + 1] = pltpu.roll(s_pair, HG_DV, 1)[HG_DK:, :HG_DV]


def _hgrn_pair(q_ref, k_ref, v_ref, o_ref, e, st_scr, pair, reverse):
    c = HG_CHUNK
    last = 0 if reverse else c - 1
    row = lax.broadcasted_iota(jnp.int32, (c, LANES), 0)
    lane_low = lax.broadcasted_iota(jnp.int32, (c, LANES), 1) < HG_DK
    low_b = jnp.where(lane_low, 1.0, 0.0).astype(BF16)
    high_b = jnp.where(lane_low, 0.0, 1.0).astype(BF16)
    blank_t = jnp.zeros((HG_DK, c), BF16)
    xor = lax.broadcasted_iota(jnp.int32, (c, c), 0) ^ lax.broadcasted_iota(jnp.int32, (c, c), 1)
    diag_block = (lax.broadcasted_iota(jnp.int32, (LANES, LANES), 0) < HG_DV) == (
        lax.broadcasted_iota(jnp.int32, (LANES, LANES), 1) < HG_DK)
    zeros8 = jnp.zeros((SUBLANES, LANES), F32)

    def q_side_block(b):
        return (b % 2 == 1) != reverse

    sl = slice(pair * LANES, (pair + 1) * LANES)
    q = q_ref[0, :, sl]
    k = k_ref[0, :, sl]
    v = v_ref[0, :, sl]
    cum = e[0:c, sl]
    cum_last = cum[last:last + 1]

    def blank(rows):
        return jnp.concatenate([zeros8] * (rows // SUBLANES), axis=0)

    products = []
    for lvl in range(HG_LEVELS, -1, -1):
        half = (1 << lvl) // 2
        if lvl == 0:
            lhs, zk = q.astype(BF16), k
        elif lvl > HG_TILE_LEVELS:
            ys, zs = [], []
            for b in range(0, c // half, 2):
                ref_row = (b + 1) * half if reverse else (b + 1) * half - 1
                r = cum[ref_row:ref_row + 1]
                for bb in (b, b + 1):
                    rs = slice(bb * half, (bb + 1) * half)
                    if q_side_block(bb):
                        ys.append(q[rs] * jnp.exp2(cum[rs] - r))
                        zs.append(blank(half))
                    else:
                        zs.append(k[rs] * jnp.exp2(r - cum[rs]))
            lhs = jnp.concatenate(ys, axis=0).astype(BF16)
            zk = jnp.concatenate(zs, axis=0)
        else:
            z = jnp.exp2(e[lvl * c:(lvl + 1) * c, sl])
            q_row = ((row & half) == 0) if reverse else ((row & half) != 0)
            y = jnp.where(q_row, q, k) * z
            lhs = y.astype(BF16)
            zk = jnp.where(q_row, 0.0, y)
        zt = zk.T.astype(BF16)
        rhs_t = jnp.concatenate([jnp.concatenate([zt[:HG_DK], blank_t], axis=0),
                                 jnp.concatenate([blank_t, zt[HG_DK:]], axis=0)], axis=1)
        products.append(_dot(lhs, rhs_t))
    yield
    a_cat = None
    for lvl, p in zip(range(HG_LEVELS, -1, -1), products):
        half = (1 << lvl) // 2
        if lvl > HG_TILE_LEVELS:
            q_blocks = [bb for bb in range(c // half) if q_side_block(bb)]
            new = []
            for h_idx in range(2):
                p_h = p[:, h_idx * c:(h_idx + 1) * c]
                rows = []
                for bb in range(c // half):
                    rs = slice(bb * half, (bb + 1) * half)
                    if not q_side_block(bb):
                        rows.append(blank(half) if a_cat is None else a_cat[h_idx][rs])
                        continue
                    i = q_blocks.index(bb)
                    p_blk = p_h[i * half:(i + 1) * half]
                    if a_cat is None:
                        rows.append(p_blk)
                    else:
                        rows.append(jnp.where(xor[rs] < (1 << lvl), p_blk, a_cat[h_idx][rs]))
                new.append(jnp.concatenate(rows, axis=0))
            a_cat = tuple(new)
        else:
            same = xor < (1 << lvl)
            a_cat = tuple(jnp.where(same, p[:, h_idx * c:(h_idx + 1) * c], a_cat[h_idx]) for h_idx in range(2))
    a_cat = jnp.concatenate(a_cat, axis=1)
    v_cat = jnp.concatenate([v * low_b, v * high_b], axis=0)
    st = st_scr[pair]
    o = _dot(a_cat.astype(BF16), v_cat) + _dot_nt((q * jnp.exp2(cum)).astype(BF16), st.astype(BF16))
    o_ref[0, :, sl] = o
    kd = (k * jnp.exp2(cum_last - cum)).astype(BF16)
    upd = _dot_tn(v, kd)
    st_scr[pair] = jnp.exp2(cum_last) * st + jnp.where(diag_block, upd, 0.0)


def _hgrn(qh, kf, lff, kb, lfb, vh, s0t, emit_state):
    bsz, n, _ = qh.shape
    c = HG_CHUNK
    nc = n // c
    mstack = jnp.stack([_hgrn_constants(False), _hgrn_constants(True)])
    fwd = pl.BlockSpec((1, c, HG_KW), lambda b, i: (b, i, 0))
    bwd = pl.BlockSpec((1, c, HG_KW), lambda b, i: (b, nc - 1 - i, 0))
    in_specs = [fwd, bwd, fwd, fwd, bwd, bwd, fwd, bwd, _const_spec(mstack.shape)]
    args = [qh, qh, kf, lff, kb, lfb, vh, vh, mstack]
    if s0t is not None:
        in_specs.append(pl.BlockSpec((1, 2, N_PAIRS, LANES, LANES), lambda b, i: (b, 0, 0, 0, 0)))
        args.append(s0t)
    out_specs = [fwd, bwd]
    out_shape = [jax.ShapeDtypeStruct((bsz, n, HG_W), F32)] * 2
    if emit_state:
        out_specs.append(pl.BlockSpec((1, 2, HG_HEADS, HG_DK, HG_DV), lambda b, i: (b, 0, 0, 0, 0)))
        out_shape.append(jax.ShapeDtypeStruct((bsz, 2, HG_HEADS, HG_DK, HG_DV), F32))
    return pl.pallas_call(
        functools.partial(_hgrn_kernel, has_s0=s0t is not None, emit_state=emit_state),
        grid=(bsz, nc),
        in_specs=in_specs, out_specs=out_specs, out_shape=out_shape,
        scratch_shapes=[pltpu.VMEM((2, N_PAIRS, LANES, LANES), F32)],
        compiler_params=_cparams(("parallel", "arbitrary")),
        name="hgrn",
    )(*args)


def _state_to_pairs(s):
    b = s.shape[0]
    st = jnp.swapaxes(s, -1, -2).reshape(b, N_PAIRS, 2, HG_DV, HG_DK)
    eye = jnp.eye(2, dtype=s.dtype)
    out = jnp.einsum('bpavk,ac->bpavck', st, eye)
    return out.reshape(b, N_PAIRS, 2 * HG_DV, 2 * HG_DK)


def _merge_kernel(x_ref, sh_ref, sc_ref, ga_ref, om_ref, of_ref, ob_ref, sg_ref, gmix_ref, wgate_ref, ghg_ref,
                  hmean_ref, wbm_ref, wbh_ref, wout_ref, x1_ref):
    tm = x_ref.shape[1]
    halves = [slice(i * tm // 2, (i + 1) * tm // 2) for i in range(2)]
    o = [of_ref[0, rs] + ob_ref[0, rs] for rs in halves]

    def head_mean_sq(i):
        return _dot((o[i] * o[i]).astype(BF16), hmean_ref[...])

    ms = [head_mean_sq(i) for i in range(2)]
    mla = [_dot(om_ref[0, rs], wbm_ref[...]) for rs in halves]
    hb = [(_rms(x_ref[0, rs], gmix_ref[...]) * (1.0 + sc_ref[0]) + sh_ref[0]).astype(BF16) for rs in halves]
    gates = [_dot(hb[i], wgate_ref[...]) for i in range(2)]
    hg = []
    for i, rs in enumerate(halves):
        o_hg = (o[i] * lax.rsqrt(ms[i] + EPS) * ghg_ref[...]) * sg_ref[0, rs]
        hg.append(_dot(o_hg.astype(BF16), wbh_ref[...]))
    out = []
    for i, rs in enumerate(halves):
        merged = (jax.nn.sigmoid(gates[i][:, :D_MODEL]) * mla[i]
                  + jax.nn.sigmoid(gates[i][:, D_MODEL:]) * hg[i])
        out.append(_dot(merged.astype(BF16), wout_ref[...]))
    for i, rs in enumerate(halves):
        x1_ref[0, rs] = x_ref[0, rs] + ga_ref[0] * out[i]


def _merge(x, mods, mod_row, o_mla, o_f, o_b, sg, wts, tm):
    bsz, n, _ = x.shape
    g_mix, w_gate, g_hg, hmean, w_br_mla, w_br_hg, w_out = wts

    def tok(width):
        return pl.BlockSpec((1, tm, width), lambda b, i: (b, i, 0))

    def mod(col):
        return pl.BlockSpec((1, 1, D_MODEL), lambda b, i: (mod_row(b), 0, col))

    return pl.pallas_call(
        _merge_kernel,
        grid=(bsz, n // tm),
        in_specs=[tok(D_MODEL), mod(0), mod(1), mod(2),
                  tok(MLA_W), tok(HG_W), tok(HG_W), tok(HG_W),
                  _const_spec((1, D_MODEL)), _const_spec(w_gate.shape),
                  _const_spec((1, HG_W)), _const_spec(hmean.shape), _const_spec(w_br_mla.shape),
                  _const_spec(w_br_hg.shape), _const_spec(w_out.shape)],
        out_specs=tok(D_MODEL),
        out_shape=jax.ShapeDtypeStruct((bsz, n, D_MODEL), F32),
        compiler_params=_cparams(("parallel", "parallel")),
        name="merge",
    )(x, mods, mods, mods, o_mla, o_f, o_b, sg, g_mix, w_gate, g_hg, hmean, w_br_mla, w_br_hg, w_out)


FFN_CHUNK = 1024


def _ffn_kernel(x_ref, sh_ref, sc_ref, ga_ref, gff_ref, w1_ref, w2_ref, gfin_ref, y_ref):
    tm = x_ref.shape[1]
    halves = [slice(i * tm // 2, (i + 1) * tm // 2) for i in range(2)]
    h = [(_rms(x_ref[0, rs], gff_ref[...]) * (1.0 + sc_ref[0]) + sh_ref[0]).astype(BF16) for rs in halves]

    up = [(i, c0) for c0 in range(0, D_FF, FFN_CHUNK) for i in range(2)]
    acts = [[], []]

    def use_up(i):
        def use(a):
            a = jnp.maximum(a, 0.0)
            acts[i].append((a * a).astype(BF16))
        return use

    def use_down(rs):
        def use(f):
            y_ref[0, rs] = _rms(x_ref[0, rs] + ga_ref[0] * f, gfin_ref[...])
        return use

    stages = [(functools.partial(lambda i, c0: _dot(h[i], w1_ref[:, c0:c0 + FFN_CHUNK]), i, c0), use_up(i))
              for i, c0 in up]
    stages += [(functools.partial(lambda i: _dot(jnp.concatenate(acts[i], axis=1), w2_ref[...]), i), use_down(rs))
               for i, rs in enumerate(halves)]
    _run_staggered(stages)


def _ffn(x1, mods, mod_row, wts, tm):
    bsz, n, _ = x1.shape
    g_ff, w_ff1, w_ff2, g_final = wts

    def mod(col):
        return pl.BlockSpec((1, 1, D_MODEL), lambda b, i: (mod_row(b), 0, col))

    tok = pl.BlockSpec((1, tm, D_MODEL), lambda b, i: (b, i, 0))
    return pl.pallas_call(
        _ffn_kernel,
        grid=(bsz, n // tm),
        in_specs=[tok, mod(3), mod(4), mod(5), _const_spec((1, D_MODEL)),
                  _const_spec(w_ff1.shape), _const_spec(w_ff2.shape), _const_spec((1, D_MODEL))],
        out_specs=tok,
        out_shape=jax.ShapeDtypeStruct((bsz, n, D_MODEL), F32),
        compiler_params=_cparams(("parallel", "parallel")),
        name="ffn",
    )(x1, mods, mods, mods, g_ff, w_ff1, w_ff2, g_final)


def _prep_weights(w_in, w_uq, w_ukv):
    w_lat = jnp.concatenate([w_in[:, :_LAT_W].astype(BF16),
                             jnp.pad(w_in[:, _LAT_W:_LAT_W + QK_ROPE].astype(BF16), ((0, 0), ROPE_PAD))], axis=1)
    mix0 = _LAT_W + QK_ROPE
    w_mix = w_in[:, mix0:mix0 + _MIX_OFFS[-1]].astype(BF16)
    w_gate = w_in[:, mix0 + _MIX_OFFS[-1]:].astype(BF16)
    w_in_p = (w_lat, w_mix, w_gate)
    w_uq_p = jnp.pad(w_uq.reshape(Q_LORA, MLA_HEADS, QK_NOPE + QK_ROPE).astype(BF16),
                     ((0, 0), (0, 0), (0, ROPE_PAD[1]))).reshape(Q_LORA, QK_PAD_W)
    assert QK_NOPE + V_HEAD == HEAD_PAD
    return w_in_p, w_uq_p, w_ukv.astype(BF16)


def _rope_tables(n):
    rows = n // GRID_W
    row = jnp.repeat(jnp.arange(rows, dtype=F32), GRID_W)
    col = jnp.tile(jnp.arange(GRID_W, dtype=F32), rows)
    half = QK_ROPE // 2
    inv = ROPE_BASE ** (-jnp.arange(0, half, 2, dtype=F32) / half)
    ang = jnp.concatenate([row[:, None] * inv, col[:, None] * inv], axis=-1)
    cos = jnp.repeat(jnp.cos(ang), 2, axis=-1)
    sin = jnp.repeat(jnp.sin(ang), 2, axis=-1)
    cos = jnp.pad(cos, ((0, 0), ROPE_PAD), constant_values=1.0)
    sin = jnp.pad(sin, ((0, 0), ROPE_PAD))
    return cos, sin


def _trunk(x, mods, mod_row, wts, rope_tabs, ctx):
    inproj_w, merge_w, ffn_w = wts
    bsz, n, _ = x.shape
    if ctx is None:
        cache, past = None, 0
        s0 = None
    else:
        ckv_c, krope_c, state_c = ctx
        cache, past = (ckv_c, jnp.pad(krope_c, ((0, 0), (0, 0), ROPE_PAD))), ckv_c.shape[1]
        s0 = jnp.stack([_state_to_pairs(state_c[:, 0]), _state_to_pairs(state_c[:, 1])], axis=1)
    outs = _inproj(x, mods, mod_row, inproj_w, rope_tabs, ctx is None, min(n, TOKEN_TILE), cache)
    q, k, v, qh, kf, lff, kb, lfb, vh, sg = outs[:N_INPROJ_OUT]
    if ctx is None:
        ckv, krope = outs[N_INPROJ_OUT:]
    bk = next(b for b in ATTN_KV_TILES if (n + past) % b == 0)
    o_mla = _attention(q, k, v, min(n, ATTN_Q_TILE), bk)
    o_f, o_b, *state = _hgrn(qh, kf, lff, kb, lfb, vh, s0, ctx is None)
    per_token = (x, o_mla, o_f, o_b, sg)
    if ctx is None:
        per_token = tuple(a.reshape(1, bsz * n, a.shape[-1]) for a in per_token)
    tm = min(per_token[0].shape[1], TOKEN_TILE)
    x1 = _merge(*per_token[:1], mods, mod_row, *per_token[1:], merge_w, tm)
    y = _ffn(x1, mods, mod_row, ffn_w, tm).reshape(bsz, n, D_MODEL)
    if ctx is None:
        return y, (ckv, krope, state[0])
    return y, None


def kernel(x_prompt, x_sample, cache_ckv, cache_krope, state_hgrn, c, c_ctx, w_ada, b_ada, g_norm_mix, g_norm_ff, w_in, g_q_norm, w_uq, g_kv_norm, w_ukv, g_hg_norm, hg_lb_logits, w_br_mla, w_br_hg, w_out, w_ff1, w_ff2, g_final):
    assert w_in.shape[0] == 1, "single-layer trunk"
    dec_b = c.shape[0]
    assert dec_b + 1 <= SUBLANES
    cc = jnp.concatenate([c, c_ctx[None, :], jnp.zeros((SUBLANES - dec_b - 1, D_MODEL), F32)], axis=0)
    mods = _ada(cc, w_ada[0], b_ada[0][None, :]).reshape(SUBLANES, 1, 6 * D_MODEL)

    (w_lat, w_mix, w_gate), w_uq_p, w_ukv_p = _prep_weights(w_in[0], w_uq[0], w_ukv[0])
    g_mix = g_norm_mix[0][None]
    inproj_w = ((w_lat, w_mix), g_mix, g_q_norm[0][None], w_uq_p, g_kv_norm[0][None], w_ukv_p, hg_lb_logits)
    head_id = np.arange(HG_W) // HG_DV
    hmean = jnp.asarray((head_id[:, None] == head_id[None, :]).astype(np.float32) / HG_DV, BF16)
    merge_w = (g_mix, w_gate, g_hg_norm[0][None], hmean, w_br_mla[0].astype(BF16), w_br_hg[0].astype(BF16),
               w_out[0].astype(BF16))
    ffn_w = (g_norm_ff[0][None], w_ff1[0].astype(BF16), w_ff2[0].astype(BF16), g_final[None])
    wts = (inproj_w, merge_w, ffn_w)

    y_prompt, (ckv, krope, state) = _trunk(x_prompt, mods, lambda b: dec_b, wts, None, None)
    rope_tabs = _rope_tables(x_sample.shape[1])
    y_sample, _ = _trunk(x_sample, mods, lambda b: b, wts, rope_tabs,
                         (cache_ckv[:, 0], cache_krope[:, 0], state_hgrn[:, 0]))
    return (y_prompt, y_sample, ckv[:, None], krope[:, None], state[:, None])
```

```python
import functools

import numpy as np
import jax
import jax.numpy as jnp
from jax import lax
from jax.experimental import pallas as pl
from jax.experimental.pallas import tpu as pltpu

D_MODEL = 1024
GRID_W = 64
MLA_HEADS = 8
Q_LORA = 384
KV_LORA = 256
QK_NOPE = 64
QK_ROPE = 32
V_HEAD = 64
MLA_W = MLA_HEADS * V_HEAD
MLA_SCALE = (QK_NOPE + QK_ROPE) ** -0.5
LOG2_E = 1.4426950408889634
HG_HEADS = 8
HG_DK = 64
HG_DV = 64
HG_KW = HG_HEADS * HG_DK
HG_W = HG_HEADS * HG_DV
D_FF = 4 * D_MODEL
ROPE_BASE = 10000.0
EPS = 1e-6

LANES = 128
SUBLANES = 8
HEAD_PAD = LANES
ROPE_PAD = (QK_NOPE, HEAD_PAD - QK_NOPE - QK_ROPE)
QK_PAD_W = MLA_HEADS * HEAD_PAD
N_PAIRS = HG_HEADS // 2
VMEM_LIMIT = 56 * 1024 * 1024

_LAT_W = Q_LORA + KV_LORA
_MIX_SIZES = (HG_KW, HG_KW, HG_KW, HG_W, HG_W)
_MIX_OFFS = tuple(int(o) for o in np.cumsum((0,) + _MIX_SIZES))

HG_CHUNK = 128
HG_LEVELS = 7

TOKEN_TILE = 512
ATTN_Q_TILE = 1024
ATTN_Q_SPLIT = 2
ATTN_KV_TILES = (768, 512, 256, 128)
ADA_COL_TILE = 1536

F32 = jnp.float32
BF16 = jnp.bfloat16


def _cparams(sem):
    return pltpu.CompilerParams(dimension_semantics=sem, vmem_limit_bytes=VMEM_LIMIT)


def _rms(x, g):
    return x * lax.rsqrt(jnp.mean(x * x, axis=-1, keepdims=True) + EPS) * g


def _dot(a, b):
    return jnp.dot(a, b, preferred_element_type=F32)


def _dot_nt(a, b):
    return lax.dot_general(a, b, (((1,), (1,)), ((), ())), preferred_element_type=F32)


def _dot_tn(a, b):
    return lax.dot_general(a, b, (((0,), (0,)), ((), ())), preferred_element_type=F32)


def _split_hi_lo(x):
    hi = x.astype(BF16)
    lo = (x - hi.astype(F32)).astype(BF16)
    return hi, lo


def _ada_kernel(c_ref, w_ref, b_ref, o_ref):
    c = c_ref[...]
    a = c * jax.nn.sigmoid(c)
    a_hi, a_lo = _split_hi_lo(a)
    w_hi, w_lo = _split_hi_lo(w_ref[...])
    o_ref[...] = _dot(a_hi, w_hi) + _dot(a_hi, w_lo) + _dot(a_lo, w_hi) + b_ref[...]


def _ada(cc, w_ada, b_ada):
    rows, tn = cc.shape[0], ADA_COL_TILE
    n = w_ada.shape[1]
    return pl.pallas_call(
        _ada_kernel,
        grid=(n // tn,),
        in_specs=[pl.BlockSpec((rows, D_MODEL), lambda j: (0, 0)),
                  pl.BlockSpec((D_MODEL, tn), lambda j: (0, j)),
                  pl.BlockSpec((1, tn), lambda j: (0, j))],
        out_specs=pl.BlockSpec((rows, tn), lambda j: (0, j)),
        out_shape=jax.ShapeDtypeStruct((rows, n), F32),
        compiler_params=_cparams(("arbitrary",)),
        name="ada",
    )(cc, w_ada, b_ada)


def _rope_tile(blk, cos, sin, even):
    rot = jnp.where(even, -pltpu.roll(blk, LANES - 1, 1), pltpu.roll(blk, 1, 1))
    return blk * cos + rot * sin


def _run_staggered(stages):
    pending = stages[0][0]()
    for i, (_, consume) in enumerate(stages):
        current = pending
        if i + 1 < len(stages):
            pending = stages[i + 1][0]()
        consume(current)


def _store_kv(kv, kr, k_ref, v_ref, bi=0, rs=slice(None)):
    rows = kv.shape[0]
    nope_lane = lax.broadcasted_iota(jnp.int32, (rows, HEAD_PAD), 1) < QK_NOPE
    ones_row = jnp.where(lax.broadcasted_iota(jnp.int32, (SUBLANES, rows), 0) == 0, 1.0, 0.0)
    tail = jnp.concatenate([ones_row, jnp.zeros((HEAD_PAD - V_HEAD - SUBLANES, rows), F32)], axis=0)
    for hh in range(MLA_HEADS):
        sl = slice(hh * HEAD_PAD, (hh + 1) * HEAD_PAD)
        blk = kv[:, sl]
        k_ref[bi, rs, sl] = jnp.where(nope_lane, blk, kr).astype(BF16)
        v_t = blk.T[QK_NOPE:]
        v_ref[bi, sl, rs] = jnp.concatenate([v_t, tail], axis=0).astype(BF16)


N_INPROJ_IN = 11
N_INPROJ_OUT = 10


def _inproj_kernel(*refs, rope, cache_out, n_blocks, with_cache):
    if with_cache:
        step = pl.program_id(1)
        n_in = N_INPROJ_IN + 2 * rope
        ckvc_ref, krc_ref = refs[n_in:n_in + 2]
        refs = refs[:n_in] + refs[n_in + 2:]
        k_ref, v_ref, wukv_ref = refs[n_in + 1], refs[n_in + 2], refs[9]

        @pl.when(step >= n_blocks)
        def _():
            _store_kv(_dot(ckvc_ref[0].astype(BF16), wukv_ref[...]), krc_ref[0], k_ref, v_ref)

        pl.when(step < n_blocks)(functools.partial(_inproj_tile, refs, rope, cache_out))
    else:
        _inproj_tile(refs, rope, cache_out)


def _inproj_tile(refs, rope, cache_out):
    (x_ref, sh_ref, sc_ref, gmix_ref, wlat_ref, wmix_ref, gq_ref, wuq_ref, gkv_ref, wukv_ref, lbl_ref) = refs[:N_INPROJ_IN]
    refs = refs[N_INPROJ_IN:]
    if rope:
        cos_ref, sin_ref = refs[:2]
        refs = refs[2:]
    (q_ref, k_ref, v_ref, qh_ref, kf_ref, lff_ref, kb_ref, lfb_ref, vh_ref, sg_ref) = refs[:N_INPROJ_OUT]
    refs = refs[N_INPROJ_OUT:]
    tb, tm = x_ref.shape[:2]
    if tb > 1:
        parts = [(bi, slice(None)) for bi in range(tb)]
    elif tm == TOKEN_TILE:
        parts = [(0, slice(0, tm // 2)), (0, slice(tm // 2, tm))]
    else:
        parts = [(0, slice(None))]

    def stages_for(bi, rs):
        hb = (_rms(x_ref[bi, rs], gmix_ref[...]) * (1.0 + sc_ref[0]) + sh_ref[0]).astype(BF16)
        if rope:
            cos, sin = cos_ref[rs], sin_ref[rs]
            even = (lax.broadcasted_iota(jnp.int32, cos.shape, 1) & 1) == 0
        kept = {}

        def use_latents(y):
            kept["qn"] = _rms(y[:, :Q_LORA], gq_ref[...]).astype(BF16)
            kept["ckv"] = _rms(y[:, Q_LORA:_LAT_W], gkv_ref[...])
            kr = y[:, _LAT_W:]
            if cache_out:
                refs[0][bi, rs] = kept["ckv"]
                refs[1][bi, rs] = kr[:, ROPE_PAD[0]:ROPE_PAD[0] + QK_ROPE]
            kept["kr"] = _rope_tile(kr, cos, sin, even) if rope else kr

        def use_q(q):
            q = q * (MLA_SCALE * LOG2_E)
            for hh in range(MLA_HEADS):
                sl = slice(hh * HEAD_PAD, (hh + 1) * HEAD_PAD)
                blk = q[:, sl]
                if rope:
                    blk = _rope_tile(blk, cos, sin, even)
                q_ref[bi, sl, rs] = blk.T.astype(BF16)

        def use_hq(hq):
            qh_ref[bi, rs] = hq * jax.nn.sigmoid(hq) * (HG_DK ** -0.5)

        def use_forget(d, k_out, lf_out):
            def use(z):
                l0, l1 = lbl_ref[0, d:d + 1], lbl_ref[1, d:d + 1]
                lmax = jnp.maximum(l0, l1)
                e0, e1 = jnp.exp(l0 - lmax), jnp.exp(l1 - lmax)
                lbd = e0 / (e0 + e1)
                f = lbd + (1.0 - lbd) * jax.nn.sigmoid(z)
                k_out[bi, rs] = 1.0 - f
                lf_out[bi, rs] = jnp.log2(f)
            return use

        def use_hi(y):
            vh_ref[bi, rs] = y.astype(vh_ref.dtype)

        def use_hg(hg):
            sg_ref[bi, rs] = (hg * jax.nn.sigmoid(hg)).astype(sg_ref.dtype)

        def mix(i):
            return lambda: _dot(hb, wmix_ref[:, _MIX_OFFS[i]:_MIX_OFFS[i + 1]])

        return [
            (lambda: _dot(hb, wlat_ref[...]), use_latents),
            (mix(0), use_hq),
            (lambda: _dot(kept["qn"], wuq_ref[...]), use_q),
            (lambda: _dot(kept["ckv"].astype(BF16), wukv_ref[...]),
             lambda kv: _store_kv(kv, kept["kr"], k_ref, v_ref, bi, rs)),
            (mix(1), use_forget(0, kf_ref, lff_ref)),
            (mix(2), use_forget(1, kb_ref, lfb_ref)),
            (mix(3), use_hi),
            (mix(4), use_hg),
        ]

    per_part = [stages_for(bi, rs) for bi, rs in parts]
    _run_staggered([stage for group in zip(*per_part) for stage in group])


def _const_spec(shape):
    return pl.BlockSpec(shape, lambda *_: (0,) * len(shape), pipeline_mode=pl.Buffered(1))


def _inproj(x, mods, mod_row, wts, rope_tabs, cache_out, tm, cache, shared_mods):
    bsz, n, _ = x.shape
    rope = rope_tabs is not None
    (w_lat, w_mix), g_mix, g_q, w_uq_p, g_kv, w_ukv_p, lb_logits = wts
    n_blocks = n // tm
    past = 0 if cache is None else cache[0].shape[1]
    assert n_blocks * tm == n and past % tm == 0
    kv_rows = n + past
    tb = 2 if (shared_mods and cache is None and tm < TOKEN_TILE and bsz % 2 == 0) else 1

    def own(i):
        return i if cache is None else jnp.minimum(i, n_blocks - 1)

    def tok(width):
        return pl.BlockSpec((tb, tm, width), lambda b, i: (b, own(i), 0))

    def mod(col):
        return pl.BlockSpec((1, 1, D_MODEL), lambda b, i: (mod_row(b * tb), 0, col))

    in_specs = [tok(D_MODEL), mod(0), mod(1), _const_spec((1, D_MODEL)), _const_spec(w_lat.shape),
                _const_spec(w_mix.shape),
                _const_spec((1, Q_LORA)), _const_spec(w_uq_p.shape), _const_spec((1, KV_LORA)),
                _const_spec(w_ukv_p.shape), _const_spec(lb_logits.shape)]
    args = [x, mods, mods, g_mix, w_lat, w_mix, g_q, w_uq_p, g_kv, w_ukv_p, lb_logits]
    assert len(args) == N_INPROJ_IN
    if rope:
        in_specs += [pl.BlockSpec((tm, LANES), lambda b, i: (own(i), 0))] * 2
        args += list(rope_tabs)
    if cache is not None:
        in_specs += [pl.BlockSpec((1, tm, w), lambda b, i: (b, jnp.maximum(i - n_blocks, 0), 0))
                     for w in (KV_LORA, LANES)]
        args += list(cache)

    widths = [(QK_PAD_W, BF16)] * 3 + [(HG_KW, F32)] * 5 + [(HG_W, BF16)] * 2
    assert len(widths) == N_INPROJ_OUT
    if cache_out:
        widths += [(KV_LORA, F32), (QK_ROPE, F32)]
    out_specs = [tok(w) for w, _ in widths]
    out_shape = [jax.ShapeDtypeStruct((bsz, n, w), dt) for w, dt in widths]
    out_specs[0] = pl.BlockSpec((tb, QK_PAD_W, tm), lambda b, i: (b, 0, own(i)))
    out_shape[0] = jax.ShapeDtypeStruct((bsz, QK_PAD_W, n), BF16)
    out_specs[1] = pl.BlockSpec((tb, tm, QK_PAD_W), lambda b, i: (b, i, 0))
    out_shape[1] = jax.ShapeDtypeStruct((bsz, kv_rows, QK_PAD_W), BF16)
    out_specs[2] = pl.BlockSpec((tb, QK_PAD_W, tm), lambda b, i: (b, 0, i))
    out_shape[2] = jax.ShapeDtypeStruct((bsz, QK_PAD_W, kv_rows), BF16)
    return pl.pallas_call(
        functools.partial(_inproj_kernel, rope=rope, cache_out=cache_out, n_blocks=n_blocks,
                          with_cache=cache is not None),
        grid=(bsz // tb, kv_rows // tm),
        in_specs=in_specs, out_specs=out_specs, out_shape=out_shape,
        compiler_params=_cparams(("parallel", "arbitrary")),
        name="inproj_rope" if rope else "inproj",
    )(*args)


def _attn_kernel(q_ref, k_ref, v_ref, o_ref, m_scr, acc_scr):
    j = pl.program_id(2)

    @pl.when(j == 0)
    def _():
        m_scr[...] = jnp.full(m_scr.shape, -jnp.inf, F32)
        acc_scr[...] = jnp.zeros(acc_scr.shape, F32)

    def scores(hh):
        sl = slice(hh * HEAD_PAD, (hh + 1) * HEAD_PAD)
        return _dot(k_ref[0, :, sl], q_ref[0, sl, :])

    ahead = 2
    pending = [scores(hh) for hh in range(ahead)]
    for hh in range(MLA_HEADS):
        sl = slice(hh * HEAD_PAD, (hh + 1) * HEAD_PAD)
        s = pending.pop(0)
        if hh + ahead < MLA_HEADS:
            pending.append(scores(hh + ahead))
        bq = s.shape[1]
        n_split = min(ATTN_Q_SPLIT, bq // LANES)
        for cs in (slice(c * bq // n_split, (c + 1) * bq // n_split) for c in range(n_split)):
            s_c = s[:, cs]
            m_prev = m_scr[hh, :, cs]
            m_cur = jnp.maximum(m_prev, jnp.max(s_c, axis=0, keepdims=True))
            alpha = jnp.exp2(m_prev - m_cur)
            p = jnp.exp2(s_c - m_cur[:1])
            m_scr[hh, :, cs] = m_cur
            acc_scr[hh, :, cs] = alpha[:1] * acc_scr[hh, :, cs] + _dot(v_ref[0, sl, :], p.astype(BF16))

    @pl.when(j == pl.num_programs(2) - 1)
    def _():
        def head_out(hh):
            acc = acc_scr[hh]
            return acc[:V_HEAD] / acc[V_HEAD:V_HEAD + 1]

        for pair in range(MLA_HEADS // 2):
            both = jnp.concatenate([head_out(2 * pair), head_out(2 * pair + 1)], axis=0)
            o_ref[0, :, pair * LANES:(pair + 1) * LANES] = both.T.astype(o_ref.dtype)


def _attention(q, k, v, bq, bk):
    bsz, _, nq = q.shape
    nk = k.shape[1]
    return pl.pallas_call(
        _attn_kernel,
        grid=(bsz, nq // bq, nk // bk),
        in_specs=[pl.BlockSpec((1, QK_PAD_W, bq), lambda b, i, j: (b, 0, i)),
                  pl.BlockSpec((1, bk, QK_PAD_W), lambda b, i, j: (b, j, 0)),
                  pl.BlockSpec((1, QK_PAD_W, bk), lambda b, i, j: (b, 0, j))],
        out_specs=pl.BlockSpec((1, bq, MLA_W), lambda b, i, j: (b, i, 0)),
        out_shape=jax.ShapeDtypeStruct((bsz, nq, MLA_W), BF16),
        scratch_shapes=[pltpu.VMEM((MLA_HEADS, SUBLANES, bq), F32),
                        pltpu.VMEM((MLA_HEADS, HEAD_PAD, bq), F32)],
        compiler_params=_cparams(("parallel", "parallel", "arbitrary")),
        name="attn",
    )(q, k, v)


HG_TILE_LEVELS = 3


def _hgrn_constants(reverse):
    c = HG_CHUNK
    t = np.arange(c)
    mats = [(t[None, :] <= t[:, None])]
    for lvl in range(1, HG_TILE_LEVELS + 1):
        g, half = 1 << lvl, 1 << (lvl - 1)
        p = t % g
        mid = t - p + half - 1
        isq = p >= half
        u = t[None, :]
        mats.append(np.where(isq[:, None], (u > mid[:, None]) & (u <= t[:, None]),
                             (u > t[:, None]) & (u <= mid[:, None])))
    mstack = np.stack(mats).astype(np.float32)
    if reverse:
        mstack = mstack[:, ::-1, ::-1]
    mstack = mstack.reshape(-1, c)
    return jnp.asarray(np.concatenate([mstack, mstack], axis=1), BF16)


def _hgrn_kernel(qf_ref, qb_ref, kf_ref, lff_ref, kb_ref, lfb_ref, vf_ref, vb_ref, mstack_ref, *rest,
                 has_s0, emit_state):
    if has_s0:
        s0_ref, rest = rest[0], rest[1:]
    of_ref, ob_ref = rest[:2]
    rest = rest[2:]
    st_scr = rest[-1]
    c = HG_CHUNK
    step = pl.program_id(1)

    @pl.when(step == 0)
    def _():
        st_scr[...] = s0_ref[0] if has_s0 else jnp.zeros(st_scr.shape, F32)

    dirs = ((qf_ref, kf_ref, lff_ref, vf_ref, of_ref, False), (qb_ref, kb_ref, lfb_ref, vb_ref, ob_ref, True))
    exps = []
    for d, (_, _, lf_ref, _, _, _) in enumerate(dirs):
        lf_hi, lf_lo = _split_hi_lo(lf_ref[0])
        exps.append(_dot(mstack_ref[d], jnp.concatenate([lf_hi, lf_lo], axis=0)))

    units = [_hgrn_pair(q_ref, k_ref, v_ref, o_ref, exps[d], st_scr.at[d], pair, reverse)
             for pair in range(N_PAIRS) for d, (q_ref, k_ref, _, v_ref, o_ref, reverse) in enumerate(dirs)]
    ahead = 3
    for unit in units[:ahead]:
        next(unit)
    for i, unit in enumerate(units):
        if i + ahead < len(units):
            next(units[i + ahead])
        next(unit, None)

    if emit_state:
        sfin_ref = rest[0]

        @pl.when(step == pl.num_programs(1) - 1)
        def _():
            for d in range(2):
                for pair in range(N_PAIRS):
                    s_pair = st_scr[d, pair].T
                    sfin_ref[0, d, 2 * pair] = s_pair[:HG_DK, :HG_DV]
                    sfin_ref[0, d, 2 * pair + 1] = pltpu.roll(s_pair, HG_DV, 1)[HG_DK:, :HG_DV]


def _hgrn_pair(q_ref, k_ref, v_ref, o_ref, e, st_scr, pair, reverse):
    c = HG_CHUNK
    last = 0 if reverse else c - 1
    row = lax.broadcasted_iota(jnp.int32, (c, LANES), 0)
    lane_low = lax.broadcasted_iota(jnp.int32, (c, LANES), 1) < HG_DK
    low_b = jnp.where(lane_low, 1.0, 0.0).astype(BF16)
    high_b = jnp.where(lane_low, 0.0, 1.0).astype(BF16)
    blank_t = jnp.zeros((HG_DK, c), BF16)
    xor = lax.broadcasted_iota(jnp.int32, (c, c), 0) ^ lax.broadcasted_iota(jnp.int32, (c, c), 1)
    diag_block = (lax.broadcasted_iota(jnp.int32, (LANES, LANES), 0) < HG_DV) == (
        lax.broadcasted_iota(jnp.int32, (LANES, LANES), 1) < HG_DK)
    zeros8 = jnp.zeros((SUBLANES, LANES), F32)

    def q_side_block(b):
        return (b % 2 == 1) != reverse

    sl = slice(pair * LANES, (pair + 1) * LANES)
    q = q_ref[0, :, sl]
    k = k_ref[0, :, sl]
    v = v_ref[0, :, sl]
    cum = e[0:c, sl]
    cum_last = cum[last:last + 1]

    def blank(rows):
        return jnp.concatenate([zeros8] * (rows // SUBLANES), axis=0)

    products = []
    for lvl in range(HG_LEVELS, -1, -1):
        half = (1 << lvl) // 2
        if lvl == 0:
            lhs, zk = q.astype(BF16), k
        elif lvl > HG_TILE_LEVELS:
            ys, zs = [], []
            for b in range(0, c // half, 2):
                ref_row = (b + 1) * half if reverse else (b + 1) * half - 1
                r = cum[ref_row:ref_row + 1]
                for bb in (b, b + 1):
                    rs = slice(bb * half, (bb + 1) * half)
                    if q_side_block(bb):
                        ys.append(q[rs] * jnp.exp2(cum[rs] - r))
                        zs.append(blank(half))
                    else:
                        zs.append(k[rs] * jnp.exp2(r - cum[rs]))
            lhs = jnp.concatenate(ys, axis=0).astype(BF16)
            zk = jnp.concatenate(zs, axis=0)
        else:
            z = jnp.exp2(e[lvl * c:(lvl + 1) * c, sl])
            q_row = ((row & half) == 0) if reverse else ((row & half) != 0)
            y = jnp.where(q_row, q, k) * z
            lhs = y.astype(BF16)
            zk = jnp.where(q_row, 0.0, y)
        zt = zk.T.astype(BF16)
        rhs_t = jnp.concatenate([jnp.concatenate([zt[:HG_DK], blank_t], axis=0),
                                 jnp.concatenate([blank_t, zt[HG_DK:]], axis=0)], axis=1)
        products.append(_dot(lhs, rhs_t))
    yield
    a_cat = None
    for lvl, p in zip(range(HG_LEVELS, -1, -1), products):
        half = (1 << lvl) // 2
        if lvl > HG_TILE_LEVELS:
            q_blocks = [bb for bb in range(c // half) if q_side_block(bb)]
            new = []
            for h_idx in range(2):
                p_h = p[:, h_idx * c:(h_idx + 1) * c]
                rows = []
                for bb in range(c // half):
                    rs = slice(bb * half, (bb + 1) * half)
                    if not q_side_block(bb):
                        rows.append(blank(half) if a_cat is None else a_cat[h_idx][rs])
                        continue
                    i = q_blocks.index(bb)
                    p_blk = p_h[i * half:(i + 1) * half]
                    if a_cat is None:
                        rows.append(p_blk)
                    else:
                        rows.append(jnp.where(xor[rs] < (1 << lvl), p_blk, a_cat[h_idx][rs]))
                new.append(jnp.concatenate(rows, axis=0))
            a_cat = tuple(new)
        else:
            same = xor < (1 << lvl)
            a_cat = tuple(jnp.where(same, p[:, h_idx * c:(h_idx + 1) * c], a_cat[h_idx]) for h_idx in range(2))
    a_cat = jnp.concatenate(a_cat, axis=1)
    v_cat = jnp.concatenate([v * low_b, v * high_b], axis=0)
    st = st_scr[pair]
    o = _dot(a_cat.astype(BF16), v_cat) + _dot_nt((q * jnp.exp2(cum)).astype(BF16), st.astype(BF16))
    o_ref[0, :, sl] = o
    kd = (k * jnp.exp2(cum_last - cum)).astype(BF16)
    upd = _dot_tn(v, kd)
    st_scr[pair] = jnp.exp2(cum_last) * st + jnp.where(diag_block, upd, 0.0)


def _hgrn(qh, kf, lff, kb, lfb, vh, s0t, emit_state):
    bsz, n, _ = qh.shape
    c = HG_CHUNK
    nc = n // c
    mstack = jnp.stack([_hgrn_constants(False), _hgrn_constants(True)])
    fwd = pl.BlockSpec((1, c, HG_KW), lambda b, i: (b, i, 0))
    bwd = pl.BlockSpec((1, c, HG_KW), lambda b, i: (b, nc - 1 - i, 0))
    in_specs = [fwd, bwd, fwd, fwd, bwd, bwd, fwd, bwd, _const_spec(mstack.shape)]
    args = [qh, qh, kf, lff, kb, lfb, vh, vh, mstack]
    if s0t is not None:
        in_specs.append(pl.BlockSpec((1, 2, N_PAIRS, LANES, LANES), lambda b, i: (b, 0, 0, 0, 0)))
        args.append(s0t)
    out_specs = [fwd, bwd]
    out_shape = [jax.ShapeDtypeStruct((bsz, n, HG_W), F32)] * 2
    if emit_state:
        out_specs.append(pl.BlockSpec((1, 2, HG_HEADS, HG_DK, HG_DV), lambda b, i: (b, 0, 0, 0, 0)))
        out_shape.append(jax.ShapeDtypeStruct((bsz, 2, HG_HEADS, HG_DK, HG_DV), F32))
    return pl.pallas_call(
        functools.partial(_hgrn_kernel, has_s0=s0t is not None, emit_state=emit_state),
        grid=(bsz, nc),
        in_specs=in_specs, out_specs=out_specs, out_shape=out_shape,
        scratch_shapes=[pltpu.VMEM((2, N_PAIRS, LANES, LANES), F32)],
        compiler_params=_cparams(("parallel", "arbitrary")),
        name="hgrn",
    )(*args)


def _state_to_pairs(s):
    b = s.shape[0]
    st = jnp.swapaxes(s, -1, -2).reshape(b, N_PAIRS, 2, HG_DV, HG_DK)
    eye = jnp.eye(2, dtype=s.dtype)
    out = jnp.einsum('bpavk,ac->bpavck', st, eye)
    return out.reshape(b, N_PAIRS, 2 * HG_DV, 2 * HG_DK)


def _merge_kernel(x_ref, sh_ref, sc_ref, ga_ref, om_ref, of_ref, ob_ref, sg_ref, gmix_ref, wgate_ref, ghg_ref,
                  hmean_ref, wbm_ref, wbh_ref, wout_ref, x1_ref):
    tm = x_ref.shape[1]
    halves = [slice(i * tm // 2, (i + 1) * tm // 2) for i in range(2)]
    o = [of_ref[0, rs] + ob_ref[0, rs] for rs in halves]

    def head_mean_sq(i):
        return _dot((o[i] * o[i]).astype(BF16), hmean_ref[...])

    ms = [head_mean_sq(i) for i in range(2)]
    mla = [_dot(om_ref[0, rs], wbm_ref[...]) for rs in halves]
    hb = [(_rms(x_ref[0, rs], gmix_ref[...]) * (1.0 + sc_ref[0]) + sh_ref[0]).astype(BF16) for rs in halves]
    gates = [_dot(hb[i], wgate_ref[...]) for i in range(2)]
    hg = []
    for i, rs in enumerate(halves):
        o_hg = (o[i] * lax.rsqrt(ms[i] + EPS) * ghg_ref[...]) * sg_ref[0, rs]
        hg.append(_dot(o_hg.astype(BF16), wbh_ref[...]))
    out = []
    for i, rs in enumerate(halves):
        merged = (jax.nn.sigmoid(gates[i][:, :D_MODEL]) * mla[i]
                  + jax.nn.sigmoid(gates[i][:, D_MODEL:]) * hg[i])
        out.append(_dot(merged.astype(BF16), wout_ref[...]))
    for i, rs in enumerate(halves):
        x1_ref[0, rs] = x_ref[0, rs] + ga_ref[0] * out[i]


def _merge(x, mods, mod_row, o_mla, o_f, o_b, sg, wts, tm):
    bsz, n, _ = x.shape
    g_mix, w_gate, g_hg, hmean, w_br_mla, w_br_hg, w_out = wts

    def tok(width):
        return pl.BlockSpec((1, tm, width), lambda b, i: (b, i, 0))

    def mod(col):
        return pl.BlockSpec((1, 1, D_MODEL), lambda b, i: (mod_row(b), 0, col))

    return pl.pallas_call(
        _merge_kernel,
        grid=(bsz, n // tm),
        in_specs=[tok(D_MODEL), mod(0), mod(1), mod(2),
                  tok(MLA_W), tok(HG_W), tok(HG_W), tok(HG_W),
                  _const_spec((1, D_MODEL)), _const_spec(w_gate.shape),
                  _const_spec((1, HG_W)), _const_spec(hmean.shape), _const_spec(w_br_mla.shape),
                  _const_spec(w_br_hg.shape), _const_spec(w_out.shape)],
        out_specs=tok(D_MODEL),
        out_shape=jax.ShapeDtypeStruct((bsz, n, D_MODEL), F32),
        compiler_params=_cparams(("parallel", "parallel")),
        name="merge",
    )(x, mods, mods, mods, o_mla, o_f, o_b, sg, g_mix, w_gate, g_hg, hmean, w_br_mla, w_br_hg, w_out)


FFN_CHUNK = 1024


def _ffn_kernel(x_ref, sh_ref, sc_ref, ga_ref, gff_ref, w1_ref, w2_ref, gfin_ref, y_ref):
    tm = x_ref.shape[1]
    halves = [slice(i * tm // 2, (i + 1) * tm // 2) for i in range(2)]
    h = [(_rms(x_ref[0, rs], gff_ref[...]) * (1.0 + sc_ref[0]) + sh_ref[0]).astype(BF16) for rs in halves]

    up = [(i, c0) for c0 in range(0, D_FF, FFN_CHUNK) for i in range(2)]
    acts = [[], []]

    def use_up(i):
        def use(a):
            a = jnp.maximum(a, 0.0)
            acts[i].append((a * a).astype(BF16))
        return use

    def use_down(rs):
        def use(f):
            y_ref[0, rs] = _rms(x_ref[0, rs] + ga_ref[0] * f, gfin_ref[...])
        return use

    stages = [(functools.partial(lambda i, c0: _dot(h[i], w1_ref[:, c0:c0 + FFN_CHUNK]), i, c0), use_up(i))
              for i, c0 in up]
    stages += [(functools.partial(lambda i: _dot(jnp.concatenate(acts[i], axis=1), w2_ref[...]), i), use_down(rs))
               for i, rs in enumerate(halves)]
    _run_staggered(stages)


def _ffn(x1, mods, mod_row, wts, tm):
    bsz, n, _ = x1.shape
    g_ff, w_ff1, w_ff2, g_final = wts

    def mod(col):
        return pl.BlockSpec((1, 1, D_MODEL), lambda b, i: (mod_row(b), 0, col))

    tok = pl.BlockSpec((1, tm, D_MODEL), lambda b, i: (b, i, 0))
    return pl.pallas_call(
        _ffn_kernel,
        grid=(bsz, n // tm),
        in_specs=[tok, mod(3), mod(4), mod(5), _const_spec((1, D_MODEL)),
                  _const_spec(w_ff1.shape), _const_spec(w_ff2.shape), _const_spec((1, D_MODEL))],
        out_specs=tok,
        out_shape=jax.ShapeDtypeStruct((bsz, n, D_MODEL), F32),
        compiler_params=_cparams(("parallel", "parallel")),
        name="ffn",
    )(x1, mods, mods, mods, g_ff, w_ff1, w_ff2, g_final)


def _prep_weights(w_in, w_uq, w_ukv):
    w_lat = jnp.concatenate([w_in[:, :_LAT_W].astype(BF16),
                             jnp.pad(w_in[:, _LAT_W:_LAT_W + QK_ROPE].astype(BF16), ((0, 0), ROPE_PAD))], axis=1)
    mix0 = _LAT_W + QK_ROPE
    w_mix = w_in[:, mix0:mix0 + _MIX_OFFS[-1]].astype(BF16)
    w_gate = w_in[:, mix0 + _MIX_OFFS[-1]:].astype(BF16)
    w_in_p = (w_lat, w_mix, w_gate)
    w_uq_p = jnp.pad(w_uq.reshape(Q_LORA, MLA_HEADS, QK_NOPE + QK_ROPE).astype(BF16),
                     ((0, 0), (0, 0), (0, ROPE_PAD[1]))).reshape(Q_LORA, QK_PAD_W)
    assert QK_NOPE + V_HEAD == HEAD_PAD
    return w_in_p, w_uq_p, w_ukv.astype(BF16)


def _rope_tables(n):
    rows = n // GRID_W
    row = jnp.repeat(jnp.arange(rows, dtype=F32), GRID_W)
    col = jnp.tile(jnp.arange(GRID_W, dtype=F32), rows)
    half = QK_ROPE // 2
    inv = ROPE_BASE ** (-jnp.arange(0, half, 2, dtype=F32) / half)
    ang = jnp.concatenate([row[:, None] * inv, col[:, None] * inv], axis=-1)
    cos = jnp.repeat(jnp.cos(ang), 2, axis=-1)
    sin = jnp.repeat(jnp.sin(ang), 2, axis=-1)
    cos = jnp.pad(cos, ((0, 0), ROPE_PAD), constant_values=1.0)
    sin = jnp.pad(sin, ((0, 0), ROPE_PAD))
    return cos, sin


def _trunk(x, mods, mod_row, wts, rope_tabs, ctx):
    inproj_w, merge_w, ffn_w = wts
    bsz, n, _ = x.shape
    if ctx is None:
        cache, past = None, 0
        s0 = None
    else:
        ckv_c, krope_c, state_c = ctx
        cache, past = (ckv_c, jnp.pad(krope_c, ((0, 0), (0, 0), ROPE_PAD))), ckv_c.shape[1]
        s0 = jnp.stack([_state_to_pairs(state_c[:, 0]), _state_to_pairs(state_c[:, 1])], axis=1)
    outs = _inproj(x, mods, mod_row, inproj_w, rope_tabs, ctx is None, min(n, TOKEN_TILE), cache, ctx is None)
    q, k, v, qh, kf, lff, kb, lfb, vh, sg = outs[:N_INPROJ_OUT]
    if ctx is None:
        ckv, krope = outs[N_INPROJ_OUT:]
    bk = next(b for b in ATTN_KV_TILES if (n + past) % b == 0)
    o_mla = _attention(q, k, v, min(n, ATTN_Q_TILE), bk)
    o_f, o_b, *state = _hgrn(qh, kf, lff, kb, lfb, vh, s0, ctx is None)
    per_token = (x, o_mla, o_f, o_b, sg)
    if ctx is None:
        per_token = tuple(a.reshape(1, bsz * n, a.shape[-1]) for a in per_token)
    tm = min(per_token[0].shape[1], TOKEN_TILE)
    x1 = _merge(*per_token[:1], mods, mod_row, *per_token[1:], merge_w, tm)
    y = _ffn(x1, mods, mod_row, ffn_w, tm).reshape(bsz, n, D_MODEL)
    if ctx is None:
        return y, (ckv, krope, state[0])
    return y, None


def kernel(x_prompt, x_sample, cache_ckv, cache_krope, state_hgrn, c, c_ctx, w_ada, b_ada, g_norm_mix, g_norm_ff, w_in, g_q_norm, w_uq, g_kv_norm, w_ukv, g_hg_norm, hg_lb_logits, w_br_mla, w_br_hg, w_out, w_ff1, w_ff2, g_final):
    assert w_in.shape[0] == 1, "single-layer trunk"
    dec_b = c.shape[0]
    assert dec_b + 1 <= SUBLANES
    cc = jnp.concatenate([c, c_ctx[None, :], jnp.zeros((SUBLANES - dec_b - 1, D_MODEL), F32)], axis=0)
    mods = _ada(cc, w_ada[0], b_ada[0][None, :]).reshape(SUBLANES, 1, 6 * D_MODEL)

    (w_lat, w_mix, w_gate), w_uq_p, w_ukv_p = _prep_weights(w_in[0], w_uq[0], w_ukv[0])
    g_mix = g_norm_mix[0][None]
    inproj_w = ((w_lat, w_mix), g_mix, g_q_norm[0][None], w_uq_p, g_kv_norm[0][None], w_ukv_p, hg_lb_logits)
    head_id = np.arange(HG_W) // HG_DV
    hmean = jnp.asarray((head_id[:, None] == head_id[None, :]).astype(np.float32) / HG_DV, BF16)
    merge_w = (g_mix, w_gate, g_hg_norm[0][None], hmean, w_br_mla[0].astype(BF16), w_br_hg[0].astype(BF16),
               w_out[0].astype(BF16))
    ffn_w = (g_norm_ff[0][None], w_ff1[0].astype(BF16), w_ff2[0].astype(BF16), g_final[None])
    wts = (inproj_w, merge_w, ffn_w)

    y_prompt, (ckv, krope, state) = _trunk(x_prompt, mods, lambda b: dec_b, wts, None, None)
    rope_tabs = _rope_tables(x_sample.shape[1])
    y_sample, _ = _trunk(x_sample, mods, lambda b: b, wts, rope_tabs,
                         (cache_ckv[:, 0], cache_krope[:, 0], state_hgrn[:, 0]))
    return (y_prompt, y_sample, ckv[:, None], krope[:, None], state[:, None])
```

```python
import functools

import numpy as np
import jax
import jax.numpy as jnp
from jax import lax
from jax.experimental import pallas as pl
from jax.experimental.pallas import tpu as pltpu

D_MODEL = 1024
GRID_W = 64
MLA_HEADS = 8
Q_LORA = 384
KV_LORA = 256
QK_NOPE = 64
QK_ROPE = 32
V_HEAD = 64
MLA_W = MLA_HEADS * V_HEAD
MLA_SCALE = (QK_NOPE + QK_ROPE) ** -0.5
LOG2_E = 1.4426950408889634
HG_HEADS = 8
HG_DK = 64
HG_DV = 64
HG_KW = HG_HEADS * HG_DK
HG_W = HG_HEADS * HG_DV
D_FF = 4 * D_MODEL
ROPE_BASE = 10000.0
EPS = 1e-6

LANES = 128
SUBLANES = 8
HEAD_PAD = LANES
ROPE_PAD = (QK_NOPE, HEAD_PAD - QK_NOPE - QK_ROPE)
QK_PAD_W = MLA_HEADS * HEAD_PAD
N_PAIRS = HG_HEADS // 2
VMEM_LIMIT = 56 * 1024 * 1024

_LAT_W = Q_LORA + KV_LORA
_MIX_SIZES = (HG_KW, HG_KW, HG_KW, HG_W, HG_W)
_MIX_OFFS = tuple(int(o) for o in np.cumsum((0,) + _MIX_SIZES))

HG_CHUNK = 128
HG_STEP_CHUNKS = 4
HG_LEVELS = 7

TOKEN_TILE = 512
ATTN_Q_TILE = 1024
ATTN_Q_SPLIT = 2
ATTN_KV_TILES = (768, 512, 256, 128)
ADA_COL_TILE = 1536

F32 = jnp.float32
BF16 = jnp.bfloat16


def _cparams(sem):
    return pltpu.CompilerParams(dimension_semantics=sem, vmem_limit_bytes=VMEM_LIMIT)


def _rms(x, g):
    return x * lax.rsqrt(jnp.mean(x * x, axis=-1, keepdims=True) + EPS) * g


def _dot(a, b):
    return jnp.dot(a, b, preferred_element_type=F32)


def _dot_nt(a, b):
    return lax.dot_general(a, b, (((1,), (1,)), ((), ())), preferred_element_type=F32)


def _dot_tn(a, b):
    return lax.dot_general(a, b, (((0,), (0,)), ((), ())), preferred_element_type=F32)


def _split_hi_lo(x):
    hi = x.astype(BF16)
    lo = (x - hi.astype(F32)).astype(BF16)
    return hi, lo


def _ada_kernel(c_ref, w_ref, b_ref, o_ref):
    c = c_ref[...]
    a = c * jax.nn.sigmoid(c)
    a_hi, a_lo = _split_hi_lo(a)
    w_hi, w_lo = _split_hi_lo(w_ref[...])
    o_ref[...] = _dot(a_hi, w_hi) + _dot(a_hi, w_lo) + _dot(a_lo, w_hi) + b_ref[...]


def _ada(cc, w_ada, b_ada):
    rows, tn = cc.shape[0], ADA_COL_TILE
    n = w_ada.shape[1]
    return pl.pallas_call(
        _ada_kernel,
        grid=(n // tn,),
        in_specs=[pl.BlockSpec((rows, D_MODEL), lambda j: (0, 0)),
                  pl.BlockSpec((D_MODEL, tn), lambda j: (0, j)),
                  pl.BlockSpec((1, tn), lambda j: (0, j))],
        out_specs=pl.BlockSpec((rows, tn), lambda j: (0, j)),
        out_shape=jax.ShapeDtypeStruct((rows, n), F32),
        compiler_params=_cparams(("arbitrary",)),
        name="ada",
    )(cc, w_ada, b_ada)


def _rope_tile(blk, cos, sin, even):
    rot = jnp.where(even, -pltpu.roll(blk, LANES - 1, 1), pltpu.roll(blk, 1, 1))
    return blk * cos + rot * sin


def _run_staggered(stages):
    pending = stages[0][0]()
    for i, (_, consume) in enumerate(stages):
        current = pending
        if i + 1 < len(stages):
            pending = stages[i + 1][0]()
        consume(current)


def _store_kv(kv, kr, k_ref, v_ref, bi=0, rs=slice(None)):
    rows = kv.shape[0]
    nope_lane = lax.broadcasted_iota(jnp.int32, (rows, HEAD_PAD), 1) < QK_NOPE
    ones_row = jnp.where(lax.broadcasted_iota(jnp.int32, (SUBLANES, rows), 0) == 0, 1.0, 0.0)
    tail = jnp.concatenate([ones_row, jnp.zeros((HEAD_PAD - V_HEAD - SUBLANES, rows), F32)], axis=0)
    for hh in range(MLA_HEADS):
        sl = slice(hh * HEAD_PAD, (hh + 1) * HEAD_PAD)
        blk = kv[:, sl]
        k_ref[bi, rs, sl] = jnp.where(nope_lane, blk, kr).astype(BF16)
        v_t = blk.T[QK_NOPE:]
        v_ref[bi, sl, rs] = jnp.concatenate([v_t, tail], axis=0).astype(BF16)


N_INPROJ_IN = 11
N_INPROJ_OUT = 10


def _inproj_kernel(*refs, rope, cache_out, n_blocks, with_cache):
    if with_cache:
        step = pl.program_id(1)
        n_in = N_INPROJ_IN + 2 * rope
        ckvc_ref, krc_ref = refs[n_in:n_in + 2]
        refs = refs[:n_in] + refs[n_in + 2:]
        k_ref, v_ref, wukv_ref = refs[n_in + 1], refs[n_in + 2], refs[9]

        @pl.when(step >= n_blocks)
        def _():
            _store_kv(_dot(ckvc_ref[0].astype(BF16), wukv_ref[...]), krc_ref[0], k_ref, v_ref)

        pl.when(step < n_blocks)(functools.partial(_inproj_tile, refs, rope, cache_out))
    else:
        _inproj_tile(refs, rope, cache_out)


def _inproj_tile(refs, rope, cache_out):
    (x_ref, sh_ref, sc_ref, gmix_ref, wlat_ref, wmix_ref, gq_ref, wuq_ref, gkv_ref, wukv_ref, lbl_ref) = refs[:N_INPROJ_IN]
    refs = refs[N_INPROJ_IN:]
    if rope:
        cos_ref, sin_ref = refs[:2]
        refs = refs[2:]
    (q_ref, k_ref, v_ref, qh_ref, kf_ref, lff_ref, kb_ref, lfb_ref, vh_ref, sg_ref) = refs[:N_INPROJ_OUT]
    refs = refs[N_INPROJ_OUT:]
    tb, tm = x_ref.shape[:2]
    if tb > 1:
        parts = [(bi, slice(None)) for bi in range(tb)]
    elif tm == TOKEN_TILE:
        parts = [(0, slice(0, tm // 2)), (0, slice(tm // 2, tm))]
    else:
        parts = [(0, slice(None))]

    def stages_for(bi, rs):
        hb = (_rms(x_ref[bi, rs], gmix_ref[...]) * (1.0 + sc_ref[0]) + sh_ref[0]).astype(BF16)
        if rope:
            cos, sin = cos_ref[rs], sin_ref[rs]
            even = (lax.broadcasted_iota(jnp.int32, cos.shape, 1) & 1) == 0
        kept = {}

        def use_latents(y):
            kept["qn"] = _rms(y[:, :Q_LORA], gq_ref[...]).astype(BF16)
            kept["ckv"] = _rms(y[:, Q_LORA:_LAT_W], gkv_ref[...])
            kr = y[:, _LAT_W:]
            if cache_out:
                refs[0][bi, rs] = kept["ckv"]
                refs[1][bi, rs] = kr[:, ROPE_PAD[0]:ROPE_PAD[0] + QK_ROPE]
            kept["kr"] = _rope_tile(kr, cos, sin, even) if rope else kr

        def use_q(q):
            q = q * (MLA_SCALE * LOG2_E)
            for hh in range(MLA_HEADS):
                sl = slice(hh * HEAD_PAD, (hh + 1) * HEAD_PAD)
                blk = q[:, sl]
                if rope:
                    blk = _rope_tile(blk, cos, sin, even)
                q_ref[bi, sl, rs] = blk.T.astype(BF16)

        def use_hq(hq):
            qh_ref[bi, rs] = hq * jax.nn.sigmoid(hq) * (HG_DK ** -0.5)

        def use_forget(d, k_out, lf_out):
            def use(z):
                l0, l1 = lbl_ref[0, d:d + 1], lbl_ref[1, d:d + 1]
                lmax = jnp.maximum(l0, l1)
                e0, e1 = jnp.exp(l0 - lmax), jnp.exp(l1 - lmax)
                lbd = e0 / (e0 + e1)
                f = lbd + (1.0 - lbd) * jax.nn.sigmoid(z)
                k_out[bi, rs] = 1.0 - f
                lf_out[bi, rs] = jnp.log2(f)
            return use

        def use_hi(y):
            vh_ref[bi, rs] = y.astype(vh_ref.dtype)

        def use_hg(hg):
            sg_ref[bi, rs] = (hg * jax.nn.sigmoid(hg)).astype(sg_ref.dtype)

        def mix(i):
            return lambda: _dot(hb, wmix_ref[:, _MIX_OFFS[i]:_MIX_OFFS[i + 1]])

        return [
            (lambda: _dot(hb, wlat_ref[...]), use_latents),
            (mix(0), use_hq),
            (lambda: _dot(kept["qn"], wuq_ref[...]), use_q),
            (lambda: _dot(kept["ckv"].astype(BF16), wukv_ref[...]),
             lambda kv: _store_kv(kv, kept["kr"], k_ref, v_ref, bi, rs)),
            (mix(1), use_forget(0, kf_ref, lff_ref)),
            (mix(2), use_forget(1, kb_ref, lfb_ref)),
            (mix(3), use_hi),
            (mix(4), use_hg),
        ]

    per_part = [stages_for(bi, rs) for bi, rs in parts]
    _run_staggered([stage for group in zip(*per_part) for stage in group])


def _const_spec(shape):
    return pl.BlockSpec(shape, lambda *_: (0,) * len(shape), pipeline_mode=pl.Buffered(1))


def _inproj(x, mods, mod_row, wts, rope_tabs, cache_out, tm, cache, shared_mods):
    bsz, n, _ = x.shape
    rope = rope_tabs is not None
    (w_lat, w_mix), g_mix, g_q, w_uq_p, g_kv, w_ukv_p, lb_logits = wts
    n_blocks = n // tm
    past = 0 if cache is None else cache[0].shape[1]
    assert n_blocks * tm == n and past % tm == 0
    kv_rows = n + past
    tb = 2 if (shared_mods and cache is None and tm < TOKEN_TILE and bsz % 2 == 0) else 1

    def own(i):
        return i if cache is None else jnp.minimum(i, n_blocks - 1)

    def tok(width):
        return pl.BlockSpec((tb, tm, width), lambda b, i: (b, own(i), 0))

    def mod(col):
        return pl.BlockSpec((1, 1, D_MODEL), lambda b, i: (mod_row(b * tb), 0, col))

    in_specs = [tok(D_MODEL), mod(0), mod(1), _const_spec((1, D_MODEL)), _const_spec(w_lat.shape),
                _const_spec(w_mix.shape),
                _const_spec((1, Q_LORA)), _const_spec(w_uq_p.shape), _const_spec((1, KV_LORA)),
                _const_spec(w_ukv_p.shape), _const_spec(lb_logits.shape)]
    args = [x, mods, mods, g_mix, w_lat, w_mix, g_q, w_uq_p, g_kv, w_ukv_p, lb_logits]
    assert len(args) == N_INPROJ_IN
    if rope:
        in_specs += [pl.BlockSpec((tm, LANES), lambda b, i: (own(i), 0))] * 2
        args += list(rope_tabs)
    if cache is not None:
        in_specs += [pl.BlockSpec((1, tm, w), lambda b, i: (b, jnp.maximum(i - n_blocks, 0), 0))
                     for w in (KV_LORA, LANES)]
        args += list(cache)

    widths = [(QK_PAD_W, BF16)] * 3 + [(HG_KW, F32)] * 5 + [(HG_W, BF16)] * 2
    assert len(widths) == N_INPROJ_OUT
    if cache_out:
        widths += [(KV_LORA, F32), (QK_ROPE, F32)]
    out_specs = [tok(w) for w, _ in widths]
    out_shape = [jax.ShapeDtypeStruct((bsz, n, w), dt) for w, dt in widths]
    out_specs[0] = pl.BlockSpec((tb, QK_PAD_W, tm), lambda b, i: (b, 0, own(i)))
    out_shape[0] = jax.ShapeDtypeStruct((bsz, QK_PAD_W, n), BF16)
    out_specs[1] = pl.BlockSpec((tb, tm, QK_PAD_W), lambda b, i: (b, i, 0))
    out_shape[1] = jax.ShapeDtypeStruct((bsz, kv_rows, QK_PAD_W), BF16)
    out_specs[2] = pl.BlockSpec((tb, QK_PAD_W, tm), lambda b, i: (b, 0, i))
    out_shape[2] = jax.ShapeDtypeStruct((bsz, QK_PAD_W, kv_rows), BF16)
    return pl.pallas_call(
        functools.partial(_inproj_kernel, rope=rope, cache_out=cache_out, n_blocks=n_blocks,
                          with_cache=cache is not None),
        grid=(bsz // tb, kv_rows // tm),
        in_specs=in_specs, out_specs=out_specs, out_shape=out_shape,
        compiler_params=_cparams(("parallel", "arbitrary")),
        name="inproj_rope" if rope else "inproj",
    )(*args)


def _attn_kernel(q_ref, k_ref, v_ref, o_ref, m_scr, acc_scr):
    j = pl.program_id(2)

    @pl.when(j == 0)
    def _():
        m_scr[...] = jnp.full(m_scr.shape, -jnp.inf, F32)
        acc_scr[...] = jnp.zeros(acc_scr.shape, F32)

    def scores(hh):
        sl = slice(hh * HEAD_PAD, (hh + 1) * HEAD_PAD)
        return _dot(k_ref[0, :, sl], q_ref[0, sl, :])

    ahead = 2
    pending = [scores(hh) for hh in range(ahead)]
    for hh in range(MLA_HEADS):
        sl = slice(hh * HEAD_PAD, (hh + 1) * HEAD_PAD)
        s = pending.pop(0)
        if hh + ahead < MLA_HEADS:
            pending.append(scores(hh + ahead))
        bq = s.shape[1]
        n_split = min(ATTN_Q_SPLIT, bq // LANES)
        for cs in (slice(c * bq // n_split, (c + 1) * bq // n_split) for c in range(n_split)):
            s_c = s[:, cs]
            m_prev = m_scr[hh, :, cs]
            m_cur = jnp.maximum(m_prev, jnp.max(s_c, axis=0, keepdims=True))
            alpha = jnp.exp2(m_prev - m_cur)
            p = jnp.exp2(s_c - m_cur[:1])
            m_scr[hh, :, cs] = m_cur
            acc_scr[hh, :, cs] = alpha[:1] * acc_scr[hh, :, cs] + _dot(v_ref[0, sl, :], p.astype(BF16))

    @pl.when(j == pl.num_programs(2) - 1)
    def _():
        def head_out(hh):
            acc = acc_scr[hh]
            return acc[:V_HEAD] / acc[V_HEAD:V_HEAD + 1]

        for pair in range(MLA_HEADS // 2):
            both = jnp.concatenate([head_out(2 * pair), head_out(2 * pair + 1)], axis=0)
            o_ref[0, :, pair * LANES:(pair + 1) * LANES] = both.T.astype(o_ref.dtype)


def _attention(q, k, v, bq, bk):
    bsz, _, nq = q.shape
    nk = k.shape[1]
    return pl.pallas_call(
        _attn_kernel,
        grid=(bsz, nq // bq, nk // bk),
        in_specs=[pl.BlockSpec((1, QK_PAD_W, bq), lambda b, i, j: (b, 0, i)),
                  pl.BlockSpec((1, bk, QK_PAD_W), lambda b, i, j: (b, j, 0)),
                  pl.BlockSpec((1, QK_PAD_W, bk), lambda b, i, j: (b, 0, j))],
        out_specs=pl.BlockSpec((1, bq, MLA_W), lambda b, i, j: (b, i, 0)),
        out_shape=jax.ShapeDtypeStruct((bsz, nq, MLA_W), BF16),
        scratch_shapes=[pltpu.VMEM((MLA_HEADS, SUBLANES, bq), F32),
                        pltpu.VMEM((MLA_HEADS, HEAD_PAD, bq), F32)],
        compiler_params=_cparams(("parallel", "parallel", "arbitrary")),
        name="attn",
    )(q, k, v)


HG_TILE_LEVELS = 3


def _hgrn_constants(reverse):
    c = HG_CHUNK
    t = np.arange(c)
    mats = [(t[None, :] <= t[:, None])]
    for lvl in range(1, HG_TILE_LEVELS + 1):
        g, half = 1 << lvl, 1 << (lvl - 1)
        p = t % g
        mid = t - p + half - 1
        isq = p >= half
        u = t[None, :]
        mats.append(np.where(isq[:, None], (u > mid[:, None]) & (u <= t[:, None]),
                             (u > t[:, None]) & (u <= mid[:, None])))
    mstack = np.stack(mats).astype(np.float32)
    if reverse:
        mstack = mstack[:, ::-1, ::-1]
    mstack = mstack.reshape(-1, c)
    return jnp.asarray(np.concatenate([mstack, mstack], axis=1), BF16)


def _hgrn_kernel(qf_ref, qb_ref, kf_ref, lff_ref, kb_ref, lfb_ref, vf_ref, vb_ref, mstack_ref, *rest,
                 has_s0, emit_state):
    if has_s0:
        s0_ref, rest = rest[0], rest[1:]
    of_ref, ob_ref = rest[:2]
    rest = rest[2:]
    st_scr = rest[-1]
    c = HG_CHUNK
    n_sub = qf_ref.shape[1] // c
    step = pl.program_id(1)

    @pl.when(step == 0)
    def _():
        st_scr[...] = s0_ref[0] if has_s0 else jnp.zeros(st_scr.shape, F32)

    dirs = ((qf_ref, kf_ref, lff_ref, vf_ref, of_ref, False), (qb_ref, kb_ref, lfb_ref, vb_ref, ob_ref, True))
    exps = {}

    def exponents(d, lf_ref, rows):
        if (d, rows.start) not in exps:
            lf_hi, lf_lo = _split_hi_lo(lf_ref[0, rows])
            exps[d, rows.start] = _dot(mstack_ref[d], jnp.concatenate([lf_hi, lf_lo], axis=0))
        return exps[d, rows.start]

    units = []
    for j in range(n_sub):
        for pair in range(N_PAIRS):
            for d, (q_ref, k_ref, lf_ref, v_ref, o_ref, reverse) in enumerate(dirs):
                jj = n_sub - 1 - j if reverse else j
                rows = slice(jj * c, (jj + 1) * c)
                units.append(_hgrn_pair(q_ref, k_ref, v_ref, o_ref, rows,
                                        functools.partial(exponents, d, lf_ref, rows), st_scr.at[d], pair, reverse))
    ahead = 3
    for unit in units[:ahead]:
        next(unit)
    for i, unit in enumerate(units):
        if i + ahead < len(units):
            next(units[i + ahead])
        next(unit, None)

    if emit_state:
        sfin_ref = rest[0]

        @pl.when(step == pl.num_programs(1) - 1)
        def _():
            for d in range(2):
                for pair in range(N_PAIRS):
                    s_pair = st_scr[d, pair].T
                    sfin_ref[0, d, 2 * pair] = s_pair[:HG_DK, :HG_DV]
                    sfin_ref[0, d, 2 * pair + 1] = pltpu.roll(s_pair, HG_DV, 1)[HG_DK:, :HG_DV]


def _hgrn_pair(q_ref, k_ref, v_ref, o_ref, tok_rows, get_exponents, st_scr, pair, reverse):
    c = HG_CHUNK
    e = get_exponents()
    last = 0 if reverse else c - 1
    row = lax.broadcasted_iota(jnp.int32, (c, LANES), 0)
    lane_low = lax.broadcasted_iota(jnp.int32, (c, LANES), 1) < HG_DK
    low_b = jnp.where(lane_low, 1.0, 0.0).astype(BF16)
    high_b = jnp.where(lane_low, 0.0, 1.0).astype(BF16)
    blank_t = jnp.zeros((HG_DK, c), BF16)
    xor = lax.broadcasted_iota(jnp.int32, (c, c), 0) ^ lax.broadcasted_iota(jnp.int32, (c, c), 1)
    diag_block = (lax.broadcasted_iota(jnp.int32, (LANES, LANES), 0) < HG_DV) == (
        lax.broadcasted_iota(jnp.int32, (LANES, LANES), 1) < HG_DK)
    zeros8 = jnp.zeros((SUBLANES, LANES), F32)

    def q_side_block(b):
        return (b % 2 == 1) != reverse

    sl = slice(pair * LANES, (pair + 1) * LANES)
    q = q_ref[0, tok_rows, sl]
    k = k_ref[0, tok_rows, sl]
    v = v_ref[0, tok_rows, sl]
    cum = e[0:c, sl]
    cum_last = cum[last:last + 1]

    def blank(rows):
        return jnp.concatenate([zeros8] * (rows // SUBLANES), axis=0)

    products = []
    for lvl in range(HG_LEVELS, -1, -1):
        half = (1 << lvl) // 2
        if lvl == 0:
            lhs, zk = q.astype(BF16), k
        elif lvl > HG_TILE_LEVELS:
            ys, zs = [], []
            for b in range(0, c // half, 2):
                ref_row = (b + 1) * half if reverse else (b + 1) * half - 1
                r = cum[ref_row:ref_row + 1]
                for bb in (b, b + 1):
                    rs = slice(bb * half, (bb + 1) * half)
                    if q_side_block(bb):
                        ys.append(q[rs] * jnp.exp2(cum[rs] - r))
                        zs.append(blank(half))
                    else:
                        zs.append(k[rs] * jnp.exp2(r - cum[rs]))
            lhs = jnp.concatenate(ys, axis=0).astype(BF16)
            zk = jnp.concatenate(zs, axis=0)
        else:
            z = jnp.exp2(e[lvl * c:(lvl + 1) * c, sl])
            q_row = ((row & half) == 0) if reverse else ((row & half) != 0)
            y = jnp.where(q_row, q, k) * z
            lhs = y.astype(BF16)
            zk = jnp.where(q_row, 0.0, y)
        zt = zk.T.astype(BF16)
        rhs_t = jnp.concatenate([jnp.concatenate([zt[:HG_DK], blank_t], axis=0),
                                 jnp.concatenate([blank_t, zt[HG_DK:]], axis=0)], axis=1)
        products.append(_dot(lhs, rhs_t))
    yield
    a_cat = None
    for lvl, p in zip(range(HG_LEVELS, -1, -1), products):
        half = (1 << lvl) // 2
        if lvl > HG_TILE_LEVELS:
            q_blocks = [bb for bb in range(c // half) if q_side_block(bb)]
            new = []
            for h_idx in range(2):
                p_h = p[:, h_idx * c:(h_idx + 1) * c]
                rows = []
                for bb in range(c // half):
                    rs = slice(bb * half, (bb + 1) * half)
                    if not q_side_block(bb):
                        rows.append(blank(half) if a_cat is None else a_cat[h_idx][rs])
                        continue
                    i = q_blocks.index(bb)
                    p_blk = p_h[i * half:(i + 1) * half]
                    if a_cat is None:
                        rows.append(p_blk)
                    else:
                        rows.append(jnp.where(xor[rs] < (1 << lvl), p_blk, a_cat[h_idx][rs]))
                new.append(jnp.concatenate(rows, axis=0))
            a_cat = tuple(new)
        else:
            same = xor < (1 << lvl)
            a_cat = tuple(jnp.where(same, p[:, h_idx * c:(h_idx + 1) * c], a_cat[h_idx]) for h_idx in range(2))
    a_cat = jnp.concatenate(a_cat, axis=1)
    v_cat = jnp.concatenate([v * low_b, v * high_b], axis=0)
    st = st_scr[pair]
    o = _dot(a_cat.astype(BF16), v_cat) + _dot_nt((q * jnp.exp2(cum)).astype(BF16), st.astype(BF16))
    o_ref[0, tok_rows, sl] = o
    kd = (k * jnp.exp2(cum_last - cum)).astype(BF16)
    upd = _dot_tn(v, kd)
    st_scr[pair] = jnp.exp2(cum_last) * st + jnp.where(diag_block, upd, 0.0)


def _hgrn(qh, kf, lff, kb, lfb, vh, s0t, emit_state):
    bsz, n, _ = qh.shape
    rows = HG_CHUNK * min(HG_STEP_CHUNKS, n // HG_CHUNK)
    nc = n // rows
    assert nc * rows == n
    mstack = jnp.stack([_hgrn_constants(False), _hgrn_constants(True)])
    fwd = pl.BlockSpec((1, rows, HG_KW), lambda b, i: (b, i, 0))
    bwd = pl.BlockSpec((1, rows, HG_KW), lambda b, i: (b, nc - 1 - i, 0))
    in_specs = [fwd, bwd, fwd, fwd, bwd, bwd, fwd, bwd, _const_spec(mstack.shape)]
    args = [qh, qh, kf, lff, kb, lfb, vh, vh, mstack]
    if s0t is not None:
        in_specs.append(pl.BlockSpec((1, 2, N_PAIRS, LANES, LANES), lambda b, i: (b, 0, 0, 0, 0)))
        args.append(s0t)
    out_specs = [fwd, bwd]
    out_shape = [jax.ShapeDtypeStruct((bsz, n, HG_W), F32)] * 2
    if emit_state:
        out_specs.append(pl.BlockSpec((1, 2, HG_HEADS, HG_DK, HG_DV), lambda b, i: (b, 0, 0, 0, 0)))
        out_shape.append(jax.ShapeDtypeStruct((bsz, 2, HG_HEADS, HG_DK, HG_DV), F32))
    return pl.pallas_call(
        functools.partial(_hgrn_kernel, has_s0=s0t is not None, emit_state=emit_state),
        grid=(bsz, nc),
        in_specs=in_specs, out_specs=out_specs, out_shape=out_shape,
        scratch_shapes=[pltpu.VMEM((2, N_PAIRS, LANES, LANES), F32)],
        compiler_params=_cparams(("parallel", "arbitrary")),
        name="hgrn",
    )(*args)


def _state_to_pairs(s):
    b = s.shape[0]
    st = jnp.swapaxes(s, -1, -2).reshape(b, N_PAIRS, 2, HG_DV, HG_DK)
    eye = jnp.eye(2, dtype=s.dtype)
    out = jnp.einsum('bpavk,ac->bpavck', st, eye)
    return out.reshape(b, N_PAIRS, 2 * HG_DV, 2 * HG_DK)


def _merge_kernel(x_ref, sh_ref, sc_ref, ga_ref, om_ref, of_ref, ob_ref, sg_ref, gmix_ref, wgate_ref, ghg_ref,
                  hmean_ref, wbm_ref, wbh_ref, wout_ref, x1_ref):
    tm = x_ref.shape[1]
    halves = [slice(i * tm // 2, (i + 1) * tm // 2) for i in range(2)]
    o = [of_ref[0, rs] + ob_ref[0, rs] for rs in halves]

    def head_mean_sq(i):
        return _dot((o[i] * o[i]).astype(BF16), hmean_ref[...])

    ms = [head_mean_sq(i) for i in range(2)]
    mla = [_dot(om_ref[0, rs], wbm_ref[...]) for rs in halves]
    hb = [(_rms(x_ref[0, rs], gmix_ref[...]) * (1.0 + sc_ref[0]) + sh_ref[0]).astype(BF16) for rs in halves]
    gates = [_dot(hb[i], wgate_ref[...]) for i in range(2)]
    hg = []
    for i, rs in enumerate(halves):
        o_hg = (o[i] * lax.rsqrt(ms[i] + EPS) * ghg_ref[...]) * sg_ref[0, rs]
        hg.append(_dot(o_hg.astype(BF16), wbh_ref[...]))
    out = []
    for i, rs in enumerate(halves):
        merged = (jax.nn.sigmoid(gates[i][:, :D_MODEL]) * mla[i]
                  + jax.nn.sigmoid(gates[i][:, D_MODEL:]) * hg[i])
        out.append(_dot(merged.astype(BF16), wout_ref[...]))
    for i, rs in enumerate(halves):
        x1_ref[0, rs] = x_ref[0, rs] + ga_ref[0] * out[i]


def _merge(x, mods, mod_row, o_mla, o_f, o_b, sg, wts, tm):
    bsz, n, _ = x.shape
    g_mix, w_gate, g_hg, hmean, w_br_mla, w_br_hg, w_out = wts

    def tok(width):
        return pl.BlockSpec((1, tm, width), lambda b, i: (b, i, 0))

    def mod(col):
        return pl.BlockSpec((1, 1, D_MODEL), lambda b, i: (mod_row(b), 0, col))

    return pl.pallas_call(
        _merge_kernel,
        grid=(bsz, n // tm),
        in_specs=[tok(D_MODEL), mod(0), mod(1), mod(2),
                  tok(MLA_W), tok(HG_W), tok(HG_W), tok(HG_W),
                  _const_spec((1, D_MODEL)), _const_spec(w_gate.shape),
                  _const_spec((1, HG_W)), _const_spec(hmean.shape), _const_spec(w_br_mla.shape),
                  _const_spec(w_br_hg.shape), _const_spec(w_out.shape)],
        out_specs=tok(D_MODEL),
        out_shape=jax.ShapeDtypeStruct((bsz, n, D_MODEL), F32),
        compiler_params=_cparams(("parallel", "parallel")),
        name="merge",
    )(x, mods, mods, mods, o_mla, o_f, o_b, sg, g_mix, w_gate, g_hg, hmean, w_br_mla, w_br_hg, w_out)


FFN_CHUNK = 1024


def _ffn_kernel(x_ref, sh_ref, sc_ref, ga_ref, gff_ref, w1_ref, w2_ref, gfin_ref, y_ref):
    tm = x_ref.shape[1]
    halves = [slice(i * tm // 2, (i + 1) * tm // 2) for i in range(2)]
    h = [(_rms(x_ref[0, rs], gff_ref[...]) * (1.0 + sc_ref[0]) + sh_ref[0]).astype(BF16) for rs in halves]

    up = [(i, c0) for c0 in range(0, D_FF, FFN_CHUNK) for i in range(2)]
    acts = [[], []]

    def use_up(i):
        def use(a):
            a = jnp.maximum(a, 0.0)
            acts[i].append((a * a).astype(BF16))
        return use

    def use_down(rs):
        def use(f):
            y_ref[0, rs] = _rms(x_ref[0, rs] + ga_ref[0] * f, gfin_ref[...])
        return use

    stages = [(functools.partial(lambda i, c0: _dot(h[i], w1_ref[:, c0:c0 + FFN_CHUNK]), i, c0), use_up(i))
              for i, c0 in up]
    stages += [(functools.partial(lambda i: _dot(jnp.concatenate(acts[i], axis=1), w2_ref[...]), i), use_down(rs))
               for i, rs in enumerate(halves)]
    _run_staggered(stages)


def _ffn(x1, mods, mod_row, wts, tm):
    bsz, n, _ = x1.shape
    g_ff, w_ff1, w_ff2, g_final = wts

    def mod(col):
        return pl.BlockSpec((1, 1, D_MODEL), lambda b, i: (mod_row(b), 0, col))

    tok = pl.BlockSpec((1, tm, D_MODEL), lambda b, i: (b, i, 0))
    return pl.pallas_call(
        _ffn_kernel,
        grid=(bsz, n // tm),
        in_specs=[tok, mod(3), mod(4), mod(5), _const_spec((1, D_MODEL)),
                  _const_spec(w_ff1.shape), _const_spec(w_ff2.shape), _const_spec((1, D_MODEL))],
        out_specs=tok,
        out_shape=jax.ShapeDtypeStruct((bsz, n, D_MODEL), F32),
        compiler_params=_cparams(("parallel", "parallel")),
        name="ffn",
    )(x1, mods, mods, mods, g_ff, w_ff1, w_ff2, g_final)


def _prep_weights(w_in, w_uq, w_ukv):
    w_lat = jnp.concatenate([w_in[:, :_LAT_W].astype(BF16),
                             jnp.pad(w_in[:, _LAT_W:_LAT_W + QK_ROPE].astype(BF16), ((0, 0), ROPE_PAD))], axis=1)
    mix0 = _LAT_W + QK_ROPE
    w_mix = w_in[:, mix0:mix0 + _MIX_OFFS[-1]].astype(BF16)
    w_gate = w_in[:, mix0 + _MIX_OFFS[-1]:].astype(BF16)
    w_in_p = (w_lat, w_mix, w_gate)
    w_uq_p = jnp.pad(w_uq.reshape(Q_LORA, MLA_HEADS, QK_NOPE + QK_ROPE).astype(BF16),
                     ((0, 0), (0, 0), (0, ROPE_PAD[1]))).reshape(Q_LORA, QK_PAD_W)
    assert QK_NOPE + V_HEAD == HEAD_PAD
    return w_in_p, w_uq_p, w_ukv.astype(BF16)


def _rope_tables(n):
    rows = n // GRID_W
    row = jnp.repeat(jnp.arange(rows, dtype=F32), GRID_W)
    col = jnp.tile(jnp.arange(GRID_W, dtype=F32), rows)
    half = QK_ROPE // 2
    inv = ROPE_BASE ** (-jnp.arange(0, half, 2, dtype=F32) / half)
    ang = jnp.concatenate([row[:, None] * inv, col[:, None] * inv], axis=-1)
    cos = jnp.repeat(jnp.cos(ang), 2, axis=-1)
    sin = jnp.repeat(jnp.sin(ang), 2, axis=-1)
    cos = jnp.pad(cos, ((0, 0), ROPE_PAD), constant_values=1.0)
    sin = jnp.pad(sin, ((0, 0), ROPE_PAD))
    return cos, sin


def _trunk(x, mods, mod_row, wts, rope_tabs, ctx):
    inproj_w, merge_w, ffn_w = wts
    bsz, n, _ = x.shape
    if ctx is None:
        cache, past = None, 0
        s0 = None
    else:
        ckv_c, krope_c, state_c = ctx
        cache, past = (ckv_c, jnp.pad(krope_c, ((0, 0), (0, 0), ROPE_PAD))), ckv_c.shape[1]
        s0 = jnp.stack([_state_to_pairs(state_c[:, 0]), _state_to_pairs(state_c[:, 1])], axis=1)
    outs = _inproj(x, mods, mod_row, inproj_w, rope_tabs, ctx is None, min(n, TOKEN_TILE), cache, ctx is None)
    q, k, v, qh, kf, lff, kb, lfb, vh, sg = outs[:N_INPROJ_OUT]
    if ctx is None:
        ckv, krope = outs[N_INPROJ_OUT:]
    bk = next(b for b in ATTN_KV_TILES if (n + past) % b == 0)
    o_mla = _attention(q, k, v, min(n, ATTN_Q_TILE), bk)
    o_f, o_b, *state = _hgrn(qh, kf, lff, kb, lfb, vh, s0, ctx is None)
    per_token = (x, o_mla, o_f, o_b, sg)
    if ctx is None:
        per_token = tuple(a.reshape(1, bsz * n, a.shape[-1]) for a in per_token)
    tm = min(per_token[0].shape[1], TOKEN_TILE)
    x1 = _merge(*per_token[:1], mods, mod_row, *per_token[1:], merge_w, tm)
    y = _ffn(x1, mods, mod_row, ffn_w, tm).reshape(bsz, n, D_MODEL)
    if ctx is None:
        return y, (ckv, krope, state[0])
    return y, None


def kernel(x_prompt, x_sample, cache_ckv, cache_krope, state_hgrn, c, c_ctx, w_ada, b_ada, g_norm_mix, g_norm_ff, w_in, g_q_norm, w_uq, g_kv_norm, w_ukv, g_hg_norm, hg_lb_logits, w_br_mla, w_br_hg, w_out, w_ff1, w_ff2, g_final):
    assert w_in.shape[0] == 1, "single-layer trunk"
    dec_b = c.shape[0]
    assert dec_b + 1 <= SUBLANES
    cc = jnp.concatenate([c, c_ctx[None, :], jnp.zeros((SUBLANES - dec_b - 1, D_MODEL), F32)], axis=0)
    mods = _ada(cc, w_ada[0], b_ada[0][None, :]).reshape(SUBLANES, 1, 6 * D_MODEL)

    (w_lat, w_mix, w_gate), w_uq_p, w_ukv_p = _prep_weights(w_in[0], w_uq[0], w_ukv[0])
    g_mix = g_norm_mix[0][None]
    inproj_w = ((w_lat, w_mix), g_mix, g_q_norm[0][None], w_uq_p, g_kv_norm[0][None], w_ukv_p, hg_lb_logits)
    head_id = np.arange(HG_W) // HG_DV
    hmean = jnp.asarray((head_id[:, None] == head_id[None, :]).astype(np.float32) / HG_DV, BF16)
    merge_w = (g_mix, w_gate, g_hg_norm[0][None], hmean, w_br_mla[0].astype(BF16), w_br_hg[0].astype(BF16),
               w_out[0].astype(BF16))
    ffn_w = (g_norm_ff[0][None], w_ff1[0].astype(BF16), w_ff2[0].astype(BF16), g_final[None])
    wts = (inproj_w, merge_w, ffn_w)

    y_prompt, (ckv, krope, state) = _trunk(x_prompt, mods, lambda b: dec_b, wts, None, None)
    rope_tabs = _rope_tables(x_sample.shape[1])
    y_sample, _ = _trunk(x_sample, mods, lambda b: b, wts, rope_tabs,
                         (cache_ckv[:, 0], cache_krope[:, 0], state_hgrn[:, 0]))
    return (y_prompt, y_sample, ckv[:, None], krope[:, None], state[:, None])
```

```python
import functools

import numpy as np
import jax
import jax.numpy as jnp
from jax import lax
from jax.experimental import pallas as pl
from jax.experimental.pallas import tpu as pltpu

D_MODEL = 1024
GRID_W = 64
MLA_HEADS = 8
Q_LORA = 384
KV_LORA = 256
QK_NOPE = 64
QK_ROPE = 32
V_HEAD = 64
MLA_W = MLA_HEADS * V_HEAD
MLA_SCALE = (QK_NOPE + QK_ROPE) ** -0.5
LOG2_E = 1.4426950408889634
HG_HEADS = 8
HG_DK = 64
HG_DV = 64
HG_KW = HG_HEADS * HG_DK
HG_W = HG_HEADS * HG_DV
D_FF = 4 * D_MODEL
ROPE_BASE = 10000.0
EPS = 1e-6

LANES = 128
SUBLANES = 8
HEAD_PAD = LANES
ROPE_PAD = (QK_NOPE, HEAD_PAD - QK_NOPE - QK_ROPE)
QK_PAD_W = MLA_HEADS * HEAD_PAD
N_PAIRS = HG_HEADS // 2
VMEM_LIMIT = 56 * 1024 * 1024

_LAT_W = Q_LORA + KV_LORA
_MIX_SIZES = (HG_KW, HG_KW, HG_KW, HG_W, HG_W)
_MIX_OFFS = tuple(int(o) for o in np.cumsum((0,) + _MIX_SIZES))

HG_CHUNK = 128
HG_STEP_CHUNKS = 4
HG_LEVELS = 7

TOKEN_TILE = 512
ATTN_Q_TILE = 1024
ATTN_Q_SPLIT = 2
ATTN_KV_TILES = (768, 512, 256, 128)
ATTN_KV_SUB = 3
ADA_COL_TILE = 1536

F32 = jnp.float32
BF16 = jnp.bfloat16


def _cparams(sem):
    return pltpu.CompilerParams(dimension_semantics=sem, vmem_limit_bytes=VMEM_LIMIT)


def _rms(x, g):
    return x * lax.rsqrt(jnp.mean(x * x, axis=-1, keepdims=True) + EPS) * g


def _dot(a, b):
    return jnp.dot(a, b, preferred_element_type=F32)


def _dot_nt(a, b):
    return lax.dot_general(a, b, (((1,), (1,)), ((), ())), preferred_element_type=F32)


def _dot_tn(a, b):
    return lax.dot_general(a, b, (((0,), (0,)), ((), ())), preferred_element_type=F32)


def _split_hi_lo(x):
    hi = x.astype(BF16)
    lo = (x - hi.astype(F32)).astype(BF16)
    return hi, lo


def _ada_kernel(c_ref, w_ref, b_ref, o_ref):
    c = c_ref[...]
    a = c * jax.nn.sigmoid(c)
    a_hi, a_lo = _split_hi_lo(a)
    w_hi, w_lo = _split_hi_lo(w_ref[...])
    o_ref[...] = _dot(a_hi, w_hi) + _dot(a_hi, w_lo) + _dot(a_lo, w_hi) + b_ref[...]


def _ada(cc, w_ada, b_ada):
    rows, tn = cc.shape[0], ADA_COL_TILE
    n = w_ada.shape[1]
    return pl.pallas_call(
        _ada_kernel,
        grid=(n // tn,),
        in_specs=[pl.BlockSpec((rows, D_MODEL), lambda j: (0, 0)),
                  pl.BlockSpec((D_MODEL, tn), lambda j: (0, j)),
                  pl.BlockSpec((1, tn), lambda j: (0, j))],
        out_specs=pl.BlockSpec((rows, tn), lambda j: (0, j)),
        out_shape=jax.ShapeDtypeStruct((rows, n), F32),
        compiler_params=_cparams(("arbitrary",)),
        name="ada",
    )(cc, w_ada, b_ada)


def _rope_tile(blk, cos, sin, even):
    rot = jnp.where(even, -pltpu.roll(blk, LANES - 1, 1), pltpu.roll(blk, 1, 1))
    return blk * cos + rot * sin


def _run_staggered(stages):
    pending = stages[0][0]()
    for i, (_, consume) in enumerate(stages):
        current = pending
        if i + 1 < len(stages):
            pending = stages[i + 1][0]()
        consume(current)


def _store_kv(kv, kr, k_ref, v_ref, bi=0, rs=slice(None)):
    rows = kv.shape[0]
    nope_lane = lax.broadcasted_iota(jnp.int32, (rows, HEAD_PAD), 1) < QK_NOPE
    ones_row = jnp.where(lax.broadcasted_iota(jnp.int32, (SUBLANES, rows), 0) == 0, 1.0, 0.0)
    tail = jnp.concatenate([ones_row, jnp.zeros((HEAD_PAD - V_HEAD - SUBLANES, rows), F32)], axis=0)
    for hh in range(MLA_HEADS):
        sl = slice(hh * HEAD_PAD, (hh + 1) * HEAD_PAD)
        blk = kv[:, sl]
        k_ref[bi, rs, sl] = jnp.where(nope_lane, blk, kr).astype(BF16)
        v_t = blk.T[QK_NOPE:]
        v_ref[bi, sl, rs] = jnp.concatenate([v_t, tail], axis=0).astype(BF16)


N_INPROJ_IN = 11
N_INPROJ_OUT = 10


def _inproj_kernel(*refs, rope, cache_out, n_blocks, with_cache):
    if with_cache:
        step = pl.program_id(1)
        n_in = N_INPROJ_IN + 2 * rope
        ckvc_ref, krc_ref = refs[n_in:n_in + 2]
        refs = refs[:n_in] + refs[n_in + 2:]
        k_ref, v_ref, wukv_ref = refs[n_in + 1], refs[n_in + 2], refs[9]

        @pl.when(step >= n_blocks)
        def _():
            _store_kv(_dot(ckvc_ref[0].astype(BF16), wukv_ref[...]), krc_ref[0], k_ref, v_ref)

        pl.when(step < n_blocks)(functools.partial(_inproj_tile, refs, rope, cache_out))
    else:
        _inproj_tile(refs, rope, cache_out)


def _inproj_tile(refs, rope, cache_out):
    (x_ref, sh_ref, sc_ref, gmix_ref, wlat_ref, wmix_ref, gq_ref, wuq_ref, gkv_ref, wukv_ref, lbl_ref) = refs[:N_INPROJ_IN]
    refs = refs[N_INPROJ_IN:]
    if rope:
        cos_ref, sin_ref = refs[:2]
        refs = refs[2:]
    (q_ref, k_ref, v_ref, qh_ref, kf_ref, lff_ref, kb_ref, lfb_ref, vh_ref, sg_ref) = refs[:N_INPROJ_OUT]
    refs = refs[N_INPROJ_OUT:]
    tb, tm = x_ref.shape[:2]
    if tb > 1:
        parts = [(bi, slice(None)) for bi in range(tb)]
    elif tm == TOKEN_TILE:
        parts = [(0, slice(0, tm // 2)), (0, slice(tm // 2, tm))]
    else:
        parts = [(0, slice(None))]

    def stages_for(bi, rs):
        hb = (_rms(x_ref[bi, rs], gmix_ref[...]) * (1.0 + sc_ref[0]) + sh_ref[0]).astype(BF16)
        if rope:
            cos, sin = cos_ref[rs], sin_ref[rs]
            even = (lax.broadcasted_iota(jnp.int32, cos.shape, 1) & 1) == 0
        kept = {}

        def use_latents(y):
            kept["qn"] = _rms(y[:, :Q_LORA], gq_ref[...]).astype(BF16)
            kept["ckv"] = _rms(y[:, Q_LORA:_LAT_W], gkv_ref[...])
            kr = y[:, _LAT_W:]
            if cache_out:
                refs[0][bi, rs] = kept["ckv"]
                refs[1][bi, rs] = kr[:, ROPE_PAD[0]:ROPE_PAD[0] + QK_ROPE]
            kept["kr"] = _rope_tile(kr, cos, sin, even) if rope else kr

        def use_q(q):
            q = q * (MLA_SCALE * LOG2_E)
            for hh in range(MLA_HEADS):
                sl = slice(hh * HEAD_PAD, (hh + 1) * HEAD_PAD)
                blk = q[:, sl]
                if rope:
                    blk = _rope_tile(blk, cos, sin, even)
                q_ref[bi, sl, rs] = blk.T.astype(BF16)

        def use_hq(hq):
            qh_ref[bi, rs] = hq * jax.nn.sigmoid(hq) * (HG_DK ** -0.5)

        def use_forget(d, k_out, lf_out):
            def use(z):
                l0, l1 = lbl_ref[0, d:d + 1], lbl_ref[1, d:d + 1]
                lmax = jnp.maximum(l0, l1)
                e0, e1 = jnp.exp(l0 - lmax), jnp.exp(l1 - lmax)
                lbd = e0 / (e0 + e1)
                f = lbd + (1.0 - lbd) * jax.nn.sigmoid(z)
                k_out[bi, rs] = 1.0 - f
                lf_out[bi, rs] = jnp.log2(f)
            return use

        def use_hi(y):
            vh_ref[bi, rs] = y.astype(vh_ref.dtype)

        def use_hg(hg):
            sg_ref[bi, rs] = (hg * jax.nn.sigmoid(hg)).astype(sg_ref.dtype)

        def mix(i):
            return lambda: _dot(hb, wmix_ref[:, _MIX_OFFS[i]:_MIX_OFFS[i + 1]])

        return [
            (lambda: _dot(hb, wlat_ref[...]), use_latents),
            (mix(0), use_hq),
            (lambda: _dot(kept["qn"], wuq_ref[...]), use_q),
            (lambda: _dot(kept["ckv"].astype(BF16), wukv_ref[...]),
             lambda kv: _store_kv(kv, kept["kr"], k_ref, v_ref, bi, rs)),
            (mix(1), use_forget(0, kf_ref, lff_ref)),
            (mix(2), use_forget(1, kb_ref, lfb_ref)),
            (mix(3), use_hi),
            (mix(4), use_hg),
        ]

    per_part = [stages_for(bi, rs) for bi, rs in parts]
    _run_staggered([stage for group in zip(*per_part) for stage in group])


def _const_spec(shape):
    return pl.BlockSpec(shape, lambda *_: (0,) * len(shape), pipeline_mode=pl.Buffered(1))


def _inproj(x, mods, mod_row, wts, rope_tabs, cache_out, tm, cache, shared_mods):
    bsz, n, _ = x.shape
    rope = rope_tabs is not None
    (w_lat, w_mix), g_mix, g_q, w_uq_p, g_kv, w_ukv_p, lb_logits = wts
    n_blocks = n // tm
    past = 0 if cache is None else cache[0].shape[1]
    assert n_blocks * tm == n and past % tm == 0
    kv_rows = n + past
    tb = 2 if (shared_mods and cache is None and tm < TOKEN_TILE and bsz % 2 == 0) else 1

    def own(i):
        return i if cache is None else jnp.minimum(i, n_blocks - 1)

    def tok(width):
        return pl.BlockSpec((tb, tm, width), lambda b, i: (b, own(i), 0))

    def mod(col):
        return pl.BlockSpec((1, 1, D_MODEL), lambda b, i: (mod_row(b * tb), 0, col))

    in_specs = [tok(D_MODEL), mod(0), mod(1), _const_spec((1, D_MODEL)), _const_spec(w_lat.shape),
                _const_spec(w_mix.shape),
                _const_spec((1, Q_LORA)), _const_spec(w_uq_p.shape), _const_spec((1, KV_LORA)),
                _const_spec(w_ukv_p.shape), _const_spec(lb_logits.shape)]
    args = [x, mods, mods, g_mix, w_lat, w_mix, g_q, w_uq_p, g_kv, w_ukv_p, lb_logits]
    assert len(args) == N_INPROJ_IN
    if rope:
        in_specs += [pl.BlockSpec((tm, LANES), lambda b, i: (own(i), 0))] * 2
        args += list(rope_tabs)
    if cache is not None:
        in_specs += [pl.BlockSpec((1, tm, w), lambda b, i: (b, jnp.maximum(i - n_blocks, 0), 0))
                     for w in (KV_LORA, LANES)]
        args += list(cache)

    widths = [(QK_PAD_W, BF16)] * 3 + [(HG_KW, F32)] * 5 + [(HG_W, BF16)] * 2
    assert len(widths) == N_INPROJ_OUT
    if cache_out:
        widths += [(KV_LORA, F32), (QK_ROPE, F32)]
    out_specs = [tok(w) for w, _ in widths]
    out_shape = [jax.ShapeDtypeStruct((bsz, n, w), dt) for w, dt in widths]
    out_specs[0] = pl.BlockSpec((tb, QK_PAD_W, tm), lambda b, i: (b, 0, own(i)))
    out_shape[0] = jax.ShapeDtypeStruct((bsz, QK_PAD_W, n), BF16)
    out_specs[1] = pl.BlockSpec((tb, tm, QK_PAD_W), lambda b, i: (b, i, 0))
    out_shape[1] = jax.ShapeDtypeStruct((bsz, kv_rows, QK_PAD_W), BF16)
    out_specs[2] = pl.BlockSpec((tb, QK_PAD_W, tm), lambda b, i: (b, 0, i))
    out_shape[2] = jax.ShapeDtypeStruct((bsz, QK_PAD_W, kv_rows), BF16)
    return pl.pallas_call(
        functools.partial(_inproj_kernel, rope=rope, cache_out=cache_out, n_blocks=n_blocks,
                          with_cache=cache is not None),
        grid=(bsz // tb, kv_rows // tm),
        in_specs=in_specs, out_specs=out_specs, out_shape=out_shape,
        compiler_params=_cparams(("parallel", "arbitrary")),
        name="inproj_rope" if rope else "inproj",
    )(*args)


def _attn_kernel(q_ref, k_ref, v_ref, o_ref, m_scr, acc_scr, *, bk):
    j = pl.program_id(2)

    @pl.when(j == 0)
    def _():
        m_scr[...] = jnp.full(m_scr.shape, -jnp.inf, F32)
        acc_scr[...] = jnp.zeros(acc_scr.shape, F32)

    units = [(slice(kb * bk, (kb + 1) * bk), hh) for kb in range(k_ref.shape[1] // bk) for hh in range(MLA_HEADS)]

    def scores(keys, hh):
        sl = slice(hh * HEAD_PAD, (hh + 1) * HEAD_PAD)
        return _dot(k_ref[0, keys, sl], q_ref[0, sl, :])

    ahead = 2
    pending = [scores(*u) for u in units[:ahead]]
    for i, (keys, hh) in enumerate(units):
        sl = slice(hh * HEAD_PAD, (hh + 1) * HEAD_PAD)
        s = pending.pop(0)
        if i + ahead < len(units):
            pending.append(scores(*units[i + ahead]))
        bq = s.shape[1]
        n_split = min(ATTN_Q_SPLIT, bq // LANES)
        for cs in (slice(c * bq // n_split, (c + 1) * bq // n_split) for c in range(n_split)):
            s_c = s[:, cs]
            m_prev = m_scr[hh, :, cs]
            m_cur = jnp.maximum(m_prev, jnp.max(s_c, axis=0, keepdims=True))
            alpha = jnp.exp2(m_prev - m_cur)
            p = jnp.exp2(s_c - m_cur[:1])
            m_scr[hh, :, cs] = m_cur
            acc_scr[hh, :, cs] = alpha[:1] * acc_scr[hh, :, cs] + _dot(v_ref[0, sl, keys], p.astype(BF16))

    @pl.when(j == pl.num_programs(2) - 1)
    def _():
        def head_out(hh):
            acc = acc_scr[hh]
            return acc[:V_HEAD] / acc[V_HEAD:V_HEAD + 1]

        for pair in range(MLA_HEADS // 2):
            both = jnp.concatenate([head_out(2 * pair), head_out(2 * pair + 1)], axis=0)
            o_ref[0, :, pair * LANES:(pair + 1) * LANES] = both.T.astype(o_ref.dtype)


def _attention(q, k, v, bq, bk):
    bsz, _, nq = q.shape
    nk = k.shape[1]
    step_keys = bk * ATTN_KV_SUB if nk % (bk * ATTN_KV_SUB) == 0 else bk
    return pl.pallas_call(
        functools.partial(_attn_kernel, bk=bk),
        grid=(bsz, nq // bq, nk // step_keys),
        in_specs=[pl.BlockSpec((1, QK_PAD_W, bq), lambda b, i, j: (b, 0, i)),
                  pl.BlockSpec((1, step_keys, QK_PAD_W), lambda b, i, j: (b, j, 0)),
                  pl.BlockSpec((1, QK_PAD_W, step_keys), lambda b, i, j: (b, 0, j))],
        out_specs=pl.BlockSpec((1, bq, MLA_W), lambda b, i, j: (b, i, 0)),
        out_shape=jax.ShapeDtypeStruct((bsz, nq, MLA_W), BF16),
        scratch_shapes=[pltpu.VMEM((MLA_HEADS, SUBLANES, bq), F32),
                        pltpu.VMEM((MLA_HEADS, HEAD_PAD, bq), F32)],
        compiler_params=_cparams(("parallel", "parallel", "arbitrary")),
        name="attn",
    )(q, k, v)


HG_TILE_LEVELS = 3


def _hgrn_constants(reverse):
    c = HG_CHUNK
    t = np.arange(c)
    mats = [(t[None, :] <= t[:, None])]
    for lvl in range(1, HG_TILE_LEVELS + 1):
        g, half = 1 << lvl, 1 << (lvl - 1)
        p = t % g
        mid = t - p + half - 1
        isq = p >= half
        u = t[None, :]
        mats.append(np.where(isq[:, None], (u > mid[:, None]) & (u <= t[:, None]),
                             (u > t[:, None]) & (u <= mid[:, None])))
    mstack = np.stack(mats).astype(np.float32)
    if reverse:
        mstack = mstack[:, ::-1, ::-1]
    mstack = mstack.reshape(-1, c)
    return jnp.asarray(np.concatenate([mstack, mstack], axis=1), BF16)


def _hgrn_kernel(qf_ref, qb_ref, kf_ref, lff_ref, kb_ref, lfb_ref, vf_ref, vb_ref, mstack_ref, *rest,
                 has_s0, emit_state):
    if has_s0:
        s0_ref, rest = rest[0], rest[1:]
    of_ref, ob_ref = rest[:2]
    rest = rest[2:]
    st_scr = rest[-1]
    c = HG_CHUNK
    n_sub = qf_ref.shape[1] // c
    step = pl.program_id(1)

    @pl.when(step == 0)
    def _():
        st_scr[...] = s0_ref[0] if has_s0 else jnp.zeros(st_scr.shape, F32)

    dirs = ((qf_ref, kf_ref, lff_ref, vf_ref, of_ref, False), (qb_ref, kb_ref, lfb_ref, vb_ref, ob_ref, True))
    exps = {}

    def exponents(d, lf_ref, rows):
        if (d, rows.start) not in exps:
            lf_hi, lf_lo = _split_hi_lo(lf_ref[0, rows])
            exps[d, rows.start] = _dot(mstack_ref[d], jnp.concatenate([lf_hi, lf_lo], axis=0))
        return exps[d, rows.start]

    units = []
    for j in range(n_sub):
        for pair in range(N_PAIRS):
            for d, (q_ref, k_ref, lf_ref, v_ref, o_ref, reverse) in enumerate(dirs):
                jj = n_sub - 1 - j if reverse else j
                rows = slice(jj * c, (jj + 1) * c)
                units.append(_hgrn_pair(q_ref, k_ref, v_ref, o_ref, rows,
                                        functools.partial(exponents, d, lf_ref, rows), st_scr.at[d], pair, reverse))
    ahead = 3
    for unit in units[:ahead]:
        next(unit)
    for i, unit in enumerate(units):
        if i + ahead < len(units):
            next(units[i + ahead])
        next(unit, None)

    if emit_state:
        sfin_ref = rest[0]

        @pl.when(step == pl.num_programs(1) - 1)
        def _():
            for d in range(2):
                for pair in range(N_PAIRS):
                    s_pair = st_scr[d, pair].T
                    sfin_ref[0, d, 2 * pair] = s_pair[:HG_DK, :HG_DV]
                    sfin_ref[0, d, 2 * pair + 1] = pltpu.roll(s_pair, HG_DV, 1)[HG_DK:, :HG_DV]


def _hgrn_pair(q_ref, k_ref, v_ref, o_ref, tok_rows, get_exponents, st_scr, pair, reverse):
    c = HG_CHUNK
    e = get_exponents()
    last = 0 if reverse else c - 1
    row = lax.broadcasted_iota(jnp.int32, (c, LANES), 0)
    lane_low = lax.broadcasted_iota(jnp.int32, (c, LANES), 1) < HG_DK
    low_b = jnp.where(lane_low, 1.0, 0.0).astype(BF16)
    high_b = jnp.where(lane_low, 0.0, 1.0).astype(BF16)
    blank_t = jnp.zeros((HG_DK, c), BF16)
    xor = lax.broadcasted_iota(jnp.int32, (c, c), 0) ^ lax.broadcasted_iota(jnp.int32, (c, c), 1)
    diag_block = (lax.broadcasted_iota(jnp.int32, (LANES, LANES), 0) < HG_DV) == (
        lax.broadcasted_iota(jnp.int32, (LANES, LANES), 1) < HG_DK)
    zeros8 = jnp.zeros((SUBLANES, LANES), F32)

    def q_side_block(b):
        return (b % 2 == 1) != reverse

    sl = slice(pair * LANES, (pair + 1) * LANES)
    q = q_ref[0, tok_rows, sl]
    k = k_ref[0, tok_rows, sl]
    v = v_ref[0, tok_rows, sl]
    cum = e[0:c, sl]
    cum_last = cum[last:last + 1]

    def blank(rows):
        return jnp.concatenate([zeros8] * (rows // SUBLANES), axis=0)

    products = []
    for lvl in range(HG_LEVELS, -1, -1):
        half = (1 << lvl) // 2
        if lvl == 0:
            lhs, zk = q.astype(BF16), k
        elif lvl > HG_TILE_LEVELS:
            ys, zs = [], []
            for b in range(0, c // half, 2):
                ref_row = (b + 1) * half if reverse else (b + 1) * half - 1
                r = cum[ref_row:ref_row + 1]
                for bb in (b, b + 1):
                    rs = slice(bb * half, (bb + 1) * half)
                    if q_side_block(bb):
                        ys.append(q[rs] * jnp.exp2(cum[rs] - r))
                        zs.append(blank(half))
                    else:
                        zs.append(k[rs] * jnp.exp2(r - cum[rs]))
            lhs = jnp.concatenate(ys, axis=0).astype(BF16)
            zk = jnp.concatenate(zs, axis=0)
        else:
            z = jnp.exp2(e[lvl * c:(lvl + 1) * c, sl])
            q_row = ((row & half) == 0) if reverse else ((row & half) != 0)
            y = jnp.where(q_row, q, k) * z
            lhs = y.astype(BF16)
            zk = jnp.where(q_row, 0.0, y)
        zt = zk.T.astype(BF16)
        rhs_t = jnp.concatenate([jnp.concatenate([zt[:HG_DK], blank_t], axis=0),
                                 jnp.concatenate([blank_t, zt[HG_DK:]], axis=0)], axis=1)
        products.append(_dot(lhs, rhs_t))
    yield
    a_cat = None
    for lvl, p in zip(range(HG_LEVELS, -1, -1), products):
        half = (1 << lvl) // 2
        if lvl > HG_TILE_LEVELS:
            q_blocks = [bb for bb in range(c // half) if q_side_block(bb)]
            new = []
            for h_idx in range(2):
                p_h = p[:, h_idx * c:(h_idx + 1) * c]
                rows = []
                for bb in range(c // half):
                    rs = slice(bb * half, (bb + 1) * half)
                    if not q_side_block(bb):
                        rows.append(blank(half) if a_cat is None else a_cat[h_idx][rs])
                        continue
                    i = q_blocks.index(bb)
                    p_blk = p_h[i * half:(i + 1) * half]
                    if a_cat is None:
                        rows.append(p_blk)
                    else:
                        rows.append(jnp.where(xor[rs] < (1 << lvl), p_blk, a_cat[h_idx][rs]))
                new.append(jnp.concatenate(rows, axis=0))
            a_cat = tuple(new)
        else:
            same = xor < (1 << lvl)
            a_cat = tuple(jnp.where(same, p[:, h_idx * c:(h_idx + 1) * c], a_cat[h_idx]) for h_idx in range(2))
    a_cat = jnp.concatenate(a_cat, axis=1)
    v_cat = jnp.concatenate([v * low_b, v * high_b], axis=0)
    st = st_scr[pair]
    o = _dot(a_cat.astype(BF16), v_cat) + _dot_nt((q * jnp.exp2(cum)).astype(BF16), st.astype(BF16))
    o_ref[0, tok_rows, sl] = o
    kd = (k * jnp.exp2(cum_last - cum)).astype(BF16)
    upd = _dot_tn(v, kd)
    st_scr[pair] = jnp.exp2(cum_last) * st + jnp.where(diag_block, upd, 0.0)


def _hgrn(qh, kf, lff, kb, lfb, vh, s0t, emit_state):
    bsz, n, _ = qh.shape
    rows = HG_CHUNK * min(HG_STEP_CHUNKS, n // HG_CHUNK)
    nc = n // rows
    assert nc * rows == n
    mstack = jnp.stack([_hgrn_constants(False), _hgrn_constants(True)])
    fwd = pl.BlockSpec((1, rows, HG_KW), lambda b, i: (b, i, 0))
    bwd = pl.BlockSpec((1, rows, HG_KW), lambda b, i: (b, nc - 1 - i, 0))
    in_specs = [fwd, bwd, fwd, fwd, bwd, bwd, fwd, bwd, _const_spec(mstack.shape)]
    args = [qh, qh, kf, lff, kb, lfb, vh, vh, mstack]
    if s0t is not None:
        in_specs.append(pl.BlockSpec((1, 2, N_PAIRS, LANES, LANES), lambda b, i: (b, 0, 0, 0, 0)))
        args.append(s0t)
    out_specs = [fwd, bwd]
    out_shape = [jax.ShapeDtypeStruct((bsz, n, HG_W), F32)] * 2
    if emit_state:
        out_specs.append(pl.BlockSpec((1, 2, HG_HEADS, HG_DK, HG_DV), lambda b, i: (b, 0, 0, 0, 0)))
        out_shape.append(jax.ShapeDtypeStruct((bsz, 2, HG_HEADS, HG_DK, HG_DV), F32))
    return pl.pallas_call(
        functools.partial(_hgrn_kernel, has_s0=s0t is not None, emit_state=emit_state),
        grid=(bsz, nc),
        in_specs=in_specs, out_specs=out_specs, out_shape=out_shape,
        scratch_shapes=[pltpu.VMEM((2, N_PAIRS, LANES, LANES), F32)],
        compiler_params=_cparams(("parallel", "arbitrary")),
        name="hgrn",
    )(*args)


def _state_to_pairs(s):
    b = s.shape[0]
    st = jnp.swapaxes(s, -1, -2).reshape(b, N_PAIRS, 2, HG_DV, HG_DK)
    eye = jnp.eye(2, dtype=s.dtype)
    out = jnp.einsum('bpavk,ac->bpavck', st, eye)
    return out.reshape(b, N_PAIRS, 2 * HG_DV, 2 * HG_DK)


def _merge_kernel(x_ref, sh_ref, sc_ref, ga_ref, om_ref, of_ref, ob_ref, sg_ref, gmix_ref, wgate_ref, ghg_ref,
                  hmean_ref, wbm_ref, wbh_ref, wout_ref, x1_ref):
    tm = x_ref.shape[1]
    halves = [slice(i * tm // 2, (i + 1) * tm // 2) for i in range(2)]
    o = [of_ref[0, rs] + ob_ref[0, rs] for rs in halves]

    def head_mean_sq(i):
        return _dot((o[i] * o[i]).astype(BF16), hmean_ref[...])

    ms = [head_mean_sq(i) for i in range(2)]
    mla = [_dot(om_ref[0, rs], wbm_ref[...]) for rs in halves]
    hb = [(_rms(x_ref[0, rs], gmix_ref[...]) * (1.0 + sc_ref[0]) + sh_ref[0]).astype(BF16) for rs in halves]
    gates = [_dot(hb[i], wgate_ref[...]) for i in range(2)]
    hg = []
    for i, rs in enumerate(halves):
        o_hg = (o[i] * lax.rsqrt(ms[i] + EPS) * ghg_ref[...]) * sg_ref[0, rs]
        hg.append(_dot(o_hg.astype(BF16), wbh_ref[...]))
    out = []
    for i, rs in enumerate(halves):
        merged = (jax.nn.sigmoid(gates[i][:, :D_MODEL]) * mla[i]
                  + jax.nn.sigmoid(gates[i][:, D_MODEL:]) * hg[i])
        out.append(_dot(merged.astype(BF16), wout_ref[...]))
    for i, rs in enumerate(halves):
        x1_ref[0, rs] = x_ref[0, rs] + ga_ref[0] * out[i]


def _merge(x, mods, mod_row, o_mla, o_f, o_b, sg, wts, tm):
    bsz, n, _ = x.shape
    g_mix, w_gate, g_hg, hmean, w_br_mla, w_br_hg, w_out = wts

    def tok(width):
        return pl.BlockSpec((1, tm, width), lambda b, i: (b, i, 0))

    def mod(col):
        return pl.BlockSpec((1, 1, D_MODEL), lambda b, i: (mod_row(b), 0, col))

    return pl.pallas_call(
        _merge_kernel,
        grid=(bsz, n // tm),
        in_specs=[tok(D_MODEL), mod(0), mod(1), mod(2),
                  tok(MLA_W), tok(HG_W), tok(HG_W), tok(HG_W),
                  _const_spec((1, D_MODEL)), _const_spec(w_gate.shape),
                  _const_spec((1, HG_W)), _const_spec(hmean.shape), _const_spec(w_br_mla.shape),
                  _const_spec(w_br_hg.shape), _const_spec(w_out.shape)],
        out_specs=tok(D_MODEL),
        out_shape=jax.ShapeDtypeStruct((bsz, n, D_MODEL), F32),
        compiler_params=_cparams(("parallel", "parallel")),
        name="merge",
    )(x, mods, mods, mods, o_mla, o_f, o_b, sg, g_mix, w_gate, g_hg, hmean, w_br_mla, w_br_hg, w_out)


FFN_CHUNK = 1024


def _ffn_kernel(x_ref, sh_ref, sc_ref, ga_ref, gff_ref, w1_ref, w2_ref, gfin_ref, y_ref):
    tm = x_ref.shape[1]
    halves = [slice(i * tm // 2, (i + 1) * tm // 2) for i in range(2)]
    h = [(_rms(x_ref[0, rs], gff_ref[...]) * (1.0 + sc_ref[0]) + sh_ref[0]).astype(BF16) for rs in halves]

    up = [(i, c0) for c0 in range(0, D_FF, FFN_CHUNK) for i in range(2)]
    acts = [[], []]

    def use_up(i):
        def use(a):
            a = jnp.maximum(a, 0.0)
            acts[i].append((a * a).astype(BF16))
        return use

    def use_down(rs):
        def use(f):
            y_ref[0, rs] = _rms(x_ref[0, rs] + ga_ref[0] * f, gfin_ref[...])
        return use

    stages = [(functools.partial(lambda i, c0: _dot(h[i], w1_ref[:, c0:c0 + FFN_CHUNK]), i, c0), use_up(i))
              for i, c0 in up]
    stages += [(functools.partial(lambda i: _dot(jnp.concatenate(acts[i], axis=1), w2_ref[...]), i), use_down(rs))
               for i, rs in enumerate(halves)]
    _run_staggered(stages)


def _ffn(x1, mods, mod_row, wts, tm):
    bsz, n, _ = x1.shape
    g_ff, w_ff1, w_ff2, g_final = wts

    def mod(col):
        return pl.BlockSpec((1, 1, D_MODEL), lambda b, i: (mod_row(b), 0, col))

    tok = pl.BlockSpec((1, tm, D_MODEL), lambda b, i: (b, i, 0))
    return pl.pallas_call(
        _ffn_kernel,
        grid=(bsz, n // tm),
        in_specs=[tok, mod(3), mod(4), mod(5), _const_spec((1, D_MODEL)),
                  _const_spec(w_ff1.shape), _const_spec(w_ff2.shape), _const_spec((1, D_MODEL))],
        out_specs=tok,
        out_shape=jax.ShapeDtypeStruct((bsz, n, D_MODEL), F32),
        compiler_params=_cparams(("parallel", "parallel")),
        name="ffn",
    )(x1, mods, mods, mods, g_ff, w_ff1, w_ff2, g_final)


def _prep_weights(w_in, w_uq, w_ukv):
    w_lat = jnp.concatenate([w_in[:, :_LAT_W].astype(BF16),
                             jnp.pad(w_in[:, _LAT_W:_LAT_W + QK_ROPE].astype(BF16), ((0, 0), ROPE_PAD))], axis=1)
    mix0 = _LAT_W + QK_ROPE
    w_mix = w_in[:, mix0:mix0 + _MIX_OFFS[-1]].astype(BF16)
    w_gate = w_in[:, mix0 + _MIX_OFFS[-1]:].astype(BF16)
    w_in_p = (w_lat, w_mix, w_gate)
    w_uq_p = jnp.pad(w_uq.reshape(Q_LORA, MLA_HEADS, QK_NOPE + QK_ROPE).astype(BF16),
                     ((0, 0), (0, 0), (0, ROPE_PAD[1]))).reshape(Q_LORA, QK_PAD_W)
    assert QK_NOPE + V_HEAD == HEAD_PAD
    return w_in_p, w_uq_p, w_ukv.astype(BF16)


def _rope_tables(n):
    rows = n // GRID_W
    row = jnp.repeat(jnp.arange(rows, dtype=F32), GRID_W)
    col = jnp.tile(jnp.arange(GRID_W, dtype=F32), rows)
    half = QK_ROPE // 2
    inv = ROPE_BASE ** (-jnp.arange(0, half, 2, dtype=F32) / half)
    ang = jnp.concatenate([row[:, None] * inv, col[:, None] * inv], axis=-1)
    cos = jnp.repeat(jnp.cos(ang), 2, axis=-1)
    sin = jnp.repeat(jnp.sin(ang), 2, axis=-1)
    cos = jnp.pad(cos, ((0, 0), ROPE_PAD), constant_values=1.0)
    sin = jnp.pad(sin, ((0, 0), ROPE_PAD))
    return cos, sin


def _trunk(x, mods, mod_row, wts, rope_tabs, ctx):
    inproj_w, merge_w, ffn_w = wts
    bsz, n, _ = x.shape
    if ctx is None:
        cache, past = None, 0
        s0 = None
    else:
        ckv_c, krope_c, state_c = ctx
        cache, past = (ckv_c, jnp.pad(krope_c, ((0, 0), (0, 0), ROPE_PAD))), ckv_c.shape[1]
        s0 = jnp.stack([_state_to_pairs(state_c[:, 0]), _state_to_pairs(state_c[:, 1])], axis=1)
    outs = _inproj(x, mods, mod_row, inproj_w, rope_tabs, ctx is None, min(n, TOKEN_TILE), cache, ctx is None)
    q, k, v, qh, kf, lff, kb, lfb, vh, sg = outs[:N_INPROJ_OUT]
    if ctx is None:
        ckv, krope = outs[N_INPROJ_OUT:]
    bk = next(b for b in ATTN_KV_TILES if (n + past) % b == 0)
    o_mla = _attention(q, k, v, min(n, ATTN_Q_TILE), bk)
    o_f, o_b, *state = _hgrn(qh, kf, lff, kb, lfb, vh, s0, ctx is None)
    per_token = (x, o_mla, o_f, o_b, sg)
    if ctx is None:
        per_token = tuple(a.reshape(1, bsz * n, a.shape[-1]) for a in per_token)
    tm = min(per_token[0].shape[1], TOKEN_TILE)
    x1 = _merge(*per_token[:1], mods, mod_row, *per_token[1:], merge_w, tm)
    y = _ffn(x1, mods, mod_row, ffn_w, tm).reshape(bsz, n, D_MODEL)
    if ctx is None:
        return y, (ckv, krope, state[0])
    return y, None


def kernel(x_prompt, x_sample, cache_ckv, cache_krope, state_hgrn, c, c_ctx, w_ada, b_ada, g_norm_mix, g_norm_ff, w_in, g_q_norm, w_uq, g_kv_norm, w_ukv, g_hg_norm, hg_lb_logits, w_br_mla, w_br_hg, w_out, w_ff1, w_ff2, g_final):
    assert w_in.shape[0] == 1, "single-layer trunk"
    dec_b = c.shape[0]
    assert dec_b + 1 <= SUBLANES
    cc = jnp.concatenate([c, c_ctx[None, :], jnp.zeros((SUBLANES - dec_b - 1, D_MODEL), F32)], axis=0)
    mods = _ada(cc, w_ada[0], b_ada[0][None, :]).reshape(SUBLANES, 1, 6 * D_MODEL)

    (w_lat, w_mix, w_gate), w_uq_p, w_ukv_p = _prep_weights(w_in[0], w_uq[0], w_ukv[0])
    g_mix = g_norm_mix[0][None]
    inproj_w = ((w_lat, w_mix), g_mix, g_q_norm[0][None], w_uq_p, g_kv_norm[0][None], w_ukv_p, hg_lb_logits)
    head_id = np.arange(HG_W) // HG_DV
    hmean = jnp.asarray((head_id[:, None] == head_id[None, :]).astype(np.float32) / HG_DV, BF16)
    merge_w = (g_mix, w_gate, g_hg_norm[0][None], hmean, w_br_mla[0].astype(BF16), w_br_hg[0].astype(BF16),
               w_out[0].astype(BF16))
    ffn_w = (g_norm_ff[0][None], w_ff1[0].astype(BF16), w_ff2[0].astype(BF16), g_final[None])
    wts = (inproj_w, merge_w, ffn_w)

    y_prompt, (ckv, krope, state) = _trunk(x_prompt, mods, lambda b: dec_b, wts, None, None)
    rope_tabs = _rope_tables(x_sample.shape[1])
    y_sample, _ = _trunk(x_sample, mods, lambda b: b, wts, rope_tabs,
                         (cache_ckv[:, 0], cache_krope[:, 0], state_hgrn[:, 0]))
    return (y_prompt, y_sample, ckv[:, None], krope[:, None], state[:, None])
```

```python
import functools

import numpy as np
import jax
import jax.numpy as jnp
from jax import lax
from jax.experimental import pallas as pl
from jax.experimental.pallas import tpu as pltpu

D_MODEL = 1024
GRID_W = 64
MLA_HEADS = 8
Q_LORA = 384
KV_LORA = 256
QK_NOPE = 64
QK_ROPE = 32
V_HEAD = 64
MLA_W = MLA_HEADS * V_HEAD
MLA_SCALE = (QK_NOPE + QK_ROPE) ** -0.5
LOG2_E = 1.4426950408889634
HG_HEADS = 8
HG_DK = 64
HG_DV = 64
HG_KW = HG_HEADS * HG_DK
HG_W = HG_HEADS * HG_DV
D_FF = 4 * D_MODEL
ROPE_BASE = 10000.0
EPS = 1e-6

LANES = 128
SUBLANES = 8
HEAD_PAD = LANES
ROPE_PAD = (QK_NOPE, HEAD_PAD - QK_NOPE - QK_ROPE)
QK_PAD_W = MLA_HEADS * HEAD_PAD
N_PAIRS = HG_HEADS // 2
VMEM_LIMIT = 56 * 1024 * 1024

_LAT_W = Q_LORA + KV_LORA
_MIX_SIZES = (HG_KW, HG_KW, HG_KW, HG_W, HG_W)
_MIX_OFFS = tuple(int(o) for o in np.cumsum((0,) + _MIX_SIZES))

HG_CHUNK = 128
HG_STEP_CHUNKS = 4
HG_LEVELS = 7

TOKEN_TILE = 512
FFN_TILE = 1024
ATTN_Q_TILE = 1024
ATTN_Q_SPLIT = 2
ATTN_KV_TILES = (768, 512, 256, 128)
ATTN_KV_SUB = 3
ADA_COL_TILE = 1536

F32 = jnp.float32
BF16 = jnp.bfloat16


def _cparams(sem):
    return pltpu.CompilerParams(dimension_semantics=sem, vmem_limit_bytes=VMEM_LIMIT)


def _rms(x, g):
    return x * lax.rsqrt(jnp.mean(x * x, axis=-1, keepdims=True) + EPS) * g


def _dot(a, b):
    return jnp.dot(a, b, preferred_element_type=F32)


def _dot_nt(a, b):
    return lax.dot_general(a, b, (((1,), (1,)), ((), ())), preferred_element_type=F32)


def _dot_tn(a, b):
    return lax.dot_general(a, b, (((0,), (0,)), ((), ())), preferred_element_type=F32)


def _split_hi_lo(x):
    hi = x.astype(BF16)
    lo = (x - hi.astype(F32)).astype(BF16)
    return hi, lo


def _ada_kernel(c_ref, w_ref, b_ref, o_ref):
    c = c_ref[...]
    a = c * jax.nn.sigmoid(c)
    a_hi, a_lo = _split_hi_lo(a)
    w_hi, w_lo = _split_hi_lo(w_ref[...])
    o_ref[...] = _dot(a_hi, w_hi) + _dot(a_hi, w_lo) + _dot(a_lo, w_hi) + b_ref[...]


def _ada(cc, w_ada, b_ada):
    rows, tn = cc.shape[0], ADA_COL_TILE
    n = w_ada.shape[1]
    return pl.pallas_call(
        _ada_kernel,
        grid=(n // tn,),
        in_specs=[pl.BlockSpec((rows, D_MODEL), lambda j: (0, 0)),
                  pl.BlockSpec((D_MODEL, tn), lambda j: (0, j)),
                  pl.BlockSpec((1, tn), lambda j: (0, j))],
        out_specs=pl.BlockSpec((rows, tn), lambda j: (0, j)),
        out_shape=jax.ShapeDtypeStruct((rows, n), F32),
        compiler_params=_cparams(("arbitrary",)),
        name="ada",
    )(cc, w_ada, b_ada)


def _rope_tile(blk, cos, sin, even):
    rot = jnp.where(even, -pltpu.roll(blk, LANES - 1, 1), pltpu.roll(blk, 1, 1))
    return blk * cos + rot * sin


def _run_staggered(stages):
    pending = stages[0][0]()
    for i, (_, consume) in enumerate(stages):
        current = pending
        if i + 1 < len(stages):
            pending = stages[i + 1][0]()
        consume(current)


def _store_kv(kv, kr, k_ref, v_ref, bi=0, rs=slice(None)):
    rows = kv.shape[0]
    nope_lane = lax.broadcasted_iota(jnp.int32, (rows, HEAD_PAD), 1) < QK_NOPE
    ones_row = jnp.where(lax.broadcasted_iota(jnp.int32, (SUBLANES, rows), 0) == 0, 1.0, 0.0)
    tail = jnp.concatenate([ones_row, jnp.zeros((HEAD_PAD - V_HEAD - SUBLANES, rows), F32)], axis=0)
    for hh in range(MLA_HEADS):
        sl = slice(hh * HEAD_PAD, (hh + 1) * HEAD_PAD)
        blk = kv[:, sl]
        k_ref[bi, rs, sl] = jnp.where(nope_lane, blk, kr).astype(BF16)
        v_t = blk.T[QK_NOPE:]
        v_ref[bi, sl, rs] = jnp.concatenate([v_t, tail], axis=0).astype(BF16)


N_INPROJ_IN = 11
N_INPROJ_OUT = 10


def _inproj_kernel(*refs, rope, cache_out, n_blocks, with_cache):
    if with_cache:
        step = pl.program_id(1)
        n_in = N_INPROJ_IN + 2 * rope
        ckvc_ref, krc_ref = refs[n_in:n_in + 2]
        refs = refs[:n_in] + refs[n_in + 2:]
        k_ref, v_ref, wukv_ref = refs[n_in + 1], refs[n_in + 2], refs[9]

        @pl.when(step >= n_blocks)
        def _():
            _store_kv(_dot(ckvc_ref[0].astype(BF16), wukv_ref[...]), krc_ref[0], k_ref, v_ref)

        pl.when(step < n_blocks)(functools.partial(_inproj_tile, refs, rope, cache_out))
    else:
        _inproj_tile(refs, rope, cache_out)


def _inproj_tile(refs, rope, cache_out):
    (x_ref, sh_ref, sc_ref, gmix_ref, wlat_ref, wmix_ref, gq_ref, wuq_ref, gkv_ref, wukv_ref, lbl_ref) = refs[:N_INPROJ_IN]
    refs = refs[N_INPROJ_IN:]
    if rope:
        cos_ref, sin_ref = refs[:2]
        refs = refs[2:]
    (q_ref, k_ref, v_ref, qh_ref, kf_ref, lff_ref, kb_ref, lfb_ref, vh_ref, sg_ref) = refs[:N_INPROJ_OUT]
    refs = refs[N_INPROJ_OUT:]
    tb, tm = x_ref.shape[:2]
    if tb > 1:
        parts = [(bi, slice(None)) for bi in range(tb)]
    elif tm == TOKEN_TILE:
        parts = [(0, slice(0, tm // 2)), (0, slice(tm // 2, tm))]
    else:
        parts = [(0, slice(None))]

    def stages_for(bi, rs):
        hb = (_rms(x_ref[bi, rs], gmix_ref[...]) * (1.0 + sc_ref[0]) + sh_ref[0]).astype(BF16)
        if rope:
            cos, sin = cos_ref[rs], sin_ref[rs]
            even = (lax.broadcasted_iota(jnp.int32, cos.shape, 1) & 1) == 0
        kept = {}

        def use_latents(y):
            kept["qn"] = _rms(y[:, :Q_LORA], gq_ref[...]).astype(BF16)
            kept["ckv"] = _rms(y[:, Q_LORA:_LAT_W], gkv_ref[...])
            kr = y[:, _LAT_W:]
            if cache_out:
                refs[0][bi, rs] = kept["ckv"]
                refs[1][bi, rs] = kr[:, ROPE_PAD[0]:ROPE_PAD[0] + QK_ROPE]
            kept["kr"] = _rope_tile(kr, cos, sin, even) if rope else kr

        def use_q(q):
            q = q * (MLA_SCALE * LOG2_E)
            for hh in range(MLA_HEADS):
                sl = slice(hh * HEAD_PAD, (hh + 1) * HEAD_PAD)
                blk = q[:, sl]
                if rope:
                    blk = _rope_tile(blk, cos, sin, even)
                q_ref[bi, sl, rs] = blk.T.astype(BF16)

        def use_hq(hq):
            qh_ref[bi, rs] = hq * jax.nn.sigmoid(hq) * (HG_DK ** -0.5)

        def use_forget(d, k_out, lf_out):
            def use(z):
                l0, l1 = lbl_ref[0, d:d + 1], lbl_ref[1, d:d + 1]
                lmax = jnp.maximum(l0, l1)
                e0, e1 = jnp.exp(l0 - lmax), jnp.exp(l1 - lmax)
                lbd = e0 / (e0 + e1)
                f = lbd + (1.0 - lbd) * jax.nn.sigmoid(z)
                k_out[bi, rs] = 1.0 - f
                lf_out[bi, rs] = jnp.log2(f)
            return use

        def use_hi(y):
            vh_ref[bi, rs] = y.astype(vh_ref.dtype)

        def use_hg(hg):
            sg_ref[bi, rs] = (hg * jax.nn.sigmoid(hg)).astype(sg_ref.dtype)

        def mix(i):
            return lambda: _dot(hb, wmix_ref[:, _MIX_OFFS[i]:_MIX_OFFS[i + 1]])

        return [
            (lambda: _dot(hb, wlat_ref[...]), use_latents),
            (mix(0), use_hq),
            (lambda: _dot(kept["qn"], wuq_ref[...]), use_q),
            (lambda: _dot(kept["ckv"].astype(BF16), wukv_ref[...]),
             lambda kv: _store_kv(kv, kept["kr"], k_ref, v_ref, bi, rs)),
            (mix(1), use_forget(0, kf_ref, lff_ref)),
            (mix(2), use_forget(1, kb_ref, lfb_ref)),
            (mix(3), use_hi),
            (mix(4), use_hg),
        ]

    per_part = [stages_for(bi, rs) for bi, rs in parts]
    _run_staggered([stage for group in zip(*per_part) for stage in group])


def _const_spec(shape):
    return pl.BlockSpec(shape, lambda *_: (0,) * len(shape), pipeline_mode=pl.Buffered(1))


def _inproj(x, mods, mod_row, wts, rope_tabs, cache_out, tm, cache, shared_mods):
    bsz, n, _ = x.shape
    rope = rope_tabs is not None
    (w_lat, w_mix), g_mix, g_q, w_uq_p, g_kv, w_ukv_p, lb_logits = wts
    n_blocks = n // tm
    past = 0 if cache is None else cache[0].shape[1]
    assert n_blocks * tm == n and past % tm == 0
    kv_rows = n + past
    tb = 2 if (shared_mods and cache is None and tm < TOKEN_TILE and bsz % 2 == 0) else 1

    def own(i):
        return i if cache is None else jnp.minimum(i, n_blocks - 1)

    def tok(width):
        return pl.BlockSpec((tb, tm, width), lambda b, i: (b, own(i), 0))

    def mod(col):
        return pl.BlockSpec((1, 1, D_MODEL), lambda b, i: (mod_row(b * tb), 0, col))

    in_specs = [tok(D_MODEL), mod(0), mod(1), _const_spec((1, D_MODEL)), _const_spec(w_lat.shape),
                _const_spec(w_mix.shape),
                _const_spec((1, Q_LORA)), _const_spec(w_uq_p.shape), _const_spec((1, KV_LORA)),
                _const_spec(w_ukv_p.shape), _const_spec(lb_logits.shape)]
    args = [x, mods, mods, g_mix, w_lat, w_mix, g_q, w_uq_p, g_kv, w_ukv_p, lb_logits]
    assert len(args) == N_INPROJ_IN
    if rope:
        in_specs += [pl.BlockSpec((tm, LANES), lambda b, i: (own(i), 0))] * 2
        args += list(rope_tabs)
    if cache is not None:
        in_specs += [pl.BlockSpec((1, tm, w), lambda b, i: (b, jnp.maximum(i - n_blocks, 0), 0))
                     for w in (KV_LORA, LANES)]
        args += list(cache)

    widths = [(QK_PAD_W, BF16)] * 3 + [(HG_KW, F32)] * 5 + [(HG_W, BF16)] * 2
    assert len(widths) == N_INPROJ_OUT
    if cache_out:
        widths += [(KV_LORA, F32), (QK_ROPE, F32)]
    out_specs = [tok(w) for w, _ in widths]
    out_shape = [jax.ShapeDtypeStruct((bsz, n, w), dt) for w, dt in widths]
    out_specs[0] = pl.BlockSpec((tb, QK_PAD_W, tm), lambda b, i: (b, 0, own(i)))
    out_shape[0] = jax.ShapeDtypeStruct((bsz, QK_PAD_W, n), BF16)
    out_specs[1] = pl.BlockSpec((tb, tm, QK_PAD_W), lambda b, i: (b, i, 0))
    out_shape[1] = jax.ShapeDtypeStruct((bsz, kv_rows, QK_PAD_W), BF16)
    out_specs[2] = pl.BlockSpec((tb, QK_PAD_W, tm), lambda b, i: (b, 0, i))
    out_shape[2] = jax.ShapeDtypeStruct((bsz, QK_PAD_W, kv_rows), BF16)
    return pl.pallas_call(
        functools.partial(_inproj_kernel, rope=rope, cache_out=cache_out, n_blocks=n_blocks,
                          with_cache=cache is not None),
        grid=(bsz // tb, kv_rows // tm),
        in_specs=in_specs, out_specs=out_specs, out_shape=out_shape,
        compiler_params=_cparams(("parallel", "arbitrary")),
        name="inproj_rope" if rope else "inproj",
    )(*args)


def _attn_kernel(q_ref, k_ref, v_ref, o_ref, m_scr, acc_scr, *, bk):
    j = pl.program_id(2)

    @pl.when(j == 0)
    def _():
        m_scr[...] = jnp.full(m_scr.shape, -jnp.inf, F32)
        acc_scr[...] = jnp.zeros(acc_scr.shape, F32)

    units = [(slice(kb * bk, (kb + 1) * bk), hh) for kb in range(k_ref.shape[1] // bk) for hh in range(MLA_HEADS)]

    def scores(keys, hh):
        sl = slice(hh * HEAD_PAD, (hh + 1) * HEAD_PAD)
        return _dot(k_ref[0, keys, sl], q_ref[0, sl, :])

    ahead = 2
    pending = [scores(*u) for u in units[:ahead]]
    for i, (keys, hh) in enumerate(units):
        sl = slice(hh * HEAD_PAD, (hh + 1) * HEAD_PAD)
        s = pending.pop(0)
        if i + ahead < len(units):
            pending.append(scores(*units[i + ahead]))
        bq = s.shape[1]
        n_split = min(ATTN_Q_SPLIT, bq // LANES)
        for cs in (slice(c * bq // n_split, (c + 1) * bq // n_split) for c in range(n_split)):
            s_c = s[:, cs]
            m_prev = m_scr[hh, :, cs]
            m_cur = jnp.maximum(m_prev, jnp.max(s_c, axis=0, keepdims=True))
            alpha = jnp.exp2(m_prev - m_cur)
            p = jnp.exp2(s_c - m_cur[:1])
            m_scr[hh, :, cs] = m_cur
            acc_scr[hh, :, cs] = alpha[:1] * acc_scr[hh, :, cs] + _dot(v_ref[0, sl, keys], p.astype(BF16))

    @pl.when(j == pl.num_programs(2) - 1)
    def _():
        def head_out(hh):
            acc = acc_scr[hh]
            return acc[:V_HEAD] / acc[V_HEAD:V_HEAD + 1]

        for pair in range(MLA_HEADS // 2):
            both = jnp.concatenate([head_out(2 * pair), head_out(2 * pair + 1)], axis=0)
            o_ref[0, :, pair * LANES:(pair + 1) * LANES] = both.T.astype(o_ref.dtype)


def _attention(q, k, v, bq, bk):
    bsz, _, nq = q.shape
    nk = k.shape[1]
    step_keys = bk * ATTN_KV_SUB if nk % (bk * ATTN_KV_SUB) == 0 else bk
    return pl.pallas_call(
        functools.partial(_attn_kernel, bk=bk),
        grid=(bsz, nq // bq, nk // step_keys),
        in_specs=[pl.BlockSpec((1, QK_PAD_W, bq), lambda b, i, j: (b, 0, i)),
                  pl.BlockSpec((1, step_keys, QK_PAD_W), lambda b, i, j: (b, j, 0)),
                  pl.BlockSpec((1, QK_PAD_W, step_keys), lambda b, i, j: (b, 0, j))],
        out_specs=pl.BlockSpec((1, bq, MLA_W), lambda b, i, j: (b, i, 0)),
        out_shape=jax.ShapeDtypeStruct((bsz, nq, MLA_W), BF16),
        scratch_shapes=[pltpu.VMEM((MLA_HEADS, SUBLANES, bq), F32),
                        pltpu.VMEM((MLA_HEADS, HEAD_PAD, bq), F32)],
        compiler_params=_cparams(("parallel", "parallel", "arbitrary")),
        name="attn",
    )(q, k, v)


HG_TILE_LEVELS = 3


def _hgrn_constants(reverse):
    c = HG_CHUNK
    t = np.arange(c)
    mats = [(t[None, :] <= t[:, None])]
    for lvl in range(1, HG_TILE_LEVELS + 1):
        g, half = 1 << lvl, 1 << (lvl - 1)
        p = t % g
        mid = t - p + half - 1
        isq = p >= half
        u = t[None, :]
        mats.append(np.where(isq[:, None], (u > mid[:, None]) & (u <= t[:, None]),
                             (u > t[:, None]) & (u <= mid[:, None])))
    mstack = np.stack(mats).astype(np.float32)
    if reverse:
        mstack = mstack[:, ::-1, ::-1]
    mstack = mstack.reshape(-1, c)
    return jnp.asarray(np.concatenate([mstack, mstack], axis=1), BF16)


def _hgrn_kernel(qf_ref, qb_ref, kf_ref, lff_ref, kb_ref, lfb_ref, vf_ref, vb_ref, mstack_ref, *rest,
                 has_s0, emit_state):
    if has_s0:
        s0_ref, rest = rest[0], rest[1:]
    of_ref, ob_ref = rest[:2]
    rest = rest[2:]
    st_scr = rest[-1]
    c = HG_CHUNK
    n_sub = qf_ref.shape[1] // c
    step = pl.program_id(1)

    @pl.when(step == 0)
    def _():
        st_scr[...] = s0_ref[0] if has_s0 else jnp.zeros(st_scr.shape, F32)

    dirs = ((qf_ref, kf_ref, lff_ref, vf_ref, of_ref, False), (qb_ref, kb_ref, lfb_ref, vb_ref, ob_ref, True))
    exps = {}

    def exponents(d, lf_ref, rows):
        if (d, rows.start) not in exps:
            lf_hi, lf_lo = _split_hi_lo(lf_ref[0, rows])
            exps[d, rows.start] = _dot(mstack_ref[d], jnp.concatenate([lf_hi, lf_lo], axis=0))
        return exps[d, rows.start]

    units = []
    for j in range(n_sub):
        for pair in range(N_PAIRS):
            for d, (q_ref, k_ref, lf_ref, v_ref, o_ref, reverse) in enumerate(dirs):
                jj = n_sub - 1 - j if reverse else j
                rows = slice(jj * c, (jj + 1) * c)
                units.append(_hgrn_pair(q_ref, k_ref, v_ref, o_ref, rows,
                                        functools.partial(exponents, d, lf_ref, rows), st_scr.at[d], pair, reverse))
    ahead = 3
    for unit in units[:ahead]:
        next(unit)
    for i, unit in enumerate(units):
        if i + ahead < len(units):
            next(units[i + ahead])
        next(unit, None)

    if emit_state:
        sfin_ref = rest[0]

        @pl.when(step == pl.num_programs(1) - 1)
        def _():
            for d in range(2):
                for pair in range(N_PAIRS):
                    s_pair = st_scr[d, pair].T
                    sfin_ref[0, d, 2 * pair] = s_pair[:HG_DK, :HG_DV]
                    sfin_ref[0, d, 2 * pair + 1] = pltpu.roll(s_pair, HG_DV, 1)[HG_DK:, :HG_DV]


def _hgrn_pair(q_ref, k_ref, v_ref, o_ref, tok_rows, get_exponents, st_scr, pair, reverse):
    c = HG_CHUNK
    e = get_exponents()
    last = 0 if reverse else c - 1
    row = lax.broadcasted_iota(jnp.int32, (c, LANES), 0)
    lane_low = lax.broadcasted_iota(jnp.int32, (c, LANES), 1) < HG_DK
    low_b = jnp.where(lane_low, 1.0, 0.0).astype(BF16)
    high_b = jnp.where(lane_low, 0.0, 1.0).astype(BF16)
    blank_t = jnp.zeros((HG_DK, c), BF16)
    xor = lax.broadcasted_iota(jnp.int32, (c, c), 0) ^ lax.broadcasted_iota(jnp.int32, (c, c), 1)
    diag_block = (lax.broadcasted_iota(jnp.int32, (LANES, LANES), 0) < HG_DV) == (
        lax.broadcasted_iota(jnp.int32, (LANES, LANES), 1) < HG_DK)
    zeros8 = jnp.zeros((SUBLANES, LANES), F32)

    def q_side_block(b):
        return (b % 2 == 1) != reverse

    sl = slice(pair * LANES, (pair + 1) * LANES)
    q = q_ref[0, tok_rows, sl]
    k = k_ref[0, tok_rows, sl]
    v = v_ref[0, tok_rows, sl]
    cum = e[0:c, sl]
    cum_last = cum[last:last + 1]

    def blank(rows):
        return jnp.concatenate([zeros8] * (rows // SUBLANES), axis=0)

    products = []
    for lvl in range(HG_LEVELS, -1, -1):
        half = (1 << lvl) // 2
        if lvl == 0:
            lhs, zk = q.astype(BF16), k
        elif lvl > HG_TILE_LEVELS:
            ys, zs = [], []
            for b in range(0, c // half, 2):
                ref_row = (b + 1) * half if reverse else (b + 1) * half - 1
                r = cum[ref_row:ref_row + 1]
                for bb in (b, b + 1):
                    rs = slice(bb * half, (bb + 1) * half)
                    if q_side_block(bb):
                        ys.append(q[rs] * jnp.exp2(cum[rs] - r))
                        zs.append(blank(half))
                    else:
                        zs.append(k[rs] * jnp.exp2(r - cum[rs]))
            lhs = jnp.concatenate(ys, axis=0).astype(BF16)
            zk = jnp.concatenate(zs, axis=0)
        else:
            z = jnp.exp2(e[lvl * c:(lvl + 1) * c, sl])
            q_row = ((row & half) == 0) if reverse else ((row & half) != 0)
            y = jnp.where(q_row, q, k) * z
            lhs = y.astype(BF16)
            zk = jnp.where(q_row, 0.0, y)
        zt = zk.T.astype(BF16)
        rhs_t = jnp.concatenate([jnp.concatenate([zt[:HG_DK], blank_t], axis=0),
                                 jnp.concatenate([blank_t, zt[HG_DK:]], axis=0)], axis=1)
        products.append(_dot(lhs, rhs_t))
    yield
    a_cat = None
    for lvl, p in zip(range(HG_LEVELS, -1, -1), products):
        half = (1 << lvl) // 2
        if lvl > HG_TILE_LEVELS:
            q_blocks = [bb for bb in range(c // half) if q_side_block(bb)]
            new = []
            for h_idx in range(2):
                p_h = p[:, h_idx * c:(h_idx + 1) * c]
                rows = []
                for bb in range(c // half):
                    rs = slice(bb * half, (bb + 1) * half)
                    if not q_side_block(bb):
                        rows.append(blank(half) if a_cat is None else a_cat[h_idx][rs])
                        continue
                    i = q_blocks.index(bb)
                    p_blk = p_h[i * half:(i + 1) * half]
                    if a_cat is None:
                        rows.append(p_blk)
                    else:
                        rows.append(jnp.where(xor[rs] < (1 << lvl), p_blk, a_cat[h_idx][rs]))
                new.append(jnp.concatenate(rows, axis=0))
            a_cat = tuple(new)
        else:
            same = xor < (1 << lvl)
            a_cat = tuple(jnp.where(same, p[:, h_idx * c:(h_idx + 1) * c], a_cat[h_idx]) for h_idx in range(2))
    a_cat = jnp.concatenate(a_cat, axis=1)
    v_cat = jnp.concatenate([v * low_b, v * high_b], axis=0)
    st = st_scr[pair]
    o = _dot(a_cat.astype(BF16), v_cat) + _dot_nt((q * jnp.exp2(cum)).astype(BF16), st.astype(BF16))
    o_ref[0, tok_rows, sl] = o
    kd = (k * jnp.exp2(cum_last - cum)).astype(BF16)
    upd = _dot_tn(v, kd)
    st_scr[pair] = jnp.exp2(cum_last) * st + jnp.where(diag_block, upd, 0.0)


def _hgrn(qh, kf, lff, kb, lfb, vh, s0t, emit_state):
    bsz, n, _ = qh.shape
    rows = HG_CHUNK * min(HG_STEP_CHUNKS, n // HG_CHUNK)
    nc = n // rows
    assert nc * rows == n
    mstack = jnp.stack([_hgrn_constants(False), _hgrn_constants(True)])
    fwd = pl.BlockSpec((1, rows, HG_KW), lambda b, i: (b, i, 0))
    bwd = pl.BlockSpec((1, rows, HG_KW), lambda b, i: (b, nc - 1 - i, 0))
    in_specs = [fwd, bwd, fwd, fwd, bwd, bwd, fwd, bwd, _const_spec(mstack.shape)]
    args = [qh, qh, kf, lff, kb, lfb, vh, vh, mstack]
    if s0t is not None:
        in_specs.append(pl.BlockSpec((1, 2, N_PAIRS, LANES, LANES), lambda b, i: (b, 0, 0, 0, 0)))
        args.append(s0t)
    out_specs = [fwd, bwd]
    out_shape = [jax.ShapeDtypeStruct((bsz, n, HG_W), F32)] * 2
    if emit_state:
        out_specs.append(pl.BlockSpec((1, 2, HG_HEADS, HG_DK, HG_DV), lambda b, i: (b, 0, 0, 0, 0)))
        out_shape.append(jax.ShapeDtypeStruct((bsz, 2, HG_HEADS, HG_DK, HG_DV), F32))
    return pl.pallas_call(
        functools.partial(_hgrn_kernel, has_s0=s0t is not None, emit_state=emit_state),
        grid=(bsz, nc),
        in_specs=in_specs, out_specs=out_specs, out_shape=out_shape,
        scratch_shapes=[pltpu.VMEM((2, N_PAIRS, LANES, LANES), F32)],
        compiler_params=_cparams(("parallel", "arbitrary")),
        name="hgrn",
    )(*args)


def _state_to_pairs(s):
    b = s.shape[0]
    st = jnp.swapaxes(s, -1, -2).reshape(b, N_PAIRS, 2, HG_DV, HG_DK)
    eye = jnp.eye(2, dtype=s.dtype)
    out = jnp.einsum('bpavk,ac->bpavck', st, eye)
    return out.reshape(b, N_PAIRS, 2 * HG_DV, 2 * HG_DK)


def _merge_kernel(x_ref, sh_ref, sc_ref, ga_ref, om_ref, of_ref, ob_ref, sg_ref, gmix_ref, wgate_ref, ghg_ref,
                  hmean_ref, wbm_ref, wbh_ref, wout_ref, x1_ref):
    tm = x_ref.shape[1]
    halves = [slice(i * tm // 2, (i + 1) * tm // 2) for i in range(2)]
    o = [of_ref[0, rs] + ob_ref[0, rs] for rs in halves]

    def head_mean_sq(i):
        return _dot((o[i] * o[i]).astype(BF16), hmean_ref[...])

    ms = [head_mean_sq(i) for i in range(2)]
    mla = [_dot(om_ref[0, rs], wbm_ref[...]) for rs in halves]
    hb = [(_rms(x_ref[0, rs], gmix_ref[...]) * (1.0 + sc_ref[0]) + sh_ref[0]).astype(BF16) for rs in halves]
    gates = [_dot(hb[i], wgate_ref[...]) for i in range(2)]
    hg = []
    for i, rs in enumerate(halves):
        o_hg = (o[i] * lax.rsqrt(ms[i] + EPS) * ghg_ref[...]) * sg_ref[0, rs]
        hg.append(_dot(o_hg.astype(BF16), wbh_ref[...]))
    out = []
    for i, rs in enumerate(halves):
        merged = (jax.nn.sigmoid(gates[i][:, :D_MODEL]) * mla[i]
                  + jax.nn.sigmoid(gates[i][:, D_MODEL:]) * hg[i])
        out.append(_dot(merged.astype(BF16), wout_ref[...]))
    for i, rs in enumerate(halves):
        x1_ref[0, rs] = x_ref[0, rs] + ga_ref[0] * out[i]


def _merge(x, mods, mod_row, o_mla, o_f, o_b, sg, wts, tm):
    bsz, n, _ = x.shape
    g_mix, w_gate, g_hg, hmean, w_br_mla, w_br_hg, w_out = wts

    def tok(width):
        return pl.BlockSpec((1, tm, width), lambda b, i: (b, i, 0))

    def mod(col):
        return pl.BlockSpec((1, 1, D_MODEL), lambda b, i: (mod_row(b), 0, col))

    return pl.pallas_call(
        _merge_kernel,
        grid=(bsz, n // tm),
        in_specs=[tok(D_MODEL), mod(0), mod(1), mod(2),
                  tok(MLA_W), tok(HG_W), tok(HG_W), tok(HG_W),
                  _const_spec((1, D_MODEL)), _const_spec(w_gate.shape),
                  _const_spec((1, HG_W)), _const_spec(hmean.shape), _const_spec(w_br_mla.shape),
                  _const_spec(w_br_hg.shape), _const_spec(w_out.shape)],
        out_specs=tok(D_MODEL),
        out_shape=jax.ShapeDtypeStruct((bsz, n, D_MODEL), F32),
        compiler_params=_cparams(("parallel", "parallel")),
        name="merge",
    )(x, mods, mods, mods, o_mla, o_f, o_b, sg, g_mix, w_gate, g_hg, hmean, w_br_mla, w_br_hg, w_out)


FFN_CHUNK = 1024


def _ffn_kernel(x_ref, sh_ref, sc_ref, ga_ref, gff_ref, w1_ref, w2_ref, gfin_ref, y_ref):
    tm = x_ref.shape[1]
    part_rows = TOKEN_TILE // 2
    parts = [slice(r, r + part_rows) for r in range(0, tm, part_rows)]
    h, acts = {}, {i: [] for i in range(len(parts))}

    def up_stage(i, c0):
        def produce():
            if i not in h:
                h[i] = (_rms(x_ref[0, parts[i]], gff_ref[...]) * (1.0 + sc_ref[0]) + sh_ref[0]).astype(BF16)
            return _dot(h[i], w1_ref[:, c0:c0 + FFN_CHUNK])

        def consume(a):
            a = jnp.maximum(a, 0.0)
            acts[i].append((a * a).astype(BF16))
        return produce, consume

    def down_stage(i):
        def consume(f):
            y_ref[0, parts[i]] = _rms(x_ref[0, parts[i]] + ga_ref[0] * f, gfin_ref[...])
        return (lambda: _dot(jnp.concatenate(acts[i], axis=1), w2_ref[...])), consume

    stages = []
    for g in range(0, len(parts), 2):
        group = range(g, min(g + 2, len(parts)))
        stages += [up_stage(i, c0) for c0 in range(0, D_FF, FFN_CHUNK) for i in group]
        stages += [down_stage(i) for i in group]
    _run_staggered(stages)


def _ffn(x1, mods, mod_row, wts, tm):
    bsz, n, _ = x1.shape
    g_ff, w_ff1, w_ff2, g_final = wts

    def mod(col):
        return pl.BlockSpec((1, 1, D_MODEL), lambda b, i: (mod_row(b), 0, col))

    tok = pl.BlockSpec((1, tm, D_MODEL), lambda b, i: (b, i, 0))
    return pl.pallas_call(
        _ffn_kernel,
        grid=(bsz, n // tm),
        in_specs=[tok, mod(3), mod(4), mod(5), _const_spec((1, D_MODEL)),
                  _const_spec(w_ff1.shape), _const_spec(w_ff2.shape), _const_spec((1, D_MODEL))],
        out_specs=tok,
        out_shape=jax.ShapeDtypeStruct((bsz, n, D_MODEL), F32),
        compiler_params=_cparams(("parallel", "parallel")),
        name="ffn",
    )(x1, mods, mods, mods, g_ff, w_ff1, w_ff2, g_final)


def _prep_weights(w_in, w_uq, w_ukv):
    w_lat = jnp.concatenate([w_in[:, :_LAT_W].astype(BF16),
                             jnp.pad(w_in[:, _LAT_W:_LAT_W + QK_ROPE].astype(BF16), ((0, 0), ROPE_PAD))], axis=1)
    mix0 = _LAT_W + QK_ROPE
    w_mix = w_in[:, mix0:mix0 + _MIX_OFFS[-1]].astype(BF16)
    w_gate = w_in[:, mix0 + _MIX_OFFS[-1]:].astype(BF16)
    w_in_p = (w_lat, w_mix, w_gate)
    w_uq_p = jnp.pad(w_uq.reshape(Q_LORA, MLA_HEADS, QK_NOPE + QK_ROPE).astype(BF16),
                     ((0, 0), (0, 0), (0, ROPE_PAD[1]))).reshape(Q_LORA, QK_PAD_W)
    assert QK_NOPE + V_HEAD == HEAD_PAD
    return w_in_p, w_uq_p, w_ukv.astype(BF16)


def _rope_tables(n):
    rows = n // GRID_W
    row = jnp.repeat(jnp.arange(rows, dtype=F32), GRID_W)
    col = jnp.tile(jnp.arange(GRID_W, dtype=F32), rows)
    half = QK_ROPE // 2
    inv = ROPE_BASE ** (-jnp.arange(0, half, 2, dtype=F32) / half)
    ang = jnp.concatenate([row[:, None] * inv, col[:, None] * inv], axis=-1)
    cos = jnp.repeat(jnp.cos(ang), 2, axis=-1)
    sin = jnp.repeat(jnp.sin(ang), 2, axis=-1)
    cos = jnp.pad(cos, ((0, 0), ROPE_PAD), constant_values=1.0)
    sin = jnp.pad(sin, ((0, 0), ROPE_PAD))
    return cos, sin


def _trunk(x, mods, mod_row, wts, rope_tabs, ctx):
    inproj_w, merge_w, ffn_w = wts
    bsz, n, _ = x.shape
    if ctx is None:
        cache, past = None, 0
        s0 = None
    else:
        ckv_c, krope_c, state_c = ctx
        cache, past = (ckv_c, jnp.pad(krope_c, ((0, 0), (0, 0), ROPE_PAD))), ckv_c.shape[1]
        s0 = jnp.stack([_state_to_pairs(state_c[:, 0]), _state_to_pairs(state_c[:, 1])], axis=1)
    outs = _inproj(x, mods, mod_row, inproj_w, rope_tabs, ctx is None, min(n, TOKEN_TILE), cache, ctx is None)
    q, k, v, qh, kf, lff, kb, lfb, vh, sg = outs[:N_INPROJ_OUT]
    if ctx is None:
        ckv, krope = outs[N_INPROJ_OUT:]
    bk = next(b for b in ATTN_KV_TILES if (n + past) % b == 0)
    o_mla = _attention(q, k, v, min(n, ATTN_Q_TILE), bk)
    o_f, o_b, *state = _hgrn(qh, kf, lff, kb, lfb, vh, s0, ctx is None)
    per_token = (x, o_mla, o_f, o_b, sg)
    if ctx is None:
        per_token = tuple(a.reshape(1, bsz * n, a.shape[-1]) for a in per_token)
    tm = min(per_token[0].shape[1], TOKEN_TILE)
    x1 = _merge(*per_token[:1], mods, mod_row, *per_token[1:], merge_w, tm)
    y = _ffn(x1, mods, mod_row, ffn_w, min(x1.shape[1], FFN_TILE)).reshape(bsz, n, D_MODEL)
    if ctx is None:
        return y, (ckv, krope, state[0])
    return y, None


def kernel(x_prompt, x_sample, cache_ckv, cache_krope, state_hgrn, c, c_ctx, w_ada, b_ada, g_norm_mix, g_norm_ff, w_in, g_q_norm, w_uq, g_kv_norm, w_ukv, g_hg_norm, hg_lb_logits, w_br_mla, w_br_hg, w_out, w_ff1, w_ff2, g_final):
    assert w_in.shape[0] == 1, "single-layer trunk"
    dec_b = c.shape[0]
    assert dec_b + 1 <= SUBLANES
    cc = jnp.concatenate([c, c_ctx[None, :], jnp.zeros((SUBLANES - dec_b - 1, D_MODEL), F32)], axis=0)
    mods = _ada(cc, w_ada[0], b_ada[0][None, :]).reshape(SUBLANES, 1, 6 * D_MODEL)

    (w_lat, w_mix, w_gate), w_uq_p, w_ukv_p = _prep_weights(w_in[0], w_uq[0], w_ukv[0])
    g_mix = g_norm_mix[0][None]
    inproj_w = ((w_lat, w_mix), g_mix, g_q_norm[0][None], w_uq_p, g_kv_norm[0][None], w_ukv_p, hg_lb_logits)
    head_id = np.arange(HG_W) // HG_DV
    hmean = jnp.asarray((head_id[:, None] == head_id[None, :]).astype(np.float32) / HG_DV, BF16)
    merge_w = (g_mix, w_gate, g_hg_norm[0][None], hmean, w_br_mla[0].astype(BF16), w_br_hg[0].astype(BF16),
               w_out[0].astype(BF16))
    ffn_w = (g_norm_ff[0][None], w_ff1[0].astype(BF16), w_ff2[0].astype(BF16), g_final[None])
    wts = (inproj_w, merge_w, ffn_w)

    y_prompt, (ckv, krope, state) = _trunk(x_prompt, mods, lambda b: dec_b, wts, None, None)
    rope_tabs = _rope_tables(x_sample.shape[1])
    y_sample, _ = _trunk(x_sample, mods, lambda b: b, wts, rope_tabs,
                         (cache_ckv[:, 0], cache_krope[:, 0], state_hgrn[:, 0]))
    return (y_prompt, y_sample, ckv[:, None], krope[:, None], state[:, None])
```
